```python
import math
import functools
import jax
import jax.numpy as jnp
from jax import lax
import numpy as np

D_MODEL = 1024
BATCH = 8
SEQ = 2048
DEPTH = 4
DEC_BATCH = 128
DEC_SEQ = 1
PAST_LEN = 2048
PAGE_SIZE = 128

N_BRANCH = 4
BR_W = D_MODEL // N_BRANCH
GLA_H = 4
GLA_DK = BR_W // (2 * GLA_H)
GLA_DV = BR_W // GLA_H
GLA_RANK = 16
GLA_TAU = 16.0
GLA_CHUNK = 64
ATT_H = 4
HD = BR_W // ATT_H
ROT = HD // 4
IDX_H = 8
IDX_D = 32
IDX_ROT = IDX_D // 4
TOPK_MAX = 256
DSA_QBLK = 64
ROPE_THETA = 500000.0
S5_CH = 16
S5_G = BR_W // S5_CH
S5_P = 64
RW_H = 4
RW_N = BR_W // RW_H
RW_WR = 32
RW_AR = 32
RW_GR = 64
D_FF = 4 * D_MODEL
EPS = 1e-6
RW_GN_EPS = 64e-5

GLA_SIZES = (GLA_H * GLA_DK, GLA_H * GLA_DK, BR_W, GLA_RANK, BR_W)
DSA_SIZES = (BR_W, BR_W, BR_W, IDX_H * IDX_D, IDX_D, IDX_H)
RW_SIZES = (BR_W, BR_W, BR_W, RW_WR, RW_AR, RW_GR)
RW_COLS = sum(RW_SIZES)
IN_SIZES = GLA_SIZES + DSA_SIZES + (BR_W, RW_COLS) + (D_MODEL,) * N_BRANCH
IN_COLS = sum(IN_SIZES)

kernel_name = 'hybrid_gated_branch_decoder_step'

F32 = jnp.float32


def _split(x, sizes):
    return jnp.split(x, [int(i) for i in np.cumsum(sizes)[:-1]], axis=-1)


def _rms(x, g):
    x32 = x.astype(F32)
    return (x32 * lax.rsqrt(jnp.mean(x32 * x32, -1, keepdims=True) + EPS) * g).astype(x.dtype)


def _rope(x, pos, rot):
    half = rot // 2
    freq = ROPE_THETA ** (-jnp.arange(half, dtype=F32) * (2.0 / rot))
    ang = pos.astype(F32)[:, None] * freq
    cos = jnp.cos(ang)[:, None, :]
    sin = jnp.sin(ang)[:, None, :]
    x32 = x.astype(F32)
    x1, x2, rest = x32[..., :half], x32[..., half:rot], x32[..., rot:]
    return jnp.concatenate([x1 * cos - x2 * sin, x2 * cos + x1 * sin, rest], -1).astype(x.dtype)


def _cmul(ar, ai, br, bi):
    return ar * br - ai * bi, ar * bi + ai * br


def _gla_recurrence(q, k, v, log_a, s0):
    B, L = q.shape[:2]
    C = math.gcd(L, GLA_CHUNK)
    n = L // C
    chunks = lambda t: t.reshape(B, n, C, *t.shape[2:]).swapaxes(0, 1)
    causal = jnp.tril(jnp.ones((C, C), dtype=bool))[None, :, :, None, None]

    def step(S, inp):
        qc, kc, vc, ac = inp
        b = jnp.cumsum(ac, axis=1)
        inter = jnp.einsum('bthk,bhkv->bthv', qc * jnp.exp(b), S)
        diff = jnp.where(causal, b[:, :, None] - b[:, None, :], -jnp.inf)
        att = jnp.einsum('bthk,bshk,btshk->btsh', qc, kc, jnp.exp(diff))
        o = inter + jnp.einsum('btsh,bshv->bthv', att, vc)
        bl = b[:, -1]
        S = jnp.exp(bl)[..., None] * S + jnp.einsum('bshk,bshv->bhkv', kc * jnp.exp(bl[:, None] - b), vc)
        return S, o

    S, o = lax.scan(step, s0, tuple(map(chunks, (q, k, v, log_a))))
    return o.swapaxes(0, 1).reshape(B, L, GLA_H, GLA_DV), S


def _gla(cols, s0, P):
    q, k, v, a_lo, r = cols
    B, L = q.shape[:2]
    q = q.reshape(B, L, GLA_H, GLA_DK).astype(F32) * GLA_DK ** -0.5
    k = k.reshape(B, L, GLA_H, GLA_DK).astype(F32)
    v = v.reshape(B, L, GLA_H, GLA_DV).astype(F32)
    log_a = jax.nn.log_sigmoid((a_lo @ P['gla_a2'] + P['gla_ab']).astype(F32)) / GLA_TAU
    o, S = _gla_recurrence(q, k, v, log_a.reshape(B, L, GLA_H, GLA_DK), s0.astype(F32))
    o = _rms(o, P['gla_ng']).reshape(B, L, BR_W)
    return o * jax.nn.silu(r.astype(F32)), S


def _dsa_project(cols, pos, P):
    q, k, v, qi, ki, wi = cols
    B, L = q.shape[:2]
    q = _rope(_rms(q.reshape(B, L, ATT_H, HD), P['att_qg']), pos, ROT)
    k = _rope(_rms(k.reshape(B, L, ATT_H, HD), P['att_kg']), pos, ROT)
    v = v.reshape(B, L, ATT_H, HD)
    qi = _rope(qi.reshape(B, L, IDX_H, IDX_D), pos, IDX_ROT)
    ki = _rope(ki.reshape(B, L, 1, IDX_D), pos, IDX_ROT)[:, :, 0]
    return q, k, v, qi, ki, wi * IDX_H ** -0.5


def _index_scores(qi, ki, wi, tpos, spos):
    s = jnp.einsum('bthd,bsd->bths', qi.astype(F32), ki.astype(F32)) * IDX_D ** -0.5
    I = jnp.einsum('bths,bth->bts', jax.nn.relu(s), wi.astype(F32))
    return jnp.where(spos[None, None, :] <= tpos[None, :, None], I, -jnp.inf)


def _sparse_softmax(q, ks, vs, ok):
    logits = jnp.einsum('bthd,btkhd->bthk', q.astype(F32), ks.astype(F32)) * HD ** -0.5
    p = jax.nn.softmax(jnp.where(ok[:, :, None, :], logits, -jnp.inf), axis=-1)
    return jnp.einsum('bthk,btkhd->bthd', p, vs.astype(F32))


def _dsa_prompt(q, k, v, qi, ki, wi):
    B, S = q.shape[:2]
    topk = min(TOPK_MAX, S // 4)
    spos = jnp.arange(S)
    take = jax.vmap(lambda a, j: a[j])

    def block(t0):
        sl = lambda a: lax.dynamic_slice_in_dim(a, t0, DSA_QBLK, axis=1)
        I = _index_scores(sl(qi), ki, sl(wi), t0 + jnp.arange(DSA_QBLK), spos)
        val, idx = lax.top_k(I, topk)
        return _sparse_softmax(sl(q), take(k, idx), take(v, idx), jnp.isfinite(val))

    out = lax.map(block, jnp.arange(0, S, DSA_QBLK))
    return out.swapaxes(0, 1).reshape(B, S, BR_W)


def _dsa_sample(q, k, v, qi, ki, wi, ck, cv, ci, page_table):
    N, T = q.shape[:2]
    past = page_table.shape[1] * PAGE_SIZE
    L = past + T
    topk = min(TOPK_MAX, L // 4)
    ki_all = jnp.concatenate([ci[page_table].reshape(N, past, IDX_D).astype(ki.dtype), ki], 1)
    I = _index_scores(qi, ki_all, wi, past + jnp.arange(T), jnp.arange(L))
    val, idx = lax.top_k(I, topk)
    pidx = jnp.minimum(idx, past - 1)
    phys = jnp.take_along_axis(page_table, (pidx // PAGE_SIZE).reshape(N, -1), axis=1).reshape(idx.shape)
    slot = pidx % PAGE_SIZE
    cur = jnp.clip(idx - past, 0, T - 1)
    take = jax.vmap(lambda a, j: a[j])
    is_past = (idx < past)[..., None, None]
    ks = jnp.where(is_past, ck[phys, slot], take(k, cur))
    vs = jnp.where(is_past, cv[phys, slot], take(v, cur))
    return _sparse_softmax(q, ks, vs, jnp.isfinite(val)).reshape(N, T, BR_W)


def _s5(u, x0r, x0i, P):
    B, L = u.shape[:2]
    u = u.reshape(B, L, S5_G, S5_CH).astype(F32)
    dt = jnp.exp(P['s5_log_dt'].astype(F32))[:, None]
    lr = jnp.minimum(P['s5_a_re'].astype(F32), -1e-4)
    li = P['s5_a_im'].astype(F32)
    mag = jnp.exp(lr * dt)
    abr, abi = mag * jnp.cos(li * dt), mag * jnp.sin(li * dt)
    den = lr * lr + li * li
    fr = ((abr - 1.0) * lr + abi * li) / den
    fi = (abi * lr - (abr - 1.0) * li) / den
    bbr, bbi = _cmul(fr[..., None], fi[..., None], P['s5_b_re'], P['s5_b_im'])
    bur = jnp.einsum('blgh,gph->blgp', u, bbr)
    bui = jnp.einsum('blgh,gph->blgp', u, bbi)
    ar = jnp.broadcast_to(abr, bur.shape)
    ai = jnp.broadcast_to(abi, bui.shape)

    def comb(e1, e2):
        a1r, a1i, b1r, b1i = e1
        a2r, a2i, b2r, b2i = e2
        nar, nai = _cmul(a2r, a2i, a1r, a1i)
        nbr, nbi = _cmul(a2r, a2i, b1r, b1i)
        return nar, nai, nbr + b2r, nbi + b2i

    powr, powi, xr, xi = lax.associative_scan(comb, (ar, ai, bur, bui), axis=1)
    sr, si = _cmul(powr, powi, x0r.astype(F32)[:, None], x0i.astype(F32)[:, None])
    xr, xi = xr + sr, xi + si
    y = (jnp.einsum('blgp,ghp->blgh', xr, P['s5_c_re']) - jnp.einsum('blgp,ghp->blgh', xi, P['s5_c_im'])
         + P['s5_d'] * u)
    z = jax.nn.gelu(y.reshape(B, L, BR_W))
    return z * jax.nn.sigmoid(z @ P['s5_glu_w'] + P['s5_glu_b']), xr[:, -1], xi[:, -1]


def _rwkv(p_rows, prev, s0, P):
    B, L = p_rows.shape[:2]
    p = p_rows.astype(F32)
    shifted = jnp.concatenate([prev.astype(F32)[:, None], p[:, :-1]], 1)
    xs = p + (shifted - p) * P['rw_mu']
    r, k, v, w_lo, a_lo, g_lo = _split(xs, RW_SIZES)
    w = -jax.nn.softplus(-(P['rw_w0'] + jnp.tanh(w_lo) @ P['rw_w2'])) - 0.5
    decay = jnp.exp(-jnp.exp(w))
    a = jax.nn.sigmoid(P['rw_a0'] + a_lo @ P['rw_a2'])
    g = jax.nn.sigmoid(g_lo) @ P['rw_g2']
    heads = lambda t: t.reshape(*t.shape[:-1], RW_H, RW_N)
    kk = heads(k * P['rw_kk'])
    kk = kk * lax.rsqrt(jnp.sum(kk * kk, -1, keepdims=True) + EPS)
    k = k * (1.0 + (a - 1.0) * P['rw_ka'])
    r, k, v, a, decay = map(heads, (r, k, v, a, decay))

    def step(S, inp):
        r_t, k_t, v_t, kk_t, a_t, w_t = inp
        sa = jnp.einsum('bhvk,bhk->bhv', S, -kk_t)
        S = (S * w_t[:, :, None, :] + sa[..., None] * (kk_t * a_t)[:, :, None, :]
             + v_t[..., None] * k_t[:, :, None, :])
        return S, jnp.einsum('bhvk,bhk->bhv', S, r_t)

    tm = lambda t: t.swapaxes(0, 1)
    S, y = lax.scan(step, s0.astype(F32), tuple(map(tm, (r, k, v, kk, a, decay))))
    y = tm(y)
    mu = jnp.mean(y, -1, keepdims=True)
    var = jnp.mean(jnp.square(y - mu), -1, keepdims=True)
    y = (y - mu) * lax.rsqrt(var + RW_GN_EPS) * heads(P['rw_ng'])
    y = y + jnp.sum(r * k * heads(P['rw_rk']), -1, keepdims=True) * v
    return y.reshape(B, L, BR_W) * g, S, p_rows[:, -1]


def _mix(h, pos, states, attend, P):
    s_gla, s5_re, s5_im, s_rw, s_shift = states
    parts = _split(h @ P['w_in'], IN_SIZES)
    g0 = len(GLA_SIZES)
    d0 = g0 + len(DSA_SIZES)
    o_gla, s_gla = _gla(parts[:g0], s_gla, P)
    q, k, v, qi, ki, wi = _dsa_project(parts[g0:d0], pos, P)
    o_att = attend(q, k, v, qi, ki, wi)
    o_s5, s5_re, s5_im = _s5(parts[d0], s5_re, s5_im, P)
    o_rw, s_rw, s_shift = _rwkv(parts[d0 + 1], s_shift, s_rw, P)
    outs = (o_gla, o_att, o_s5, o_rw)
    gates = parts[d0 + 2:]
    merged = jax.nn.sigmoid(gates[0]) * (outs[0].astype(h.dtype) @ P['w_br'][0])
    for b in range(1, N_BRANCH):
        merged = merged + jax.nn.sigmoid(gates[b]) * (outs[b].astype(h.dtype) @ P['w_br'][b])
    return merged @ P['w_o'], (k, v, ki, s_gla, s5_re, s5_im, s_rw, s_shift)


def _layer(x, c, P, pos, states, attend):
    mod = (c @ P['ada_w'] + P['ada_b'])[:, None, :]
    sh1, sc1, g1, sh2, sc2, g2 = jnp.split(mod, 6, axis=-1)
    h = _rms(x, P['norm1_g']) * (1.0 + sc1) + sh1
    m, new = _mix(h, pos, states, attend, P)
    x = x + g1 * m
    h = _rms(x, P['norm2_g']) * (1.0 + sc2) + sh2
    x = x + g2 * (jnp.square(jax.nn.relu(h @ P['w_ff1'])) @ P['w_ff2'])
    return x, new


def _stack(lst, i, dtype):
    return jnp.stack([s[i] for s in lst]).astype(dtype)


def setup_inputs(seed: int = 0) -> dict:
    key = jax.random.key(seed)
    ks = iter(jax.random.split(key, 64))

    def nrm(shape, scale=1.0):
        return jax.random.normal(next(ks), shape, F32) * scale

    def unif(shape, lo, hi):
        return jax.random.uniform(next(ks), shape, F32, lo, hi)

    n_pages = PAST_LEN // PAGE_SIZE
    n_used = DEC_BATCH * n_pages
    n_phys = (5 * n_used + 3) // 4
    page_table = jax.random.permutation(next(ks), n_phys)[:n_used].reshape(DEC_BATCH, n_pages).astype(jnp.int32)
    Dp = DEPTH
    return {
        'x_prompt': nrm((BATCH, SEQ, D_MODEL)),
        'x_sample': nrm((DEC_BATCH, DEC_SEQ, D_MODEL)),
        'c_prompt': nrm((BATCH, D_MODEL)),
        'c_sample': nrm((DEC_BATCH, D_MODEL)),
        'cache_k': nrm((Dp, n_phys, PAGE_SIZE, ATT_H, HD)),
        'cache_v': nrm((Dp, n_phys, PAGE_SIZE, ATT_H, HD)),
        'cache_idx': nrm((Dp, n_phys, PAGE_SIZE, IDX_D)),
        'state_gla': nrm((Dp, DEC_BATCH, GLA_H, GLA_DK, GLA_DV), 0.5),
        'state_s5_re': nrm((Dp, DEC_BATCH, S5_G, S5_P), 0.1),
        'state_s5_im': nrm((Dp, DEC_BATCH, S5_G, S5_P), 0.1),
        'state_rwkv': nrm((Dp, DEC_BATCH, RW_H, RW_N, RW_N), 0.3),
        'state_shift': nrm((Dp, DEC_BATCH, RW_COLS)),
        'page_table': page_table,
        'ada_w': nrm((Dp, D_MODEL, 6 * D_MODEL), 0.5 * D_MODEL ** -0.5),
        'ada_b': nrm((Dp, 6 * D_MODEL), 0.02),
        'norm1_g': 1.0 + nrm((Dp, D_MODEL), 0.02),
        'norm2_g': 1.0 + nrm((Dp, D_MODEL), 0.02),
        'w_in': nrm((Dp, D_MODEL, IN_COLS), D_MODEL ** -0.5),
        'gla_a2': nrm((Dp, GLA_RANK, GLA_H * GLA_DK), GLA_RANK ** -0.5),
        'gla_ab': nrm((Dp, GLA_H * GLA_DK), 0.1),
        'gla_ng': 1.0 + nrm((Dp, GLA_DV), 0.02),
        'att_qg': 1.0 + nrm((Dp, HD), 0.02),
        'att_kg': 1.0 + nrm((Dp, HD), 0.02),
        's5_a_re': -0.5 + nrm((Dp, S5_G, S5_P), 0.01),
        's5_a_im': math.pi * jnp.arange(S5_P, dtype=F32) + nrm((Dp, S5_G, S5_P), 0.01),
        's5_log_dt': unif((Dp, S5_G), math.log(1e-3), math.log(1e-1)),
        's5_b_re': nrm((Dp, S5_G, S5_P, S5_CH), (2 * S5_CH) ** -0.5),
        's5_b_im': nrm((Dp, S5_G, S5_P, S5_CH), (2 * S5_CH) ** -0.5),
        's5_c_re': nrm((Dp, S5_G, S5_CH, S5_P), (2 * S5_P) ** -0.5),
        's5_c_im': nrm((Dp, S5_G, S5_CH, S5_P), (2 * S5_P) ** -0.5),
        's5_d': nrm((Dp, S5_G, S5_CH)),
        's5_glu_w': nrm((Dp, BR_W, BR_W), BR_W ** -0.5),
        's5_glu_b': nrm((Dp, BR_W), 0.02),
        'rw_mu': unif((Dp, RW_COLS), 0.0, 1.0),
        'rw_w0': unif((Dp, BR_W), -4.0, 1.0),
        'rw_w2': nrm((Dp, RW_WR, BR_W), 0.1 * RW_WR ** -0.5),
        'rw_a0': nrm((Dp, BR_W), 0.1),
        'rw_a2': nrm((Dp, RW_AR, BR_W), 0.3 * RW_AR ** -0.5),
        'rw_g2': nrm((Dp, RW_GR, BR_W), RW_GR ** -0.5),
        'rw_kk': 0.85 + nrm((Dp, BR_W), 0.05),
        'rw_ka': 1.0 + nrm((Dp, BR_W), 0.05),
        'rw_rk': nrm((Dp, BR_W), 0.1),
        'rw_ng': 1.0 + nrm((Dp, BR_W), 0.02),
        'w_br': nrm((Dp, N_BRANCH, BR_W, D_MODEL), BR_W ** -0.5),
        'w_o': nrm((Dp, D_MODEL, D_MODEL), D_MODEL ** -0.5),
        'w_ff1': nrm((Dp, D_MODEL, D_FF), D_MODEL ** -0.5),
        'w_ff2': nrm((Dp, D_FF, D_MODEL), D_FF ** -0.5),
    }


def reference(x_prompt, x_sample, c_prompt, c_sample, cache_k, cache_v, cache_idx, state_gla,
              state_s5_re, state_s5_im, state_rwkv, state_shift, page_table, ada_w, ada_b, norm1_g,
              norm2_g, w_in, gla_a2, gla_ab, gla_ng, att_qg, att_kg, s5_a_re, s5_a_im, s5_log_dt,
              s5_b_re, s5_b_im, s5_c_re, s5_c_im, s5_d, s5_glu_w, s5_glu_b, rw_mu, rw_w0, rw_w2,
              rw_a0, rw_a2, rw_g2, rw_kk, rw_ka, rw_rk, rw_ng, w_br, w_o, w_ff1, w_ff2):
    B, S = x_prompt.shape[:2]
    N, T = x_sample.shape[:2]
    dtp, dts = x_prompt.dtype, x_sample.dtype
    pos_p = jnp.arange(S)
    pos_s = PAST_LEN + jnp.arange(T)
    yp, ys = x_prompt, x_sample
    new_p, new_s = [], []
    for l in range(DEPTH):
        P = dict(ada_w=ada_w[l], ada_b=ada_b[l], norm1_g=norm1_g[l], norm2_g=norm2_g[l], w_in=w_in[l],
                 gla_a2=gla_a2[l], gla_ab=gla_ab[l], gla_ng=gla_ng[l], att_qg=att_qg[l], att_kg=att_kg[l],
                 s5_a_re=s5_a_re[l], s5_a_im=s5_a_im[l], s5_log_dt=s5_log_dt[l], s5_b_re=s5_b_re[l],
                 s5_b_im=s5_b_im[l], s5_c_re=s5_c_re[l], s5_c_im=s5_c_im[l], s5_d=s5_d[l],
                 s5_glu_w=s5_glu_w[l], s5_glu_b=s5_glu_b[l], rw_mu=rw_mu[l], rw_w0=rw_w0[l],
                 rw_w2=rw_w2[l], rw_a0=rw_a0[l], rw_a2=rw_a2[l], rw_g2=rw_g2[l], rw_kk=rw_kk[l],
                 rw_ka=rw_ka[l], rw_rk=rw_rk[l], rw_ng=rw_ng[l], w_br=w_br[l], w_o=w_o[l],
                 w_ff1=w_ff1[l], w_ff2=w_ff2[l])
        init_p = (jnp.zeros((B, GLA_H, GLA_DK, GLA_DV), F32), jnp.zeros((B, S5_G, S5_P), F32),
                  jnp.zeros((B, S5_G, S5_P), F32), jnp.zeros((B, RW_H, RW_N, RW_N), F32),
                  jnp.zeros((B, RW_COLS), F32))
        yp, st = _layer(yp, c_prompt, P, pos_p, init_p, _dsa_prompt)
        new_p.append(st)
        init_s = (state_gla[l], state_s5_re[l], state_s5_im[l], state_rwkv[l], state_shift[l])
        attend = functools.partial(_dsa_sample, ck=cache_k[l], cv=cache_v[l], ci=cache_idx[l],
                                   page_table=page_table)
        ys, st = _layer(ys, c_sample, P, pos_s, init_s, attend)
        new_s.append(st)
    return (yp, ys,
            _stack(new_p, 0, dtp), _stack(new_p, 1, dtp), _stack(new_p, 2, dtp), _stack(new_p, 3, dtp),
            _stack(new_p, 4, dtp), _stack(new_p, 5, dtp), _stack(new_p, 6, dtp), _stack(new_p, 7, dtp),
            _stack(new_s, 0, dts), _stack(new_s, 1, dts), _stack(new_s, 2, dts), _stack(new_s, 3, dts),
            _stack(new_s, 4, dts), _stack(new_s, 5, dts), _stack(new_s, 6, dts), _stack(new_s, 7, dts))
```

```python
import functools
import math

import numpy as np
import jax
import jax.numpy as jnp
from jax import lax
from jax.experimental import pallas as pl
from jax.experimental.pallas import tpu as pltpu

F32 = jnp.float32
BF16 = jnp.bfloat16
I32 = jnp.int32

D_MODEL = 1024
BR_W = 256
GLA_H, GLA_DK, GLA_DV, GLA_RANK, GLA_TAU = 4, 32, 64, 16, 16.0
ATT_H, HD, ROT = 4, 64, 16
IDX_H, IDX_D, IDX_ROT = 8, 32, 8
TOPK_MAX = 256
ROPE_THETA = 500000.0
S5_G, S5_P, S5_CH = 16, 64, 16
S5_N = S5_G * S5_P
RW_H, RW_N, RW_WR, RW_AR, RW_GR = 4, 64, 32, 32, 64
RW_COLS = 896
D_FF = 4096
EPS = 1e-6
RW_GN_EPS = 64e-5
PAGE_SIZE = 128
INT_MIN = -(2 ** 31)
SAMPLE_PAD = 8

C_GV, C_GR, C_GQK = 0, 256, 512
C_DQ, C_DK, C_DV, C_DQI = 768, 1024, 1280, 1536
C_S5 = 1792
C_RW = 2048
C_RWLO = 2816
C_GA = 2944
C_DKW = 3072
NP_COLS = 3200
VMEM_LIMIT = 56 * 1024 * 1024


def _cparams(n_axes):
    return pltpu.CompilerParams(dimension_semantics=("arbitrary",) * n_axes,
                                vmem_limit_bytes=VMEM_LIMIT)


def _split_dot(x, w, terms):
    acc = None
    r = x
    for i in range(terms):
        hi = r.astype(BF16)
        d = jnp.dot(hi, w, preferred_element_type=F32)
        acc = d if acc is None else acc + d
        if i + 1 < terms:
            r = r - hi.astype(F32)
    return acc


def _dot3(x, w_hi, w_lo):
    x_hi = x.astype(BF16)
    x_lo = (x - x_hi.astype(F32)).astype(BF16)
    return (jnp.dot(x_hi, w_hi, preferred_element_type=F32)
            + jnp.dot(x_hi, w_lo, preferred_element_type=F32)
            + jnp.dot(x_lo, w_hi, preferred_element_type=F32))


def _hilo(w):
    hi = w.astype(BF16)
    return hi, (w - hi.astype(F32)).astype(BF16)


def _sigmoid(x):
    return 1.0 / (1.0 + jnp.exp(-x))


def _softplus(x):
    return jnp.maximum(x, 0.0) + jnp.log1p(jnp.exp(-jnp.abs(x)))


def _norm_mod(x, g, sc, sh):
    ms = jnp.mean(x * x, axis=-1, keepdims=True)
    return (x * lax.rsqrt(ms + EPS) * g) * (1.0 + sc) + sh


def _mod_spec(mod, j, tm, seq):
    if mod.ndim == 3:
        return pl.BlockSpec((None, 1, D_MODEL), lambda i: ((i * tm) // seq, 0, j))
    return pl.BlockSpec((tm, D_MODEL), lambda i: (i, j))


def _const_spec(a):
    nd = a.ndim
    return pl.BlockSpec(a.shape, lambda *_: (0,) * nd)


def _mod_kernel(c_ref, w_ref, b_ref, o_ref):
    o_ref[...] = jnp.dot(c_ref[...], w_ref[...].astype(BF16), preferred_element_type=F32) + b_ref[...]


def _modulation(c_all, ada_w, ada_b):
    depth = ada_w.shape[0]
    rows = c_all.shape[0]
    tn = 1536
    return pl.pallas_call(
        _mod_kernel,
        grid=(depth, 6 * D_MODEL // tn),
        in_specs=[pl.BlockSpec((rows, D_MODEL), lambda l, j: (0, 0)),
                  pl.BlockSpec((None, D_MODEL, tn), lambda l, j: (l, 0, j)),
                  pl.BlockSpec((None, 1, tn), lambda l, j: (l, 0, j))],
        out_specs=pl.BlockSpec((None, rows, tn), lambda l, j: (l, 0, j)),
        out_shape=jax.ShapeDtypeStruct((depth, rows, 6 * D_MODEL), F32),
        compiler_params=_cparams(2), name="modulation",
    )(c_all.astype(BF16), ada_w, ada_b.reshape(depth, 1, 6 * D_MODEL))


def _inproj_kernel(x_ref, sc_ref, sh_ref, g_ref, w_ref, o_ref):
    h = _norm_mod(x_ref[...], g_ref[...], sc_ref[...], sh_ref[...])
    o_ref[...] = jnp.dot(h.astype(BF16), w_ref[...], preferred_element_type=F32)


def _inproj(x, mod, g, w_mix, tm, seq):
    rows = x.shape[0]
    return pl.pallas_call(
        _inproj_kernel,
        grid=(rows // tm,),
        in_specs=[pl.BlockSpec((tm, D_MODEL), lambda i: (i, 0)),
                  _mod_spec(mod, 1, tm, seq), _mod_spec(mod, 0, tm, seq),
                  _const_spec(g), _const_spec(w_mix)],
        out_specs=pl.BlockSpec((tm, NP_COLS), lambda i: (i, 0)),
        out_shape=jax.ShapeDtypeStruct((rows, NP_COLS), F32),
        compiler_params=_cparams(1), name="inproj",
    )(x, mod, mod, g, w_mix)


def _rope_apply(x, cos, sn, left, shift):
    n = x.shape[-1]
    rot = jnp.where(left > 0.0, pltpu.roll(x, n - shift, 1), pltpu.roll(x, shift, 1))
    return x * cos + rot * sn


def _dsa_prep_kernel(q_ref, k_ref, qi_ref, kw_ref, cq_ref, sq_ref, ci_ref, si_ref, ckw_ref, skw_ref,
                     lq_ref, li_ref, lkw_ref, qg_ref, kg_ref, wavg_ref,
                     qn_ref, kn_ref, qir_ref, kwr_ref):
    wavg = wavg_ref[...]

    def headnorm(x, g):
        ms = _split_dot(x * x, wavg, 3)
        return x * lax.rsqrt(ms + EPS) * g

    cq, sq, lq = cq_ref[...], sq_ref[...], lq_ref[...]
    qn_ref[...] = _rope_apply(headnorm(q_ref[...], qg_ref[...]), cq, sq, lq, ROT // 2)
    kn_ref[...] = _rope_apply(headnorm(k_ref[...], kg_ref[...]), cq, sq, lq, ROT // 2)
    qir_ref[...] = _rope_apply(qi_ref[...], ci_ref[...], si_ref[...], li_ref[...], IDX_ROT // 2)
    kwr_ref[...] = _rope_apply(kw_ref[...], ckw_ref[...], skw_ref[...], lkw_ref[...], IDX_ROT // 2)


def _dsa_prep(P, tabs, consts, qg, kg, tm, seq):
    rows = P.shape[0]
    nt = seq // tm if tabs["periodic"] else None

    def tab_spec(w):
        if tabs["periodic"]:
            return pl.BlockSpec((tm, w), lambda i: (i % nt, 0))
        return pl.BlockSpec((tm, w), lambda i: (i, 0))

    def col(off, w):
        return pl.BlockSpec((tm, w), lambda i: (i, off // w))

    out256 = jax.ShapeDtypeStruct((rows, 256), F32)
    return pl.pallas_call(
        _dsa_prep_kernel,
        grid=(rows // tm,),
        in_specs=[col(C_DQ, 256), col(C_DK, 256), col(C_DQI, 256), col(C_DKW, 128),
                  tab_spec(256), tab_spec(256), tab_spec(256), tab_spec(256), tab_spec(128), tab_spec(128),
                  _const_spec(consts["left_q"]), _const_spec(consts["left_i"]), _const_spec(consts["left_kw"]),
                  _const_spec(qg), _const_spec(kg), _const_spec(consts["wavg"])],
        out_specs=[pl.BlockSpec((tm, 256), lambda i: (i, 0))] * 3 + [pl.BlockSpec((tm, 128), lambda i: (i, 0))],
        out_shape=[out256, out256, out256, jax.ShapeDtypeStruct((rows, 128), F32)],
        compiler_params=_cparams(1), name="dsa_prep",
    )(P, P, P, P, tabs["cq"], tabs["sq"], tabs["ci"], tabs["si"], tabs["ckw"], tabs["skw"],
      consts["left_q"], consts["left_i"], consts["left_kw"], qg, kg, consts["wavg"])


def _score_keys(scores, valid):
    s = jnp.where(scores == 0.0, 0.0, scores)
    bits = pltpu.bitcast(s, I32)
    key = bits ^ (jnp.right_shift(bits, 31) & 0x7FFFFFFF)
    return jnp.where(valid, key, INT_MIN)


def _topk_select(key_ref, k, col):
    rows, cols = key_ref.shape
    kf = float(k)
    nbits = max(1, int(math.ceil(math.log2(cols))))

    def count_ge(c):
        return jnp.sum(jnp.where(key_ref[...] >= c, 1.0, 0.0), axis=-1, keepdims=True)

    base = jnp.where(count_ge(jnp.zeros((rows, 1), I32)) >= kf, 0, INT_MIN).astype(I32)

    def bit_step(i, base):
        cand = base | lax.shift_left(jnp.int32(1), 30 - i)
        return jnp.where(count_ge(cand) >= kf, cand, base)

    thr = lax.fori_loop(0, 31, bit_step, base)
    key = key_ref[...]
    need = kf - jnp.sum(jnp.where(key > thr, 1.0, 0.0), axis=-1, keepdims=True)

    def pos_step(i, pos):
        cand = pos + lax.shift_left(jnp.int32(1), nbits - 1 - i)
        hit = jnp.where(key_ref[...] == thr, jnp.where(col < cand, 1.0, 0.0), 0.0)
        return jnp.where(jnp.sum(hit, axis=-1, keepdims=True) < need, cand, pos)

    pos = lax.fori_loop(0, nbits, pos_step, jnp.zeros((rows, 1), I32))
    return jnp.where(key > thr, 1.0, jnp.where(key == thr, jnp.where(col <= pos, 1.0, 0.0), 0.0))


def _dsa_attn_kernel(qn_ref, qi_ref, kwq_ref, kn_ref, v_ref, kwk_ref, o_ref, key_ref, *, tq, topk):
    t0 = pl.program_id(1) * tq
    seq = kn_ref.shape[0]
    ki = kwk_ref[:, 0:IDX_D].astype(BF16)
    qi = qi_ref[...]
    kwq = kwq_ref[...]
    scores = jnp.zeros((tq, seq), F32)
    for h in range(IDX_H):
        s = lax.dot_general(qi[:, h * IDX_D:(h + 1) * IDX_D].astype(BF16), ki,
                            (((1,), (1,)), ((), ())), preferred_element_type=F32)
        w = kwq[:, IDX_D + h:IDX_D + h + 1] * (IDX_D ** -0.5)
        scores = scores + w * jnp.maximum(s, 0.0)
    col = lax.broadcasted_iota(I32, (tq, seq), 1)
    row = t0 + lax.broadcasted_iota(I32, (tq, seq), 0)
    causal = col <= row
    key_ref[...] = _score_keys(scores, causal)
    sel = jnp.where(causal, _topk_select(key_ref, topk, col), 0.0) > 0.0
    qn = qn_ref[...]
    for h in range(ATT_H):
        hs = slice(h * HD, (h + 1) * HD)
        lg = lax.dot_general(qn[:, hs].astype(BF16), kn_ref[:, hs].astype(BF16),
                             (((1,), (1,)), ((), ())), preferred_element_type=F32) * (HD ** -0.5)
        lg = jnp.where(sel, lg, -jnp.inf)
        m = jnp.max(lg, axis=-1, keepdims=True)
        p = jnp.exp(lg - m)
        l = jnp.sum(p, axis=-1, keepdims=True)
        o = jnp.dot(p.astype(BF16), v_ref[:, hs].astype(BF16), preferred_element_type=F32)
        o_ref[:, hs] = o / l


def _dsa_attn(qn, kn, qir, kwr, P, batch, seq, tq):
    topk = min(TOPK_MAX, seq // 4)
    nq = seq // tq
    vblk = C_DV // 256
    return pl.pallas_call(
        functools.partial(_dsa_attn_kernel, tq=tq, topk=topk),
        grid=(batch, nq),
        in_specs=[pl.BlockSpec((tq, 256), lambda b, i: (b * nq + i, 0)),
                  pl.BlockSpec((tq, 256), lambda b, i: (b * nq + i, 0)),
                  pl.BlockSpec((tq, 128), lambda b, i: (b * nq + i, 0)),
                  pl.BlockSpec((seq, 256), lambda b, i: (b, 0)),
                  pl.BlockSpec((seq, 256), lambda b, i: (b, vblk)),
                  pl.BlockSpec((seq, 128), lambda b, i: (b, 0))],
        out_specs=pl.BlockSpec((tq, 256), lambda b, i: (b * nq + i, 0)),
        out_shape=jax.ShapeDtypeStruct((batch * seq, 256), F32),
        scratch_shapes=[pltpu.VMEM((tq, seq), I32)],
        compiler_params=_cparams(2), name="dsa_attn",
    )(qn, qir, kwr, kn, P, kwr)


def _gla_kernel(v_ref, r_ref, qk_ref, a_ref, st0_ref, a2h_ref, a2l_ref, ab_ref, ng_ref,
                ltri_ref, e2_ref, bd_ref, wavg_ref, o_ref, st_ref, b_s, *, chunk, nchunk, lvalid):
    @pl.when(pl.program_id(1) == 0)
    def _():
        st_ref[...] = st0_ref[...]

    tri = ltri_ref[...]
    rowid = lax.broadcasted_iota(I32, (chunk, 128), 0)
    for c in range(nchunk):
        rows = slice(c * chunk, (c + 1) * chunk)
        z = _dot3(a_ref[rows, :], a2h_ref[...], a2l_ref[...]) + ab_ref[...]
        la = (jnp.minimum(z, 0.0) - jnp.log1p(jnp.exp(-jnp.abs(z)))) * (1.0 / GLA_TAU)
        q = qk_ref[rows, 0:128] * (GLA_DK ** -0.5)
        k = qk_ref[rows, 128:256]
        v = v_ref[rows, :]
        if lvalid < chunk:
            keep = rowid < lvalid
            la = jnp.where(keep, la, 0.0)
            k = jnp.where(keep, k, 0.0)
            v = jnp.where(lax.broadcasted_iota(I32, (chunk, 256), 0) < lvalid, v, 0.0)
        la_hi = la.astype(BF16)
        la_r = la - la_hi.astype(F32)
        la_mid = la_r.astype(BF16)
        la_lo = (la_r - la_mid.astype(F32)).astype(BF16)
        b = (jnp.dot(tri, la_hi, preferred_element_type=F32)
             + jnp.dot(tri, la_mid, preferred_element_type=F32)
             + jnp.dot(tri, la_lo, preferred_element_type=F32))
        b_s[...] = b
        st = st_ref[...]
        inter = lax.dot_general((q * jnp.exp(b)).astype(BF16), st.astype(BF16),
                                (((1,), (1,)), ((), ())), preferred_element_type=F32)

        def key_step(s, o, c=c, b=b, q=q):
            r0 = c * chunk + s
            bs = b_s[pl.ds(s, 1), :]
            ks = qk_ref[pl.ds(r0, 1), :][:, 128:256]
            vs = v_ref[pl.ds(r0, 1), :]
            if lvalid < chunk:
                ks = jnp.where(s < lvalid, ks, 0.0)
            e = jnp.exp(jnp.where(rowid >= s, b - bs, -jnp.inf))
            att = jnp.dot((q * (ks * e)).astype(BF16), e2_ref[...], preferred_element_type=F32)
            return o + att * vs

        intra = lax.fori_loop(0, chunk, key_step, jnp.zeros((chunk, 256), F32))
        o = inter + intra
        ms = _split_dot(o * o, wavg_ref[...], 2)
        o = o * lax.rsqrt(ms + EPS) * ng_ref[...]
        r = r_ref[rows, :]
        o_ref[rows, :] = o * (r * _sigmoid(r))
        bl = b[chunk - 1:chunk, :]
        kd = k * jnp.exp(bl - b)
        upd = lax.dot_general(v.astype(BF16), kd.astype(BF16), (((0,), (0,)), ((), ())),
                              preferred_element_type=F32)
        st_ref[...] = st * jnp.exp(bl) + upd * bd_ref[...]


def _gla(P, st0, w, consts, batch, seq, tg, chunk, lvalid):
    nt = seq // tg

    def col(off, wd):
        return pl.BlockSpec((tg, wd), lambda b, j: (b * nt + j, off // wd))

    cs = [w["gla_a2h"], w["gla_a2l"], w["gla_ab"], w["gla_ng"],
          consts["ltri"][chunk], consts["e2"], consts["bd"], consts["wavg"]]
    return pl.pallas_call(
        functools.partial(_gla_kernel, chunk=chunk, nchunk=tg // chunk, lvalid=lvalid),
        grid=(batch, nt),
        in_specs=[col(C_GV, 256), col(C_GR, 256), col(C_GQK, 256), col(C_GA, 128),
                  pl.BlockSpec((None, 256, 128), lambda b, j: (b, 0, 0))] + [_const_spec(a) for a in cs],
        out_specs=[pl.BlockSpec((tg, 256), lambda b, j: (b * nt + j, 0)),
                   pl.BlockSpec((None, 256, 128), lambda b, j: (b, 0, 0))],
        out_shape=[jax.ShapeDtypeStruct((batch * seq, 256), F32),
                   jax.ShapeDtypeStruct((batch, 256, 128), F32)],
        scratch_shapes=[pltpu.VMEM((chunk, 128), F32)],
        compiler_params=_cparams(2), name="gla",
    )(P, P, P, P, st0, *cs)


def _gelu_tanh(x):
    return 0.5 * x * (1.0 + jnp.tanh(math.sqrt(2.0 / math.pi) * (x + 0.044715 * (x * x * x))))


def _s5_kernel(u_ref, x0_ref, a_ref, bh_ref, bl_ref, c_ref, d_ref, gw_ref, gb_ref,
               o_ref, xf_ref, st_s, bur_s, bui_s, xr_s, xi_s, *, tile, last_row):
    @pl.when(pl.program_id(1) == 0)
    def _():
        st_s[...] = x0_ref[...]

    u = u_ref[...]
    bu = _dot3(u, bh_ref[...], bl_ref[...])
    bur_s[...] = bu[:, 0:S5_N]
    bui_s[...] = bu[:, S5_N:2 * S5_N]
    ar = a_ref[:, 0:S5_N]
    ai = a_ref[:, S5_N:2 * S5_N]

    def step(t, carry):
        xr, xi = carry
        row = pl.ds(t, 1)
        nr = ar * xr - ai * xi + bur_s[row, :]
        ni = ar * xi + ai * xr + bui_s[row, :]
        xr_s[row, :] = nr
        xi_s[row, :] = ni
        return nr, ni

    xr, xi = lax.fori_loop(0, tile, step, (st_s[:, 0:S5_N], st_s[:, S5_N:2 * S5_N]))
    st_s[:, 0:S5_N] = xr
    st_s[:, S5_N:2 * S5_N] = xi
    y = (jnp.dot(xr_s[...].astype(BF16), c_ref[0:S5_N, :], preferred_element_type=F32)
         + jnp.dot(xi_s[...].astype(BF16), c_ref[S5_N:2 * S5_N, :], preferred_element_type=F32)
         + d_ref[...] * u)
    z = _gelu_tanh(y)
    gate = jnp.dot(z.astype(BF16), gw_ref[...], preferred_element_type=F32) + gb_ref[...]
    o_ref[...] = z * _sigmoid(gate)
    xf_ref[:, 0:S5_N] = xr_s[last_row:last_row + 1, :]
    xf_ref[:, S5_N:2 * S5_N] = xi_s[last_row:last_row + 1, :]


def _s5(P, x0, w, batch, seq, tile, lvalid):
    nt = seq // tile
    last_row = (lvalid - 1) % tile
    cs = [w["s5_a"], w["s5_bh"], w["s5_bl"], w["s5_c"], w["s5_d"], w["s5_gw"], w["s5_gb"]]
    return pl.pallas_call(
        functools.partial(_s5_kernel, tile=tile, last_row=last_row),
        grid=(batch, nt),
        in_specs=[pl.BlockSpec((tile, 256), lambda b, j: (b * nt + j, C_S5 // 256)),
                  pl.BlockSpec((None, 1, 2 * S5_N), lambda b, j: (b, 0, 0))] + [_const_spec(a) for a in cs],
        out_specs=[pl.BlockSpec((tile, 256), lambda b, j: (b * nt + j, 0)),
                   pl.BlockSpec((None, 1, 2 * S5_N), lambda b, j: (b, 0, 0))],
        out_shape=[jax.ShapeDtypeStruct((batch * seq, 256), F32),
                   jax.ShapeDtypeStruct((batch, 1, 2 * S5_N), F32)],
        scratch_shapes=[pltpu.VMEM((1, 2 * S5_N), F32)] + [pltpu.VMEM((tile, S5_N), F32)] * 4,
        compiler_params=_cparams(2), name="s5",
    )(P, x0, *cs)


def _rwkv_kernel(r_ref, k_ref, v_ref, lo_ref, s0_ref, prev_ref, mu_ref, w0_ref, a0_ref,
                 w2h_ref, w2l_ref, a2h_ref, a2l_ref, g2h_ref, g2l_ref, kkp_ref, ka_ref, rk_ref, ng_ref,
                 wones_ref, wavg_ref, idt_ref,
                 o_ref, sf_ref, r_s, k_s, v_s, kk_s, ka_s, w_s, y_s, g_s, bo_s, prev_s, *, nb, tile, nsteps):
    @pl.when(pl.program_id(1) == 0)
    def _():
        sf_ref[...] = s0_ref[...]
        prev_s[...] = prev_ref[...]

    wones = wones_ref[...]
    wavg = wavg_ref[...]
    idt = idt_ref[...]
    row0_256 = lax.broadcasted_iota(I32, (tile, 256), 0) == 0
    row0_128 = lax.broadcasted_iota(I32, (tile, 128), 0) == 0

    def shift_mix(p, prev_row, mu, row0):
        sh = jnp.where(row0, prev_row, pltpu.roll(p, 1, 0))
        return p + (sh - p) * mu

    for b in range(nb):
        pr, pk, pv, plo = r_ref[b], k_ref[b], v_ref[b], lo_ref[b]
        r = shift_mix(pr, prev_s[b, :, 0:256], mu_ref[:, 0:256], row0_256)
        k = shift_mix(pk, prev_s[b, :, 256:512], mu_ref[:, 256:512], row0_256)
        v = shift_mix(pv, prev_s[b, :, 512:768], mu_ref[:, 512:768], row0_256)
        lo = shift_mix(plo, prev_s[b, :, 768:896], mu_ref[:, 768:896], row0_128)
        prev_s[b, :, 0:256] = pr[tile - 1:tile, :]
        prev_s[b, :, 256:512] = pk[tile - 1:tile, :]
        prev_s[b, :, 512:768] = pv[tile - 1:tile, :]
        prev_s[b, :, 768:896] = plo[tile - 1:tile, :]
        wl = w0_ref[...] + _dot3(jnp.tanh(lo), w2h_ref[...], w2l_ref[...])
        w = -_softplus(-wl) - 0.5
        a = _sigmoid(a0_ref[...] + _dot3(lo, a2h_ref[...], a2l_ref[...]))
        g = _dot3(_sigmoid(lo), g2h_ref[...], g2l_ref[...])
        kk = k * kkp_ref[...]
        kk = kk * lax.rsqrt(_split_dot(kk * kk, wones, 3) + EPS)
        k2 = k * (1.0 + (a - 1.0) * ka_ref[...])
        bonus = _split_dot(r * k2 * rk_ref[...], wones, 3) * v
        r_s[b] = r
        k_s[b] = k2
        v_s[b] = v
        kk_s[b] = -kk
        ka_s[b] = kk * a
        w_s[b] = jnp.exp(-jnp.exp(w))
        if nsteps < tile:
            y_s[b] = jnp.zeros((tile, 256), F32)
        g_s[b] = g
        bo_s[b] = bonus

    def step(t, carry):
        for b in range(nb):
            row = pl.ds(t, 1)
            S = sf_ref[b]
            sa = _split_dot(S * kk_s[b, row, :], wones, 2)
            vcol = _split_dot(idt * v_s[b, row, :], wones, 2)
            Sn = S * w_s[b, row, :] + sa * ka_s[b, row, :] + vcol * k_s[b, row, :]
            yb = jnp.dot((Sn * r_s[b, row, :]).astype(BF16), wones, preferred_element_type=F32)
            y_s[b, row, :] = jnp.sum(yb * idt, axis=0, keepdims=True)
            sf_ref[b] = Sn
        return carry

    lax.fori_loop(0, nsteps, step, 0)
    for b in range(nb):
        y = y_s[b]
        mu = _split_dot(y, wavg, 3)
        yc = y - mu
        var = _split_dot(yc * yc, wavg, 3)
        o_ref[b] = (yc * lax.rsqrt(var + RW_GN_EPS) * ng_ref[...] + bo_s[b]) * g_s[b]


def _rwkv(P3, s0, prev, w, consts, nb, tile, lvalid):
    batch, seq, _ = P3.shape
    nt = seq // tile
    nsteps = tile if lvalid >= seq else lvalid
    cs = [w["rw_mu"], w["rw_w0"], w["rw_a0"], w["rw_w2h"], w["rw_w2l"], w["rw_a2h"], w["rw_a2l"],
          w["rw_g2h"], w["rw_g2l"], w["rw_kk"], w["rw_ka"], w["rw_rk"], w["rw_ng"],
          consts["wones"], consts["wavg"], consts["idt"]]

    def col(off, wd):
        return pl.BlockSpec((nb, tile, wd), lambda g, j: (g, j, off // wd))

    big = lambda: pltpu.VMEM((nb, tile, 256), F32)
    return pl.pallas_call(
        functools.partial(_rwkv_kernel, nb=nb, tile=tile, nsteps=nsteps),
        grid=(batch // nb, nt),
        in_specs=[col(C_RW, 256), col(C_RW + 256, 256), col(C_RW + 512, 256), col(C_RWLO, 128),
                  pl.BlockSpec((nb, RW_N, 256), lambda g, j: (g, 0, 0)),
                  pl.BlockSpec((nb, 1, RW_COLS), lambda g, j: (g, 0, 0))] + [_const_spec(a) for a in cs],
        out_specs=[pl.BlockSpec((nb, tile, 256), lambda g, j: (g, j, 0)),
                   pl.BlockSpec((nb, RW_N, 256), lambda g, j: (g, 0, 0))],
        out_shape=[jax.ShapeDtypeStruct((batch, seq, 256), F32),
                   jax.ShapeDtypeStruct((batch, RW_N, 256), F32)],
        scratch_shapes=[big(), big(), big(), big(), big(), big(), big(), big(), big(),
                        pltpu.VMEM((nb, 1, RW_COLS), F32)],
        compiler_params=_cparams(2), name="rwkv",
    )(P3, P3, P3, P3, s0, prev, *cs)


def _merge_kernel(x_ref, sc_ref, sh_ref, gt_ref, g_ref, og_ref, oa_ref, os_ref, or_ref,
                  wg_ref, wbr_ref, wo_ref, o_ref):
    x = x_ref[...]
    h = _norm_mod(x, g_ref[...], sc_ref[...], sh_ref[...]).astype(BF16)
    merged = None
    for b, oref in enumerate((og_ref, oa_ref, os_ref, or_ref)):
        gate = _sigmoid(jnp.dot(h, wg_ref[:, b * D_MODEL:(b + 1) * D_MODEL], preferred_element_type=F32))
        proj = jnp.dot(oref[...].astype(BF16), wbr_ref[b], preferred_element_type=F32)
        merged = gate * proj if merged is None else merged + gate * proj
    y = jnp.dot(merged.astype(BF16), wo_ref[...], preferred_element_type=F32)
    o_ref[...] = x + gt_ref[...] * y


def _merge(x, mod, g, outs, w, tm, seq):
    rows = x.shape[0]
    row256 = pl.BlockSpec((tm, 256), lambda i: (i, 0))
    return pl.pallas_call(
        _merge_kernel,
        grid=(rows // tm,),
        in_specs=[pl.BlockSpec((tm, D_MODEL), lambda i: (i, 0)),
                  _mod_spec(mod, 1, tm, seq), _mod_spec(mod, 0, tm, seq), _mod_spec(mod, 2, tm, seq),
                  _const_spec(g), row256, row256, row256, row256,
                  _const_spec(w["w_gates"]), _const_spec(w["w_br"]), _const_spec(w["w_o"])],
        out_specs=pl.BlockSpec((tm, D_MODEL), lambda i: (i, 0)),
        out_shape=jax.ShapeDtypeStruct((rows, D_MODEL), F32),
        compiler_params=_cparams(1), name="merge",
    )(x, mod, mod, mod, g, *outs, w["w_gates"], w["w_br"], w["w_o"])


def _ffn_kernel(x_ref, sc_ref, sh_ref, gt_ref, g_ref, w1_ref, w2_ref, o_ref):
    x = x_ref[...]
    h = _norm_mod(x, g_ref[...], sc_ref[...], sh_ref[...]).astype(BF16)
    acc = None
    for c in range(D_FF // D_MODEL):
        cs = slice(c * D_MODEL, (c + 1) * D_MODEL)
        u = jnp.maximum(jnp.dot(h, w1_ref[:, cs], preferred_element_type=F32), 0.0)
        d = jnp.dot((u * u).astype(BF16), w2_ref[cs, :], preferred_element_type=F32)
        acc = d if acc is None else acc + d
    o_ref[...] = x + gt_ref[...] * acc


def _ffn(x, mod, g, w, tm, seq):
    rows = x.shape[0]
    return pl.pallas_call(
        _ffn_kernel,
        grid=(rows // tm,),
        in_specs=[pl.BlockSpec((tm, D_MODEL), lambda i: (i, 0)),
                  _mod_spec(mod, 4, tm, seq), _mod_spec(mod, 3, tm, seq), _mod_spec(mod, 5, tm, seq),
                  _const_spec(g), _const_spec(w["w_ff1"]), _const_spec(w["w_ff2"])],
        out_specs=pl.BlockSpec((tm, D_MODEL), lambda i: (i, 0)),
        out_shape=jax.ShapeDtypeStruct((rows, D_MODEL), F32),
        compiler_params=_cparams(1), name="ffn",
    )(x, mod, mod, mod, g, w["w_ff1"], w["w_ff2"])


def _ds_scores_kernel(pt_ref, q8_ref, w8_ref, kcur_ref, *refs, npages):
    pages, o_ref = refs[:npages], refs[npages]
    q8 = q8_ref[...]
    q8b = q8.astype(BF16)
    w8 = w8_ref[...] * (IDX_D ** -0.5)
    for p in range(npages):
        s = lax.dot_general(q8b, pages[p][...].astype(BF16), (((1,), (1,)), ((), ())),
                            preferred_element_type=F32)
        o_ref[p:p + 1, :] = jnp.sum(w8 * jnp.maximum(s, 0.0), axis=0, keepdims=True)
    s_cur = jnp.sum(q8 * kcur_ref[...], axis=-1, keepdims=True)
    i_cur = jnp.sum(w8 * jnp.maximum(s_cur, 0.0), axis=0, keepdims=True)
    lane = lax.broadcasted_iota(I32, (8, 128), 1)
    rowi = lax.broadcasted_iota(I32, (8, 128), 0)
    o_ref[npages:npages + 8, :] = jnp.where((lane == 0) & (rowi == 0), i_cur, -jnp.inf)


def _ds_scores(page_table, q8, w8, kcur, cache_idx, layer):
    n, npages = page_table.shape
    page_specs = [pl.BlockSpec((None, None, PAGE_SIZE, IDX_D), lambda i, pt, p=p: (layer, pt[i, p], 0, 0))
                  for p in range(npages)]
    return pl.pallas_call(
        functools.partial(_ds_scores_kernel, npages=npages),
        grid_spec=pltpu.PrefetchScalarGridSpec(
            num_scalar_prefetch=1, grid=(n,),
            in_specs=[pl.BlockSpec((None, IDX_H, IDX_D), lambda i, pt: (i, 0, 0)),
                      pl.BlockSpec((None, IDX_H, 1), lambda i, pt: (i, 0, 0)),
                      pl.BlockSpec((None, 1, IDX_D), lambda i, pt: (i, 0, 0))] + page_specs,
            out_specs=pl.BlockSpec((None, npages + 8, 128), lambda i, pt: (i, 0, 0))),
        out_shape=jax.ShapeDtypeStruct((n, npages + 8, 128), F32),
        compiler_params=_cparams(1), name="ds_scores",
    )(page_table, q8, w8, kcur, *([cache_idx] * npages))


def _ds_select_kernel(s_ref, o_ref, key_ref, *, topk, nvalid):
    rows, cols = s_ref.shape
    col = lax.broadcasted_iota(I32, (rows, cols), 1)
    valid = col < nvalid
    key_ref[...] = _score_keys(s_ref[...], valid)
    o_ref[...] = jnp.where(valid, _topk_select(key_ref, topk, col), 0.0)


def _ds_select(scores, topk, nvalid):
    rows, cols = scores.shape
    return pl.pallas_call(
        functools.partial(_ds_select_kernel, topk=topk, nvalid=nvalid),
        grid=(1,),
        in_specs=[pl.BlockSpec((rows, cols), lambda i: (0, 0))],
        out_specs=pl.BlockSpec((rows, cols), lambda i: (0, 0)),
        out_shape=jax.ShapeDtypeStruct((rows, cols), F32),
        scratch_shapes=[pltpu.VMEM((rows, cols), I32)],
        compiler_params=_cparams(1), name="ds_select",
    )(scores)


def _ds_attn_kernel(pt_ref, q_ref, kcur_ref, vcur_ref, m_ref, hm_ref, *refs, npages):
    kp, vp, o_ref = refs[:npages], refs[npages:2 * npages], refs[2 * npages]
    hm = hm_ref[...]
    qf = q_ref[...] * hm
    qb = qf.astype(BF16)
    sc = HD ** -0.5
    lg_cur = jnp.sum(qf * kcur_ref[...], axis=-1, keepdims=True) * sc
    cur_sel = m_ref[npages:npages + 1, 0:1] > 0.0
    mx = lg_cur
    lgs = []
    for p in range(npages):
        lg = lax.dot_general(qb, kp[p][...].astype(BF16), (((1,), (1,)), ((), ())),
                             preferred_element_type=F32) * sc
        lg = jnp.where(m_ref[p:p + 1, :] > 0.0, lg, -jnp.inf)
        lgs.append(lg)
        mx = jnp.maximum(mx, jnp.max(lg, axis=-1, keepdims=True))
    pc = jnp.where(cur_sel, jnp.exp(lg_cur - mx), 0.0)
    l = pc
    acc = pc * vcur_ref[...]
    for p in range(npages):
        pe = jnp.exp(lgs[p] - mx)
        l = l + jnp.sum(pe, axis=-1, keepdims=True)
        acc = acc + jnp.dot(pe.astype(BF16), vp[p][...].astype(BF16), preferred_element_type=F32)
    o_ref[...] = jnp.sum((acc / l) * hm, axis=0, keepdims=True)


def _ds_attn(page_table, q, kcur, vcur, mask, cache_k, cache_v, layer, headmask):
    n, npages = page_table.shape
    pspec = lambda p: pl.BlockSpec((None, None, PAGE_SIZE, 256), lambda i, pt, p=p: (layer, pt[i, p], 0, 0))
    row = pl.BlockSpec((None, 1, 256), lambda i, pt: (i, 0, 0))
    return pl.pallas_call(
        functools.partial(_ds_attn_kernel, npages=npages),
        grid_spec=pltpu.PrefetchScalarGridSpec(
            num_scalar_prefetch=1, grid=(n,),
            in_specs=[row, row, row,
                      pl.BlockSpec((None, npages + 8, 128), lambda i, pt: (i, 0, 0)),
                      pl.BlockSpec((8, 256), lambda i, pt: (0, 0))]
                     + [pspec(p) for p in range(npages)] + [pspec(p) for p in range(npages)],
            out_specs=row),
        out_shape=jax.ShapeDtypeStruct((n, 1, 256), F32),
        compiler_params=_cparams(1), name="ds_attn",
    )(page_table, q, kcur, vcur, mask, headmask, *([cache_k] * npages), *([cache_v] * npages))


def _constants():
    lane256 = np.arange(256)
    head = lane256 // 64
    wones = (head[:, None] == head[None, :]).astype(np.float32)
    idt = (np.arange(64)[:, None] == (lane256 % 64)[None, :]).astype(np.float32)
    e2 = ((np.arange(128) // 32)[:, None] == head[None, :]).astype(np.float32)
    bd = (head[:, None] == (np.arange(128) // 32)[None, :]).astype(np.float32)
    hm8 = (np.arange(8)[:, None] == head[None, :]).astype(np.float32)
    ltri = {c: jnp.asarray(np.tril(np.ones((c, c), np.float32)), BF16) for c in (8, 64)}

    def left(width, group, half):
        return jnp.asarray(((np.arange(width) % group) < half).astype(np.float32)[None, :])

    return dict(wones=jnp.asarray(wones, BF16), wavg=jnp.asarray(wones / 64.0, BF16), idt=jnp.asarray(idt),
                e2=jnp.asarray(e2, BF16), bd=jnp.asarray(bd), hm8=jnp.asarray(hm8), ltri=ltri,
                left_q=left(256, HD, ROT // 2), left_i=left(256, IDX_D, IDX_ROT // 2),
                left_kw=left(128, 128, IDX_ROT // 2))


def _rope_tables(pos, periodic):
    pos = pos.astype(F32)[:, None]

    def build(width, group, rot, extra=None):
        half = rot // 2
        freq = ROPE_THETA ** (-jnp.arange(half, dtype=F32) * (2.0 / rot))
        ang = pos * freq
        cos, sin = jnp.cos(ang), jnp.sin(ang)
        n = pos.shape[0]
        ones = jnp.ones((n, group - rot), F32)
        zeros = jnp.zeros((n, group - rot), F32)
        cg = jnp.concatenate([cos, cos, ones], axis=1)
        sg = jnp.concatenate([-sin, sin, zeros], axis=1)
        reps = width // group
        c, s = jnp.tile(cg, (1, reps)), jnp.tile(sg, (1, reps))
        if extra is not None:
            c, s = extra(c, s)
        return c, s

    cq, sq = build(256, HD, ROT)
    ci, si = build(256, IDX_D, IDX_ROT)

    def kw_extra(c, s):
        lane = jnp.arange(128)
        scale = jnp.where((lane >= IDX_D) & (lane < IDX_D + IDX_H), IDX_H ** -0.5, 1.0)
        keep = (lane < IDX_D)
        return jnp.where(keep, c, scale[None, :]), jnp.where(keep, s, 0.0)

    ckw, skw = build(128, IDX_D, IDX_ROT, kw_extra)
    return dict(cq=cq, sq=sq, ci=ci, si=si, ckw=ckw, skw=skw, periodic=periodic)


def _blockdiag(blocks):
    g, r, c = blocks.shape
    eye = jnp.eye(g, dtype=blocks.dtype)
    return jnp.einsum('grc,gh->grhc', blocks, eye).reshape(g * r, g * c)


def _layer_weights(l, p):
    w_in = p["w_in"][l]
    z = lambda n: jnp.zeros((D_MODEL, n), F32)
    w_mix = jnp.concatenate([
        w_in[:, 256:512], w_in[:, 528:784], w_in[:, 0:256],
        w_in[:, 784:1808], w_in[:, 1848:2104], w_in[:, 2104:3000],
        w_in[:, 512:528], z(112), w_in[:, 1808:1848], z(88)], axis=1).astype(BF16)
    w = dict(w_mix=w_mix, w_gates=w_in[:, 3000:7096].astype(BF16),
             w_br=p["w_br"][l].astype(BF16), w_o=p["w_o"][l].astype(BF16),
             w_ff1=p["w_ff1"][l].astype(BF16), w_ff2=p["w_ff2"][l].astype(BF16),
             norm1_g=p["norm1_g"][l][None, :], norm2_g=p["norm2_g"][l][None, :])
    a2 = jnp.zeros((128, 128), F32).at[0:GLA_RANK].set(p["gla_a2"][l])
    w["gla_a2h"], w["gla_a2l"] = _hilo(a2)
    w["gla_ab"] = p["gla_ab"][l][None, :]
    w["gla_ng"] = jnp.tile(p["gla_ng"][l], GLA_H)[None, :]
    w["att_qg"] = jnp.tile(p["att_qg"][l], ATT_H)[None, :]
    w["att_kg"] = jnp.tile(p["att_kg"][l], ATT_H)[None, :]
    dt = jnp.exp(p["s5_log_dt"][l])[:, None]
    lr = jnp.minimum(p["s5_a_re"][l], -1e-4)
    li = p["s5_a_im"][l]
    mag = jnp.exp(lr * dt)
    abr, abi = mag * jnp.cos(li * dt), mag * jnp.sin(li * dt)
    den = lr * lr + li * li
    fr = ((abr - 1.0) * lr + abi * li) / den
    fi = (abi * lr - (abr - 1.0) * li) / den
    b_re, b_im = p["s5_b_re"][l], p["s5_b_im"][l]
    bbr = fr[..., None] * b_re - fi[..., None] * b_im
    bbi = fr[..., None] * b_im + fi[..., None] * b_re
    bmat = jnp.concatenate([_blockdiag(bbr.transpose(0, 2, 1)), _blockdiag(bbi.transpose(0, 2, 1))], axis=1)
    w["s5_bh"], w["s5_bl"] = _hilo(bmat)
    w["s5_a"] = jnp.concatenate([abr.reshape(1, -1), abi.reshape(1, -1)], axis=1)
    w["s5_c"] = jnp.concatenate([_blockdiag(p["s5_c_re"][l].transpose(0, 2, 1)),
                                 -_blockdiag(p["s5_c_im"][l].transpose(0, 2, 1))], axis=0).astype(BF16)
    w["s5_d"] = p["s5_d"][l].reshape(1, -1)
    w["s5_gw"] = p["s5_glu_w"][l].astype(BF16)
    w["s5_gb"] = p["s5_glu_b"][l][None, :]
    w["rw_mu"] = p["rw_mu"][l][None, :]
    w["rw_w0"] = p["rw_w0"][l][None, :]
    w["rw_a0"] = p["rw_a0"][l][None, :]
    lo = jnp.zeros((128, 256), F32)
    w["rw_w2h"], w["rw_w2l"] = _hilo(lo.at[0:RW_WR].set(p["rw_w2"][l]))
    w["rw_a2h"], w["rw_a2l"] = _hilo(lo.at[RW_WR:RW_WR + RW_AR].set(p["rw_a2"][l]))
    w["rw_g2h"], w["rw_g2l"] = _hilo(lo.at[RW_WR + RW_AR:128].set(p["rw_g2"][l]))
    for nm in ("rw_kk", "rw_ka", "rw_rk", "rw_ng"):
        w[nm] = p[nm][l][None, :]
    return w


def _mix_and_ffn(x, mod, w, consts, P, o_att, st_gla0, st_s50, st_rw0, prev, batch, seq, lvalid, tm, tiles):
    o_gla, st_gla = _gla(P, st_gla0, w, consts, batch, seq, tiles["gla"], tiles["chunk"], lvalid)
    o_s5, st_s5 = _s5(P, st_s50, w, batch, seq, tiles["s5"], lvalid)
    o_rw, st_rw = _rwkv(P.reshape(batch, seq, NP_COLS), st_rw0, prev, w, consts, tiles["nb"], tiles["rw"], lvalid)
    x = _merge(x, mod, w["norm1_g"], (o_gla, o_att, o_s5, o_rw.reshape(batch * seq, 256)), w, tm, seq)
    x = _ffn(x, mod, w["norm2_g"], w, tm, seq)
    return x, st_gla, st_s5, st_rw


def _gla_state_out(st):
    n = st.shape[0]
    s = st.reshape(n, GLA_H, GLA_DV, GLA_H, GLA_DK)
    s = jnp.stack([s[:, h, :, h, :] for h in range(GLA_H)], axis=1)
    return s.transpose(0, 1, 3, 2)


def _gla_state_in(s):
    eye = jnp.eye(GLA_H, dtype=s.dtype)
    n = s.shape[0]
    return jnp.einsum('nhkv,hg->nhvgk', s, eye).reshape(n, GLA_H * GLA_DV, GLA_H * GLA_DK)


def _rw_state_out(st):
    n = st.shape[0]
    return st.reshape(n, RW_N, RW_H, RW_N).transpose(0, 2, 1, 3)


def _rw_state_in(s):
    n = s.shape[0]
    return s.transpose(0, 2, 1, 3).reshape(n, RW_N, RW_H * RW_N)


def _forward(x_prompt, x_sample, c_prompt, c_sample, cache_k, cache_v, cache_idx, state_gla,
             state_s5_re, state_s5_im, state_rwkv, state_shift, page_table, p):
    B, S, _ = x_prompt.shape
    N = x_sample.shape[0]
    depth = p["w_in"].shape[0]
    past = page_table.shape[1] * PAGE_SIZE
    consts = _constants()
    mod_all = _modulation(jnp.concatenate([c_prompt, c_sample], axis=0), p["ada_w"], p["ada_b"])
    tabs_p = _rope_tables(jnp.arange(S), True)
    tabs_s = _rope_tables(jnp.full((N * SAMPLE_PAD,), past), False)
    ck = cache_k.reshape(*cache_k.shape[:3], ATT_H * HD)
    cv = cache_v.reshape(*cache_v.shape[:3], ATT_H * HD)

    tm_p = min(512, S)
    tm_s = min(128, N * SAMPLE_PAD)
    tiles_p = dict(gla=min(256, S), chunk=64, s5=min(256, S), rw=min(128, S), nb=min(4, B))
    tiles_s = dict(gla=SAMPLE_PAD, chunk=SAMPLE_PAD, s5=SAMPLE_PAD, rw=SAMPLE_PAD, nb=8)
    tq = min(128, S)

    xp = x_prompt.reshape(B * S, D_MODEL)
    xs = jnp.pad(x_sample, ((0, 0), (0, SAMPLE_PAD - 1), (0, 0))).reshape(N * SAMPLE_PAD, D_MODEL)
    outs_p, outs_s = [], []
    for l in range(depth):
        w = _layer_weights(l, p)
        mod = mod_all[l, :B].reshape(B, 1, 6 * D_MODEL)
        P = _inproj(xp, mod, w["norm1_g"], w["w_mix"], tm_p, S)
        qn, kn, qir, kwr = _dsa_prep(P, tabs_p, consts, w["att_qg"], w["att_kg"], tm_p, S)
        o_att = _dsa_attn(qn, kn, qir, kwr, P, B, S, tq)
        xp, st_gla, st_s5, st_rw = _mix_and_ffn(
            xp, mod, w, consts, P, o_att,
            jnp.zeros((B, 256, 128), F32), jnp.zeros((B, 1, 2 * S5_N), F32),
            jnp.zeros((B, RW_N, 256), F32), jnp.zeros((B, 1, RW_COLS), F32), B, S, S, tm_p, tiles_p)
        P3 = P.reshape(B, S, NP_COLS)
        outs_p.append((kn.reshape(B, S, ATT_H, HD), P3[:, :, C_DV:C_DV + 256].reshape(B, S, ATT_H, HD),
                       kwr.reshape(B, S, 128)[:, :, :IDX_D], _gla_state_out(st_gla),
                       st_s5[:, 0, :S5_N].reshape(B, S5_G, S5_P), st_s5[:, 0, S5_N:].reshape(B, S5_G, S5_P),
                       _rw_state_out(st_rw), P3[:, S - 1, C_RW:C_RW + RW_COLS]))
        mod = jnp.repeat(mod_all[l, B:], SAMPLE_PAD, axis=0)
        P = _inproj(xs, mod, w["norm1_g"], w["w_mix"], tm_s, SAMPLE_PAD)
        qn, kn, qir, kwr = _dsa_prep(P, tabs_s, consts, w["att_qg"], w["att_kg"], tm_s, SAMPLE_PAD)
        first = lambda a: a.reshape(N, SAMPLE_PAD, a.shape[-1])[:, 0]
        qn1, kn1, qir1, kwr1, P1 = first(qn), first(kn), first(qir), first(kwr), first(P)
        v1 = P1[:, C_DV:C_DV + 256]
        scores = _ds_scores(page_table, qir1.reshape(N, IDX_H, IDX_D),
                            kwr1[:, IDX_D:IDX_D + IDX_H].reshape(N, IDX_H, 1),
                            kwr1[:, :IDX_D].reshape(N, 1, IDX_D), cache_idx, l)
        ncols = scores.shape[1] * 128
        sel = _ds_select(scores.reshape(N, ncols), min(TOPK_MAX, (past + 1) // 4), past + 1)
        o1 = _ds_attn(page_table, qn1.reshape(N, 1, 256), kn1.reshape(N, 1, 256), v1.reshape(N, 1, 256),
                      sel.reshape(N, ncols // 128, 128), ck, cv, l, consts["hm8"])
        o_att = jnp.pad(o1, ((0, 0), (0, SAMPLE_PAD - 1), (0, 0))).reshape(N * SAMPLE_PAD, 256)
        x0 = jnp.concatenate([state_s5_re[l].reshape(N, 1, S5_N), state_s5_im[l].reshape(N, 1, S5_N)], axis=2)
        xs, st_gla, st_s5, st_rw = _mix_and_ffn(
            xs, mod, w, consts, P, o_att, _gla_state_in(state_gla[l]), x0, _rw_state_in(state_rwkv[l]),
            state_shift[l].reshape(N, 1, RW_COLS), N, SAMPLE_PAD, 1, tm_s, tiles_s)
        outs_s.append((kn1.reshape(N, 1, ATT_H, HD), v1.reshape(N, 1, ATT_H, HD), kwr1[:, None, :IDX_D],
                       _gla_state_out(st_gla), st_s5[:, 0, :S5_N].reshape(N, S5_G, S5_P),
                       st_s5[:, 0, S5_N:].reshape(N, S5_G, S5_P), _rw_state_out(st_rw),
                       P1[:, C_RW:C_RW + RW_COLS]))
    yp = xp.reshape(B, S, D_MODEL)
    ys = xs.reshape(N, SAMPLE_PAD, D_MODEL)[:, 0:1]
    stack = lambda lst, i: jnp.stack([s[i] for s in lst])
    return (yp, ys) + tuple(stack(outs_p, i) for i in range(8)) + tuple(stack(outs_s, i) for i in range(8))


def kernel(x_prompt, x_sample, c_prompt, c_sample, cache_k, cache_v, cache_idx, state_gla, state_s5_re, state_s5_im, state_rwkv, state_shift, page_table, ada_w, ada_b, norm1_g, norm2_g, w_in, gla_a2, gla_ab, gla_ng, att_qg, att_kg, s5_a_re, s5_a_im, s5_log_dt, s5_b_re, s5_b_im, s5_c_re, s5_c_im, s5_d, s5_glu_w, s5_glu_b, rw_mu, rw_w0, rw_w2, rw_a0, rw_a2, rw_g2, rw_kk, rw_ka, rw_rk, rw_ng, w_br, w_o, w_ff1, w_ff2):
    p = dict(ada_w=ada_w, ada_b=ada_b, norm1_g=norm1_g, norm2_g=norm2_g, w_in=w_in, gla_a2=gla_a2,
             gla_ab=gla_ab, gla_ng=gla_ng, att_qg=att_qg, att_kg=att_kg, s5_a_re=s5_a_re, s5_a_im=s5_a_im,
             s5_log_dt=s5_log_dt, s5_b_re=s5_b_re, s5_b_im=s5_b_im, s5_c_re=s5_c_re, s5_c_im=s5_c_im,
             s5_d=s5_d, s5_glu_w=s5_glu_w, s5_glu_b=s5_glu_b, rw_mu=rw_mu, rw_w0=rw_w0, rw_w2=rw_w2,
             rw_a0=rw_a0, rw_a2=rw_a2, rw_g2=rw_g2, rw_kk=rw_kk, rw_ka=rw_ka, rw_rk=rw_rk, rw_ng=rw_ng,
             w_br=w_br, w_o=w_o, w_ff1=w_ff1, w_ff2=w_ff2)
    return _forward(x_prompt, x_sample, c_prompt, c_sample, cache_k, cache_v, cache_idx, state_gla,
                    state_s5_re, state_s5_im, state_rwkv, state_shift, page_table, p)
```

```python
import functools
import math

import numpy as np
import jax
import jax.numpy as jnp
from jax import lax
from jax.experimental import pallas as pl
from jax.experimental.pallas import tpu as pltpu

F32 = jnp.float32
BF16 = jnp.bfloat16
I32 = jnp.int32

D_MODEL = 1024
BR_W = 256
GLA_H, GLA_DK, GLA_DV, GLA_RANK, GLA_TAU = 4, 32, 64, 16, 16.0
ATT_H, HD, ROT = 4, 64, 16
IDX_H, IDX_D, IDX_ROT = 8, 32, 8
TOPK_MAX = 256
ROPE_THETA = 500000.0
S5_G, S5_P, S5_CH = 16, 64, 16
S5_N = S5_G * S5_P
RW_H, RW_N, RW_WR, RW_AR, RW_GR = 4, 64, 32, 32, 64
RW_COLS = 896
D_FF = 4096
EPS = 1e-6
RW_GN_EPS = 64e-5
PAGE_SIZE = 128
INT_MIN = -(2 ** 31)
SAMPLE_PAD = 8

C_GV, C_GR, C_GQK = 0, 256, 512
C_DQ, C_DK, C_DV, C_DQI = 768, 1024, 1280, 1536
C_S5 = 1792
C_RW = 2048
C_RWLO = 2816
C_GA = 2944
C_DKW = 3072
NP_COLS = 3200
VMEM_LIMIT = 56 * 1024 * 1024


def _cparams(n_axes):
    return pltpu.CompilerParams(dimension_semantics=("arbitrary",) * n_axes,
                                vmem_limit_bytes=VMEM_LIMIT)


def _split_dot(x, w, terms):
    acc = None
    r = x
    for i in range(terms):
        hi = r.astype(BF16)
        d = jnp.dot(hi, w, preferred_element_type=F32)
        acc = d if acc is None else acc + d
        if i + 1 < terms:
            r = r - hi.astype(F32)
    return acc


def _dot3(x, w_hi, w_lo):
    x_hi = x.astype(BF16)
    x_lo = (x - x_hi.astype(F32)).astype(BF16)
    return (jnp.dot(x_hi, w_hi, preferred_element_type=F32)
            + jnp.dot(x_hi, w_lo, preferred_element_type=F32)
            + jnp.dot(x_lo, w_hi, preferred_element_type=F32))


def _hilo(w):
    hi = w.astype(BF16)
    return hi, (w - hi.astype(F32)).astype(BF16)


def _sigmoid(x):
    return 1.0 / (1.0 + jnp.exp(-x))


def _softplus(x):
    return jnp.maximum(x, 0.0) + jnp.log1p(jnp.exp(-jnp.abs(x)))


def _norm_mod(x, g, sc, sh):
    ms = jnp.mean(x * x, axis=-1, keepdims=True)
    return (x * lax.rsqrt(ms + EPS) * g) * (1.0 + sc) + sh


def _mod_spec(mod, j, tm, seq):
    if mod.ndim == 3:
        return pl.BlockSpec((None, 1, D_MODEL), lambda i: ((i * tm) // seq, 0, j))
    return pl.BlockSpec((tm, D_MODEL), lambda i: (i, j))


def _const_spec(a):
    nd = a.ndim
    return pl.BlockSpec(a.shape, lambda *_: (0,) * nd)


def _mod_kernel(c_ref, w_ref, b_ref, o_ref):
    o_ref[...] = jnp.dot(c_ref[...], w_ref[...].astype(BF16), preferred_element_type=F32) + b_ref[...]


def _modulation(c_all, ada_w, ada_b):
    depth = ada_w.shape[0]
    rows = c_all.shape[0]
    tn = 1536
    return pl.pallas_call(
        _mod_kernel,
        grid=(depth, 6 * D_MODEL // tn),
        in_specs=[pl.BlockSpec((rows, D_MODEL), lambda l, j: (0, 0)),
                  pl.BlockSpec((None, D_MODEL, tn), lambda l, j: (l, 0, j)),
                  pl.BlockSpec((None, 1, tn), lambda l, j: (l, 0, j))],
        out_specs=pl.BlockSpec((None, rows, tn), lambda l, j: (l, 0, j)),
        out_shape=jax.ShapeDtypeStruct((depth, rows, 6 * D_MODEL), F32),
        compiler_params=_cparams(2), name="modulation",
    )(c_all.astype(BF16), ada_w, ada_b.reshape(depth, 1, 6 * D_MODEL))


def _inproj_kernel(x_ref, sc_ref, sh_ref, g_ref, w_ref, o_ref):
    h = _norm_mod(x_ref[...], g_ref[...], sc_ref[...], sh_ref[...])
    o_ref[...] = jnp.dot(h.astype(BF16), w_ref[...], preferred_element_type=F32)


def _inproj(x, mod, g, w_mix, tm, seq):
    rows = x.shape[0]
    return pl.pallas_call(
        _inproj_kernel,
        grid=(rows // tm,),
        in_specs=[pl.BlockSpec((tm, D_MODEL), lambda i: (i, 0)),
                  _mod_spec(mod, 1, tm, seq), _mod_spec(mod, 0, tm, seq),
                  _const_spec(g), _const_spec(w_mix)],
        out_specs=pl.BlockSpec((tm, NP_COLS), lambda i: (i, 0)),
        out_shape=jax.ShapeDtypeStruct((rows, NP_COLS), F32),
        compiler_params=_cparams(1), name="inproj",
    )(x, mod, mod, g, w_mix)


def _rope_apply(x, cos, sn, left, shift):
    n = x.shape[-1]
    rot = jnp.where(left > 0.0, pltpu.roll(x, n - shift, 1), pltpu.roll(x, shift, 1))
    return x * cos + rot * sn


def _dsa_prep_kernel(q_ref, k_ref, qi_ref, kw_ref, cq_ref, sq_ref, ci_ref, si_ref, ckw_ref, skw_ref,
                     lq_ref, li_ref, lkw_ref, qg_ref, kg_ref, wavg_ref,
                     qn_ref, kn_ref, qir_ref, kwr_ref):
    wavg = wavg_ref[...]

    def headnorm(x, g):
        ms = _split_dot(x * x, wavg, 3)
        return x * lax.rsqrt(ms + EPS) * g

    cq, sq, lq = cq_ref[...], sq_ref[...], lq_ref[...]
    qn_ref[...] = _rope_apply(headnorm(q_ref[...], qg_ref[...]), cq, sq, lq, ROT // 2)
    kn_ref[...] = _rope_apply(headnorm(k_ref[...], kg_ref[...]), cq, sq, lq, ROT // 2)
    qir_ref[...] = _rope_apply(qi_ref[...], ci_ref[...], si_ref[...], li_ref[...], IDX_ROT // 2)
    kwr_ref[...] = _rope_apply(kw_ref[...], ckw_ref[...], skw_ref[...], lkw_ref[...], IDX_ROT // 2)


def _dsa_prep(P, tabs, consts, qg, kg, tm, seq):
    rows = P.shape[0]
    nt = seq // tm if tabs["periodic"] else None

    def tab_spec(w):
        if tabs["periodic"]:
            return pl.BlockSpec((tm, w), lambda i: (i % nt, 0))
        return pl.BlockSpec((tm, w), lambda i: (i, 0))

    def col(off, w):
        return pl.BlockSpec((tm, w), lambda i: (i, off // w))

    out256 = jax.ShapeDtypeStruct((rows, 256), F32)
    return pl.pallas_call(
        _dsa_prep_kernel,
        grid=(rows // tm,),
        in_specs=[col(C_DQ, 256), col(C_DK, 256), col(C_DQI, 256), col(C_DKW, 128),
                  tab_spec(256), tab_spec(256), tab_spec(256), tab_spec(256), tab_spec(128), tab_spec(128),
                  _const_spec(consts["left_q"]), _const_spec(consts["left_i"]), _const_spec(consts["left_kw"]),
                  _const_spec(qg), _const_spec(kg), _const_spec(consts["wavg"])],
        out_specs=[pl.BlockSpec((tm, 256), lambda i: (i, 0))] * 3 + [pl.BlockSpec((tm, 128), lambda i: (i, 0))],
        out_shape=[out256, out256, out256, jax.ShapeDtypeStruct((rows, 128), F32)],
        compiler_params=_cparams(1), name="dsa_prep",
    )(P, P, P, P, tabs["cq"], tabs["sq"], tabs["ci"], tabs["si"], tabs["ckw"], tabs["skw"],
      consts["left_q"], consts["left_i"], consts["left_kw"], qg, kg, consts["wavg"])


def _score_keys(scores, valid):
    s = jnp.where(scores == 0.0, 0.0, scores)
    bits = pltpu.bitcast(s, I32)
    key = bits ^ (jnp.right_shift(bits, 31) & 0x7FFFFFFF)
    return jnp.where(valid, key, INT_MIN)


def _topk_select(key_ref, k, col):
    rows, cols = key_ref.shape
    kf = float(k)
    nbits = max(1, int(math.ceil(math.log2(cols))))

    def count_ge(c):
        return jnp.sum(jnp.where(key_ref[...] >= c, 1.0, 0.0), axis=-1, keepdims=True)

    base = jnp.where(count_ge(jnp.zeros((rows, 1), I32)) >= kf, 0, INT_MIN).astype(I32)

    def bit_step(i, base):
        cand = base | lax.shift_left(jnp.int32(1), 30 - i)
        return jnp.where(count_ge(cand) >= kf, cand, base)

    thr = lax.fori_loop(0, 31, bit_step, base)
    key = key_ref[...]
    need = kf - jnp.sum(jnp.where(key > thr, 1.0, 0.0), axis=-1, keepdims=True)
    excess = jnp.sum(jnp.where(key == thr, 1.0, 0.0), axis=-1, keepdims=True) > need

    def pos_step(i, pos):
        cand = pos + lax.shift_left(jnp.int32(1), nbits - 1 - i)
        hit = jnp.where(key_ref[...] == thr, jnp.where(col < cand, 1.0, 0.0), 0.0)
        return jnp.where(jnp.sum(hit, axis=-1, keepdims=True) < need, cand, pos)

    n_iter = jnp.where(jnp.max(jnp.where(excess, 1, 0)) > 0, nbits, 0)
    pos = lax.fori_loop(0, n_iter, pos_step, jnp.zeros((rows, 1), I32))
    pos = jnp.where(excess, pos, cols)
    return jnp.where(key > thr, 1.0, jnp.where(key == thr, jnp.where(col <= pos, 1.0, 0.0), 0.0))


def _dsa_attn_kernel(qn_ref, qi_ref, kwq_ref, kn_ref, v_ref, kwk_ref, o_ref, key_ref, *, tq, topk, nvar):
    it = pl.program_id(1)
    t0 = it * tq
    seq = kn_ref.shape[0]
    step = seq // nvar
    per = step // tq

    def body(klen):
        ki = kwk_ref[0:klen, 0:IDX_D].astype(BF16)
        qi = qi_ref[...]
        kwq = kwq_ref[...]
        scores = jnp.zeros((tq, klen), F32)
        for h in range(IDX_H):
            s = lax.dot_general(qi[:, h * IDX_D:(h + 1) * IDX_D].astype(BF16), ki, _LANES,
                                preferred_element_type=F32)
            w = kwq[:, IDX_D + h:IDX_D + h + 1] * (IDX_D ** -0.5)
            scores = scores + w * jnp.maximum(s, 0.0)
        col = lax.broadcasted_iota(I32, (tq, klen), 1)
        row = t0 + lax.broadcasted_iota(I32, (tq, klen), 0)
        causal = col <= row
        keys = key_ref.at[:, pl.ds(0, klen)]
        keys[...] = _score_keys(scores, causal)
        sel = jnp.where(causal, _topk_select(keys, topk, col), 0.0) > 0.0
        qn = qn_ref[...]
        for h in range(ATT_H):
            hs = slice(h * HD, (h + 1) * HD)
            lg = lax.dot_general(qn[:, hs].astype(BF16), kn_ref[0:klen, hs].astype(BF16), _LANES,
                                 preferred_element_type=F32) * (HD ** -0.5)
            lg = jnp.where(sel, lg, -jnp.inf)
            m = jnp.max(lg, axis=-1, keepdims=True)
            p = jnp.exp(lg - m)
            l = jnp.sum(p, axis=-1, keepdims=True)
            o = jnp.dot(p.astype(BF16), v_ref[0:klen, hs].astype(BF16), preferred_element_type=F32)
            o_ref[:, hs] = o / l

    for var in range(nvar):
        pl.when(it // per == var)(functools.partial(body, (var + 1) * step))


def _dsa_attn(qn, kn, qir, kwr, P, batch, seq, tq):
    topk = min(TOPK_MAX, seq // 4)
    nq = seq // tq
    nvar = min(4, nq)
    assert seq // nvar >= topk
    vblk = C_DV // 256
    return pl.pallas_call(
        functools.partial(_dsa_attn_kernel, tq=tq, topk=topk, nvar=nvar),
        grid=(batch, nq),
        in_specs=[pl.BlockSpec((tq, 256), lambda b, i: (b * nq + i, 0)),
                  pl.BlockSpec((tq, 256), lambda b, i: (b * nq + i, 0)),
                  pl.BlockSpec((tq, 128), lambda b, i: (b * nq + i, 0)),
                  pl.BlockSpec((seq, 256), lambda b, i: (b, 0)),
                  pl.BlockSpec((seq, 256), lambda b, i: (b, vblk)),
                  pl.BlockSpec((seq, 128), lambda b, i: (b, 0))],
        out_specs=pl.BlockSpec((tq, 256), lambda b, i: (b * nq + i, 0)),
        out_shape=jax.ShapeDtypeStruct((batch * seq, 256), F32),
        scratch_shapes=[pltpu.VMEM((tq, seq), I32)],
        compiler_params=_cparams(2), name="dsa_attn",
    )(qn, qir, kwr, kn, P, kwr)


def _split3_rhs(m, x):
    hi = x.astype(BF16)
    r = x - hi.astype(F32)
    mid = r.astype(BF16)
    lo = (r - mid.astype(F32)).astype(BF16)
    return (jnp.dot(m, hi, preferred_element_type=F32) + jnp.dot(m, mid, preferred_element_type=F32)
            + jnp.dot(m, lo, preferred_element_type=F32))


_LANES = (((1,), (1,)), ((), ()))
_ROWS = (((0,), (0,)), ((), ()))


def _gla_kernel(v_ref, r_ref, qk_ref, a_ref, st0_ref, a2h_ref, a2l_ref, ab_ref, ng_ref,
                tri_ref, ones_ref, amask_ref, hm128_ref, hm256_ref, bd_ref, wavg_ref,
                o_ref, st_ref, *, tile, chunk, lvalid):
    @pl.when(pl.program_id(1) == 0)
    def _():
        st_ref[...] = st0_ref[...]

    nch = tile // chunk
    z = _dot3(a_ref[...], a2h_ref[...], a2l_ref[...]) + ab_ref[...]
    la = (jnp.minimum(z, 0.0) - jnp.log1p(jnp.exp(-jnp.abs(z)))) * (1.0 / GLA_TAU)
    q = qk_ref[:, 0:128] * (GLA_DK ** -0.5)
    k = qk_ref[:, 128:256]
    v = v_ref[...]
    if lvalid < tile:
        keep = lax.broadcasted_iota(I32, (tile, 128), 0) < lvalid
        la = jnp.where(keep, la, 0.0)
        k = jnp.where(keep, k, 0.0)
        v = jnp.where(lax.broadcasted_iota(I32, (tile, 256), 0) < lvalid, v, 0.0)
    b = _split3_rhs(tri_ref[...], la)
    tot = _split3_rhs(ones_ref[...], la)
    qe = q * jnp.exp(b)
    kinv = (k * jnp.exp(-b)).astype(BF16)
    kd = (k * jnp.exp(tot - b)).astype(BF16)
    vb = v.astype(BF16)
    hm128 = hm128_ref[...]
    hm256 = hm256_ref[...]
    qblk = jnp.concatenate([qe[c * chunk:(c + 1) * chunk] * hm128[h:h + 1]
                            for c in range(nch) for h in range(GLA_H)], axis=0).astype(BF16)
    att = lax.dot_general(qblk, kinv, _LANES, preferred_element_type=F32) * amask_ref[...]
    intra = jnp.dot(att.astype(BF16), vb, preferred_element_type=F32)
    qeb = qe.astype(BF16)
    st = st_ref[...]
    for c in range(nch):
        rows = slice(c * chunk, (c + 1) * chunk)
        o = lax.dot_general(qeb[rows], st.astype(BF16), _LANES, preferred_element_type=F32)
        for h in range(GLA_H):
            r0 = (c * GLA_H + h) * chunk
            o = o + intra[r0:r0 + chunk] * hm256[h:h + 1]
        o_ref[rows, :] = o
        upd = lax.dot_general(vb[rows], kd[rows], _ROWS, preferred_element_type=F32)
        st = st * jnp.exp(tot[c * chunk:c * chunk + 1]) + upd * bd_ref[...]
    st_ref[...] = st
    o = o_ref[...]
    ms = _split_dot(o * o, wavg_ref[...], 2)
    r = r_ref[...]
    o_ref[...] = o * lax.rsqrt(ms + EPS) * ng_ref[...] * (r * _sigmoid(r))


def _gla(P, st0, w, consts, batch, seq, tg, chunk, lvalid):
    nt = seq // tg

    def col(off, wd):
        return pl.BlockSpec((tg, wd), lambda b, j: (b * nt + j, off // wd))

    gm = consts["gla"][(tg, chunk)]
    cs = [w["gla_a2h"], w["gla_a2l"], w["gla_ab"], w["gla_ng"],
          gm["tri"], gm["ones"], gm["amask"], consts["hm128"], consts["hm8"], consts["bd"], consts["wavg"]]
    return pl.pallas_call(
        functools.partial(_gla_kernel, tile=tg, chunk=chunk, lvalid=lvalid),
        grid=(batch, nt),
        in_specs=[col(C_GV, 256), col(C_GR, 256), col(C_GQK, 256), col(C_GA, 128),
                  pl.BlockSpec((None, 256, 128), lambda b, j: (b, 0, 0))] + [_const_spec(a) for a in cs],
        out_specs=[pl.BlockSpec((tg, 256), lambda b, j: (b * nt + j, 0)),
                   pl.BlockSpec((None, 256, 128), lambda b, j: (b, 0, 0))],
        out_shape=[jax.ShapeDtypeStruct((batch * seq, 256), F32),
                   jax.ShapeDtypeStruct((batch, 256, 128), F32)],
        compiler_params=_cparams(2), name="gla",
    )(P, P, P, P, st0, *cs)


def _gelu_tanh(x):
    return 0.5 * x * (1.0 + jnp.tanh(math.sqrt(2.0 / math.pi) * (x + 0.044715 * (x * x * x))))


def _s5_kernel(u_ref, x0_ref, a_ref, bh_ref, bl_ref, c_ref, d_ref, gw_ref, gb_ref,
               o_ref, xf_ref, st_s, bur_s, bui_s, xr_s, xi_s, *, tile, last_row):
    @pl.when(pl.program_id(1) == 0)
    def _():
        st_s[...] = x0_ref[...]

    u = u_ref[...]
    bu = _dot3(u, bh_ref[...], bl_ref[...])
    bur_s[...] = bu[:, 0:S5_N]
    bui_s[...] = bu[:, S5_N:2 * S5_N]
    ar = a_ref[:, 0:S5_N]
    ai = a_ref[:, S5_N:2 * S5_N]

    def step(t, carry):
        xr, xi = carry
        row = pl.ds(t, 1)
        nr = ar * xr - ai * xi + bur_s[row, :]
        ni = ar * xi + ai * xr + bui_s[row, :]
        xr_s[row, :] = nr
        xi_s[row, :] = ni
        return nr, ni

    xr, xi = lax.fori_loop(0, tile, step, (st_s[:, 0:S5_N], st_s[:, S5_N:2 * S5_N]))
    st_s[:, 0:S5_N] = xr
    st_s[:, S5_N:2 * S5_N] = xi
    y = (jnp.dot(xr_s[...].astype(BF16), c_ref[0:S5_N, :], preferred_element_type=F32)
         + jnp.dot(xi_s[...].astype(BF16), c_ref[S5_N:2 * S5_N, :], preferred_element_type=F32)
         + d_ref[...] * u)
    z = _gelu_tanh(y)
    gate = jnp.dot(z.astype(BF16), gw_ref[...], preferred_element_type=F32) + gb_ref[...]
    o_ref[...] = z * _sigmoid(gate)
    xf_ref[:, 0:S5_N] = xr_s[last_row:last_row + 1, :]
    xf_ref[:, S5_N:2 * S5_N] = xi_s[last_row:last_row + 1, :]


def _s5(P, x0, w, batch, seq, tile, lvalid):
    nt = seq // tile
    last_row = (lvalid - 1) % tile
    cs = [w["s5_a"], w["s5_bh"], w["s5_bl"], w["s5_c"], w["s5_d"], w["s5_gw"], w["s5_gb"]]
    return pl.pallas_call(
        functools.partial(_s5_kernel, tile=tile, last_row=last_row),
        grid=(batch, nt),
        in_specs=[pl.BlockSpec((tile, 256), lambda b, j: (b * nt + j, C_S5 // 256)),
                  pl.BlockSpec((None, 1, 2 * S5_N), lambda b, j: (b, 0, 0))] + [_const_spec(a) for a in cs],
        out_specs=[pl.BlockSpec((tile, 256), lambda b, j: (b * nt + j, 0)),
                   pl.BlockSpec((None, 1, 2 * S5_N), lambda b, j: (b, 0, 0))],
        out_shape=[jax.ShapeDtypeStruct((batch * seq, 256), F32),
                   jax.ShapeDtypeStruct((batch, 1, 2 * S5_N), F32)],
        scratch_shapes=[pltpu.VMEM((1, 2 * S5_N), F32)] + [pltpu.VMEM((tile, S5_N), F32)] * 4,
        compiler_params=_cparams(2), name="s5",
    )(P, x0, *cs)


RW_CH = 16
RW_SC = 4 * RW_CH


def _rwkv_chunked(sf_ref, r_s, k_s, v_s, al_s, be_s, lw_s, y_s, sbd_s, hm, tri, onesb, strict, incl, bd,
                  nb, tile):
    nh = RW_H

    def blk(x):
        return jnp.concatenate([x[RW_CH * c:RW_CH * (c + 1)] * hm[h:h + 1]
                                for c in range(4) for h in range(nh)], axis=0)

    def rep(x):
        return jnp.concatenate([x[RW_CH * c:RW_CH * (c + 1)] for c in range(4) for _ in range(nh)], axis=0)

    def unblk(x, c):
        r0 = nh * RW_CH * c
        return (x[r0:r0 + RW_CH] + x[r0 + RW_CH:r0 + 2 * RW_CH]
                + x[r0 + 2 * RW_CH:r0 + 3 * RW_CH] + x[r0 + 3 * RW_CH:r0 + 4 * RW_CH])

    def mm(a, b):
        return jnp.dot(a, b, preferred_element_type=F32)

    for b in range(nb):
        sbd_s[b] = jnp.concatenate([sf_ref[b]] * nh, axis=0) * bd

    def superchunk(sc, carry):
        r0 = pl.multiple_of(sc * RW_SC, RW_SC)
        rows = pl.ds(r0, RW_SC)
        for b in range(nb):
            lw = lw_s[b, rows, :]
            cum = _split3_rhs(tri, lw)
            tot = _split3_rhs(onesb, lw)
            rr, kx, vv = r_s[b, rows, :], k_s[b, rows, :], v_s[b, rows, :]
            al, be = al_s[b, rows, :], be_s[b, rows, :]
            pinv = jnp.exp(-cum)
            pend = jnp.exp(tot - cum)
            ab = al * jnp.exp(cum - lw)
            rb = rr * jnp.exp(cum)
            bt, kt, bp, kp = be * pinv, kx * pinv, be * pend, kx * pend
            ablk = blk(ab)
            lhs = jnp.concatenate([ablk, blk(rb)], axis=0).astype(BF16)
            rhs = jnp.concatenate([rep(bt), rep(kt)], axis=0).astype(BF16)
            g = lax.dot_general(lhs, rhs, _LANES, preferred_element_type=F32)
            mb = g[0:256, 0:256] * strict
            mk = g[0:256, 256:512] * strict
            myb = (g[256:512, 0:256] * incl).astype(BF16)
            myk = (g[256:512, 256:512] * incl).astype(BF16)
            u, mp = mb, mb
            for _ in range(3):
                mpb = mp.astype(BF16)
                mp = mm(mpb, mpb)
                u = u + mp + mm(u.astype(BF16), mp.astype(BF16))
            ub = u.astype(BF16)
            ab1 = ablk + mm(ub, ablk.astype(BF16))
            vblk = blk(vv).astype(BF16)
            w0 = mm(mk.astype(BF16), vblk)
            z0 = w0 + mm(ub, w0.astype(BF16))
            y0 = mm(myk, vblk)
            S = sbd_s[b]
            for c in range(4):
                cr = slice(RW_CH * c, RW_CH * (c + 1))
                lc = jnp.concatenate([unblk(ab1, c), rb[cr]], axis=0).astype(BF16)
                s_hi = S.astype(BF16)
                s_lo = (S - s_hi.astype(F32)).astype(BF16)
                x = (lax.dot_general(lc, s_hi, _LANES, preferred_element_type=F32)
                     + lax.dot_general(lc, s_lo, _LANES, preferred_element_type=F32))
                zc = x[0:RW_CH] + unblk(z0, c)
                zblk = jnp.concatenate([zc * hm[h:h + 1] for h in range(nh)], axis=0).astype(BF16)
                d0 = nh * RW_CH * c
                yz = mm(myb[d0:d0 + nh * RW_CH, d0:d0 + nh * RW_CH], zblk)
                yc = (x[RW_CH:2 * RW_CH] + unblk(y0, c)
                      + yz[0:RW_CH] + yz[RW_CH:2 * RW_CH] + yz[2 * RW_CH:3 * RW_CH] + yz[3 * RW_CH:4 * RW_CH])
                y_s[b, pl.ds(r0 + RW_CH * c, RW_CH), :] = yc
                upd = lax.dot_general(jnp.concatenate([zc, vv[cr]], axis=0).astype(BF16),
                                      jnp.concatenate([bp[cr], kp[cr]], axis=0).astype(BF16),
                                      _ROWS, preferred_element_type=F32)
                S = S * jnp.exp(tot[RW_CH * c:RW_CH * c + 1]) + upd * bd
            sbd_s[b] = S
        return carry

    lax.fori_loop(0, tile // RW_SC, superchunk, 0)
    for b in range(nb):
        S = sbd_s[b]
        sf_ref[b] = S[0:64] + S[64:128] + S[128:192] + S[192:256]


def _rwkv_kernel(r_ref, k_ref, v_ref, lo_ref, s0_ref, prev_ref, mu_ref, w0_ref, a0_ref,
                 w2h_ref, w2l_ref, a2h_ref, a2l_ref, g2h_ref, g2l_ref, kkp_ref, ka_ref, rk_ref, ng_ref,
                 wones_ref, wavg_ref, idt_ref, hm_ref, tri_ref, ones_ref, strict_ref, incl_ref, bd_ref,
                 o_ref, sf_ref, r_s, k_s, v_s, kk_s, ka_s, w_s, y_s, g_s, bo_s, prev_s, sbd_s,
                 *, nb, tile, nsteps, chunked):
    @pl.when(pl.program_id(1) == 0)
    def _():
        sf_ref[...] = s0_ref[...]
        prev_s[...] = prev_ref[...]

    wones = wones_ref[...]
    wavg = wavg_ref[...]
    idt = idt_ref[...]
    row0_256 = lax.broadcasted_iota(I32, (tile, 256), 0) == 0
    row0_128 = lax.broadcasted_iota(I32, (tile, 128), 0) == 0

    def shift_mix(p, prev_row, mu, row0):
        sh = jnp.where(row0, prev_row, pltpu.roll(p, 1, 0))
        return p + (sh - p) * mu

    for b in range(nb):
        pr, pk, pv, plo = r_ref[b], k_ref[b], v_ref[b], lo_ref[b]
        r = shift_mix(pr, prev_s[b, :, 0:256], mu_ref[:, 0:256], row0_256)
        k = shift_mix(pk, prev_s[b, :, 256:512], mu_ref[:, 256:512], row0_256)
        v = shift_mix(pv, prev_s[b, :, 512:768], mu_ref[:, 512:768], row0_256)
        lo = shift_mix(plo, prev_s[b, :, 768:896], mu_ref[:, 768:896], row0_128)
        prev_s[b, :, 0:256] = pr[tile - 1:tile, :]
        prev_s[b, :, 256:512] = pk[tile - 1:tile, :]
        prev_s[b, :, 512:768] = pv[tile - 1:tile, :]
        prev_s[b, :, 768:896] = plo[tile - 1:tile, :]
        wl = w0_ref[...] + _dot3(jnp.tanh(lo), w2h_ref[...], w2l_ref[...])
        w = -_softplus(-wl) - 0.5
        a = _sigmoid(a0_ref[...] + _dot3(lo, a2h_ref[...], a2l_ref[...]))
        g = _dot3(_sigmoid(lo), g2h_ref[...], g2l_ref[...])
        kk = k * kkp_ref[...]
        kk = kk * lax.rsqrt(_split_dot(kk * kk, wones, 3) + EPS)
        k2 = k * (1.0 + (a - 1.0) * ka_ref[...])
        bonus = _split_dot(r * k2 * rk_ref[...], wones, 3) * v
        r_s[b] = r
        k_s[b] = k2
        v_s[b] = v
        kk_s[b] = -kk
        ka_s[b] = kk * a
        w_s[b] = -jnp.exp(w)
        if nsteps < tile:
            y_s[b] = jnp.zeros((tile, 256), F32)
        g_s[b] = g
        bo_s[b] = bonus

    def step(t, carry):
        for b in range(nb):
            row = pl.ds(t, 1)
            S = sf_ref[b]
            sa = _split_dot(S * kk_s[b, row, :], wones, 2)
            vcol = _split_dot(idt * v_s[b, row, :], wones, 2)
            Sn = S * jnp.exp(w_s[b, row, :]) + sa * ka_s[b, row, :] + vcol * k_s[b, row, :]
            yb = jnp.dot((Sn * r_s[b, row, :]).astype(BF16), wones, preferred_element_type=F32)
            y_s[b, row, :] = jnp.sum(yb * idt, axis=0, keepdims=True)
            sf_ref[b] = Sn
        return carry

    if chunked:
        _rwkv_chunked(sf_ref, r_s, k_s, v_s, kk_s, ka_s, w_s, y_s, sbd_s, hm_ref[...], tri_ref[...],
                      ones_ref[...], strict_ref[...], incl_ref[...], bd_ref[...], nb, tile)
    else:
        lax.fori_loop(0, nsteps, step, 0)
    for b in range(nb):
        y = y_s[b]
        mu = _split_dot(y, wavg, 3)
        yc = y - mu
        var = _split_dot(yc * yc, wavg, 3)
        o_ref[b] = (yc * lax.rsqrt(var + RW_GN_EPS) * ng_ref[...] + bo_s[b]) * g_s[b]


def _rwkv(P3, s0, prev, w, consts, nb, tile, lvalid):
    batch, seq, _ = P3.shape
    nt = seq // tile
    nsteps = tile if lvalid >= seq else lvalid
    cs = [w["rw_mu"], w["rw_w0"], w["rw_a0"], w["rw_w2h"], w["rw_w2l"], w["rw_a2h"], w["rw_a2l"],
          w["rw_g2h"], w["rw_g2l"], w["rw_kk"], w["rw_ka"], w["rw_rk"], w["rw_ng"],
          consts["wones"], consts["wavg"], consts["idt"], consts["hm8"], consts["rw_tri"], consts["rw_ones"],
          consts["rw_strict"], consts["rw_incl"], consts["bd256"]]
    chunked = nsteps == tile and tile % RW_SC == 0

    def col(off, wd):
        return pl.BlockSpec((nb, tile, wd), lambda g, j: (g, j, off // wd))

    big = lambda: pltpu.VMEM((nb, tile, 256), F32)
    return pl.pallas_call(
        functools.partial(_rwkv_kernel, nb=nb, tile=tile, nsteps=nsteps, chunked=chunked),
        grid=(batch // nb, nt),
        in_specs=[col(C_RW, 256), col(C_RW + 256, 256), col(C_RW + 512, 256), col(C_RWLO, 128),
                  pl.BlockSpec((nb, RW_N, 256), lambda g, j: (g, 0, 0)),
                  pl.BlockSpec((nb, 1, RW_COLS), lambda g, j: (g, 0, 0))] + [_const_spec(a) for a in cs],
        out_specs=[pl.BlockSpec((nb, tile, 256), lambda g, j: (g, j, 0)),
                   pl.BlockSpec((nb, RW_N, 256), lambda g, j: (g, 0, 0))],
        out_shape=[jax.ShapeDtypeStruct((batch, seq, 256), F32),
                   jax.ShapeDtypeStruct((batch, RW_N, 256), F32)],
        scratch_shapes=[big(), big(), big(), big(), big(), big(), big(), big(), big(),
                        pltpu.VMEM((nb, 1, RW_COLS), F32),
                        pltpu.VMEM((nb, 256, 256) if chunked else (1, 8, 128), F32)],
        compiler_params=_cparams(2), name="rwkv",
    )(P3, P3, P3, P3, s0, prev, *cs)


def _merge_kernel(x_ref, sc_ref, sh_ref, gt_ref, g_ref, og_ref, oa_ref, os_ref, or_ref,
                  wg_ref, wbr_ref, wo_ref, o_ref):
    x = x_ref[...]
    h = _norm_mod(x, g_ref[...], sc_ref[...], sh_ref[...]).astype(BF16)
    merged = None
    for b, oref in enumerate((og_ref, oa_ref, os_ref, or_ref)):
        gate = _sigmoid(jnp.dot(h, wg_ref[:, b * D_MODEL:(b + 1) * D_MODEL], preferred_element_type=F32))
        proj = jnp.dot(oref[...].astype(BF16), wbr_ref[b], preferred_element_type=F32)
        merged = gate * proj if merged is None else merged + gate * proj
    y = jnp.dot(merged.astype(BF16), wo_ref[...], preferred_element_type=F32)
    o_ref[...] = x + gt_ref[...] * y


def _merge(x, mod, g, outs, w, tm, seq):
    rows = x.shape[0]
    row256 = pl.BlockSpec((tm, 256), lambda i: (i, 0))
    return pl.pallas_call(
        _merge_kernel,
        grid=(rows // tm,),
        in_specs=[pl.BlockSpec((tm, D_MODEL), lambda i: (i, 0)),
                  _mod_spec(mod, 1, tm, seq), _mod_spec(mod, 0, tm, seq), _mod_spec(mod, 2, tm, seq),
                  _const_spec(g), row256, row256, row256, row256,
                  _const_spec(w["w_gates"]), _const_spec(w["w_br"]), _const_spec(w["w_o"])],
        out_specs=pl.BlockSpec((tm, D_MODEL), lambda i: (i, 0)),
        out_shape=jax.ShapeDtypeStruct((rows, D_MODEL), F32),
        compiler_params=_cparams(1), name="merge",
    )(x, mod, mod, mod, g, *outs, w["w_gates"], w["w_br"], w["w_o"])


def _ffn_kernel(x_ref, sc_ref, sh_ref, gt_ref, g_ref, w1_ref, w2_ref, o_ref):
    x = x_ref[...]
    h = _norm_mod(x, g_ref[...], sc_ref[...], sh_ref[...]).astype(BF16)
    acc = None
    for c in range(D_FF // D_MODEL):
        cs = slice(c * D_MODEL, (c + 1) * D_MODEL)
        u = jnp.maximum(jnp.dot(h, w1_ref[:, cs], preferred_element_type=F32), 0.0)
        d = jnp.dot((u * u).astype(BF16), w2_ref[cs, :], preferred_element_type=F32)
        acc = d if acc is None else acc + d
    o_ref[...] = x + gt_ref[...] * acc


def _ffn(x, mod, g, w, tm, seq):
    rows = x.shape[0]
    return pl.pallas_call(
        _ffn_kernel,
        grid=(rows // tm,),
        in_specs=[pl.BlockSpec((tm, D_MODEL), lambda i: (i, 0)),
                  _mod_spec(mod, 4, tm, seq), _mod_spec(mod, 3, tm, seq), _mod_spec(mod, 5, tm, seq),
                  _const_spec(g), _const_spec(w["w_ff1"]), _const_spec(w["w_ff2"])],
        out_specs=pl.BlockSpec((tm, D_MODEL), lambda i: (i, 0)),
        out_shape=jax.ShapeDtypeStruct((rows, D_MODEL), F32),
        compiler_params=_cparams(1), name="ffn",
    )(x, mod, mod, mod, g, w["w_ff1"], w["w_ff2"])


def _ds_scores_kernel(pt_ref, q8_ref, w8_ref, kcur_ref, *refs, npages):
    pages, o_ref = refs[:npages], refs[npages]
    q8 = q8_ref[...]
    q8b = q8.astype(BF16)
    w8 = w8_ref[...] * (IDX_D ** -0.5)
    for p in range(npages):
        s = lax.dot_general(q8b, pages[p][...].astype(BF16), (((1,), (1,)), ((), ())),
                            preferred_element_type=F32)
        o_ref[p:p + 1, :] = jnp.sum(w8 * jnp.maximum(s, 0.0), axis=0, keepdims=True)
    s_cur = jnp.sum(q8 * kcur_ref[...], axis=-1, keepdims=True)
    i_cur = jnp.sum(w8 * jnp.maximum(s_cur, 0.0), axis=0, keepdims=True)
    lane = lax.broadcasted_iota(I32, (8, 128), 1)
    rowi = lax.broadcasted_iota(I32, (8, 128), 0)
    o_ref[npages:npages + 8, :] = jnp.where((lane == 0) & (rowi == 0), i_cur, -jnp.inf)


def _ds_scores(page_table, q8, w8, kcur, cache_idx, layer):
    n, npages = page_table.shape
    page_specs = [pl.BlockSpec((None, None, PAGE_SIZE, IDX_D), lambda i, pt, p=p: (layer, pt[i, p], 0, 0))
                  for p in range(npages)]
    return pl.pallas_call(
        functools.partial(_ds_scores_kernel, npages=npages),
        grid_spec=pltpu.PrefetchScalarGridSpec(
            num_scalar_prefetch=1, grid=(n,),
            in_specs=[pl.BlockSpec((None, IDX_H, IDX_D), lambda i, pt: (i, 0, 0)),
                      pl.BlockSpec((None, IDX_H, 1), lambda i, pt: (i, 0, 0)),
                      pl.BlockSpec((None, 1, IDX_D), lambda i, pt: (i, 0, 0))] + page_specs,
            out_specs=pl.BlockSpec((None, npages + 8, 128), lambda i, pt: (i, 0, 0))),
        out_shape=jax.ShapeDtypeStruct((n, npages + 8, 128), F32),
        compiler_params=_cparams(1), name="ds_scores",
    )(page_table, q8, w8, kcur, *([cache_idx] * npages))


def _ds_select_kernel(s_ref, o_ref, key_ref, *, topk, nvalid):
    rows, cols = s_ref.shape
    col = lax.broadcasted_iota(I32, (rows, cols), 1)
    valid = col < nvalid
    key_ref[...] = _score_keys(s_ref[...], valid)
    o_ref[...] = jnp.where(valid, _topk_select(key_ref, topk, col), 0.0)


def _ds_select(scores, topk, nvalid):
    rows, cols = scores.shape
    return pl.pallas_call(
        functools.partial(_ds_select_kernel, topk=topk, nvalid=nvalid),
        grid=(1,),
        in_specs=[pl.BlockSpec((rows, cols), lambda i: (0, 0))],
        out_specs=pl.BlockSpec((rows, cols), lambda i: (0, 0)),
        out_shape=jax.ShapeDtypeStruct((rows, cols), F32),
        scratch_shapes=[pltpu.VMEM((rows, cols), I32)],
        compiler_params=_cparams(1), name="ds_select",
    )(scores)


def _ds_attn_kernel(pt_ref, q_ref, kcur_ref, vcur_ref, m_ref, hm_ref, *refs, npages):
    kp, vp, o_ref = refs[:npages], refs[npages:2 * npages], refs[2 * npages]
    hm = hm_ref[...]
    qf = q_ref[...] * hm
    qb = qf.astype(BF16)
    sc = HD ** -0.5
    lg_cur = jnp.sum(qf * kcur_ref[...], axis=-1, keepdims=True) * sc
    cur_sel = m_ref[npages:npages + 1, 0:1] > 0.0
    mx = lg_cur
    lgs = []
    for p in range(npages):
        lg = lax.dot_general(qb, kp[p][...].astype(BF16), (((1,), (1,)), ((), ())),
                             preferred_element_type=F32) * sc
        lg = jnp.where(m_ref[p:p + 1, :] > 0.0, lg, -jnp.inf)
        lgs.append(lg)
        mx = jnp.maximum(mx, jnp.max(lg, axis=-1, keepdims=True))
    pc = jnp.where(cur_sel, jnp.exp(lg_cur - mx), 0.0)
    l = pc
    acc = pc * vcur_ref[...]
    for p in range(npages):
        pe = jnp.exp(lgs[p] - mx)
        l = l + jnp.sum(pe, axis=-1, keepdims=True)
        acc = acc + jnp.dot(pe.astype(BF16), vp[p][...].astype(BF16), preferred_element_type=F32)
    o_ref[...] = jnp.sum((acc / l) * hm, axis=0, keepdims=True)


def _ds_attn(page_table, q, kcur, vcur, mask, cache_k, cache_v, layer, headmask):
    n, npages = page_table.shape
    pspec = lambda p: pl.BlockSpec((None, None, PAGE_SIZE, 256), lambda i, pt, p=p: (layer, pt[i, p], 0, 0))
    row = pl.BlockSpec((None, 1, 256), lambda i, pt: (i, 0, 0))
    return pl.pallas_call(
        functools.partial(_ds_attn_kernel, npages=npages),
        grid_spec=pltpu.PrefetchScalarGridSpec(
            num_scalar_prefetch=1, grid=(n,),
            in_specs=[row, row, row,
                      pl.BlockSpec((None, npages + 8, 128), lambda i, pt: (i, 0, 0)),
                      pl.BlockSpec((8, 256), lambda i, pt: (0, 0))]
                     + [pspec(p) for p in range(npages)] + [pspec(p) for p in range(npages)],
            out_specs=row),
        out_shape=jax.ShapeDtypeStruct((n, 1, 256), F32),
        compiler_params=_cparams(1), name="ds_attn",
    )(page_table, q, kcur, vcur, mask, headmask, *([cache_k] * npages), *([cache_v] * npages))


def _constants():
    lane256 = np.arange(256)
    head = lane256 // 64
    wones = (head[:, None] == head[None, :]).astype(np.float32)
    idt = (np.arange(64)[:, None] == (lane256 % 64)[None, :]).astype(np.float32)
    e2 = ((np.arange(128) // 32)[:, None] == head[None, :]).astype(np.float32)
    bd = (head[:, None] == (np.arange(128) // 32)[None, :]).astype(np.float32)
    hm8 = (np.arange(8)[:, None] == head[None, :]).astype(np.float32)
    hm128 = (np.arange(8)[:, None] == (np.arange(128) // 32)[None, :]).astype(np.float32)

    def chunk_masks(tile, chunk, heads):
        t = np.arange(tile)
        same = (t[:, None] // chunk) == (t[None, :] // chunk)
        rows = np.arange(heads * tile)
        amask = ((rows // (heads * chunk))[:, None] == (t // chunk)[None, :]) & \
                ((t % chunk)[None, :] <= (rows % chunk)[:, None])
        return dict(tri=jnp.asarray(same & (t[None, :] <= t[:, None]), BF16), ones=jnp.asarray(same, BF16),
                    amask=jnp.asarray(amask, F32))

    gla = {key: chunk_masks(key[0], key[1], GLA_H) for key in ((256, 16), (SAMPLE_PAD, SAMPLE_PAD))}
    rwm = chunk_masks(RW_SC, RW_CH, RW_H)
    i256 = np.arange(256)
    same16 = (i256[:, None] // RW_CH) == (i256[None, :] // RW_CH)
    strict = same16 & ((i256 % RW_CH)[None, :] < (i256 % RW_CH)[:, None])
    incl = same16 & ((i256 % RW_CH)[None, :] <= (i256 % RW_CH)[:, None])

    def left(width, group, half):
        return jnp.asarray(((np.arange(width) % group) < half).astype(np.float32)[None, :])

    return dict(wones=jnp.asarray(wones, BF16), wavg=jnp.asarray(wones / 64.0, BF16), idt=jnp.asarray(idt),
                e2=jnp.asarray(e2, BF16), bd=jnp.asarray(bd), hm8=jnp.asarray(hm8), hm128=jnp.asarray(hm128),
                gla=gla, rw_tri=rwm["tri"], rw_ones=rwm["ones"], rw_strict=jnp.asarray(strict, F32),
                rw_incl=jnp.asarray(incl, F32), bd256=jnp.asarray(wones, F32),
                left_q=left(256, HD, ROT // 2), left_i=left(256, IDX_D, IDX_ROT // 2),
                left_kw=left(128, 128, IDX_ROT // 2))


def _rope_tables(pos, periodic):
    pos = pos.astype(F32)[:, None]

    def build(width, group, rot, extra=None):
        half = rot // 2
        freq = ROPE_THETA ** (-jnp.arange(half, dtype=F32) * (2.0 / rot))
        ang = pos * freq
        cos, sin = jnp.cos(ang), jnp.sin(ang)
        n = pos.shape[0]
        ones = jnp.ones((n, group - rot), F32)
        zeros = jnp.zeros((n, group - rot), F32)
        cg = jnp.concatenate([cos, cos, ones], axis=1)
        sg = jnp.concatenate([-sin, sin, zeros], axis=1)
        reps = width // group
        c, s = jnp.tile(cg, (1, reps)), jnp.tile(sg, (1, reps))
        if extra is not None:
            c, s = extra(c, s)
        return c, s

    cq, sq = build(256, HD, ROT)
    ci, si = build(256, IDX_D, IDX_ROT)

    def kw_extra(c, s):
        lane = jnp.arange(128)
        scale = jnp.where((lane >= IDX_D) & (lane < IDX_D + IDX_H), IDX_H ** -0.5, 1.0)
        keep = (lane < IDX_D)
        return jnp.where(keep, c, scale[None, :]), jnp.where(keep, s, 0.0)

    ckw, skw = build(128, IDX_D, IDX_ROT, kw_extra)
    return dict(cq=cq, sq=sq, ci=ci, si=si, ckw=ckw, skw=skw, periodic=periodic)


def _blockdiag(blocks):
    g, r, c = blocks.shape
    eye = jnp.eye(g, dtype=blocks.dtype)
    return jnp.einsum('grc,gh->grhc', blocks, eye).reshape(g * r, g * c)


def _layer_weights(l, p):
    w_in = p["w_in"][l]
    z = lambda n: jnp.zeros((D_MODEL, n), F32)
    w_mix = jnp.concatenate([
        w_in[:, 256:512], w_in[:, 528:784], w_in[:, 0:256],
        w_in[:, 784:1808], w_in[:, 1848:2104], w_in[:, 2104:3000],
        w_in[:, 512:528], z(112), w_in[:, 1808:1848], z(88)], axis=1).astype(BF16)
    w = dict(w_mix=w_mix, w_gates=w_in[:, 3000:7096].astype(BF16),
             w_br=p["w_br"][l].astype(BF16), w_o=p["w_o"][l].astype(BF16),
             w_ff1=p["w_ff1"][l].astype(BF16), w_ff2=p["w_ff2"][l].astype(BF16),
             norm1_g=p["norm1_g"][l][None, :], norm2_g=p["norm2_g"][l][None, :])
    a2 = jnp.zeros((128, 128), F32).at[0:GLA_RANK].set(p["gla_a2"][l])
    w["gla_a2h"], w["gla_a2l"] = _hilo(a2)
    w["gla_ab"] = p["gla_ab"][l][None, :]
    w["gla_ng"] = jnp.tile(p["gla_ng"][l], GLA_H)[None, :]
    w["att_qg"] = jnp.tile(p["att_qg"][l], ATT_H)[None, :]
    w["att_kg"] = jnp.tile(p["att_kg"][l], ATT_H)[None, :]
    dt = jnp.exp(p["s5_log_dt"][l])[:, None]
    lr = jnp.minimum(p["s5_a_re"][l], -1e-4)
    li = p["s5_a_im"][l]
    mag = jnp.exp(lr * dt)
    abr, abi = mag * jnp.cos(li * dt), mag * jnp.sin(li * dt)
    den = lr * lr + li * li
    fr = ((abr - 1.0) * lr + abi * li) / den
    fi = (abi * lr - (abr - 1.0) * li) / den
    b_re, b_im = p["s5_b_re"][l], p["s5_b_im"][l]
    bbr = fr[..., None] * b_re - fi[..., None] * b_im
    bbi = fr[..., None] * b_im + fi[..., None] * b_re
    bmat = jnp.concatenate([_blockdiag(bbr.transpose(0, 2, 1)), _blockdiag(bbi.transpose(0, 2, 1))], axis=1)
    w["s5_bh"], w["s5_bl"] = _hilo(bmat)
    w["s5_a"] = jnp.concatenate([abr.reshape(1, -1), abi.reshape(1, -1)], axis=1)
    w["s5_c"] = jnp.concatenate([_blockdiag(p["s5_c_re"][l].transpose(0, 2, 1)),
                                 -_blockdiag(p["s5_c_im"][l].transpose(0, 2, 1))], axis=0).astype(BF16)
    w["s5_d"] = p["s5_d"][l].reshape(1, -1)
    w["s5_gw"] = p["s5_glu_w"][l].astype(BF16)
    w["s5_gb"] = p["s5_glu_b"][l][None, :]
    w["rw_mu"] = p["rw_mu"][l][None, :]
    w["rw_w0"] = p["rw_w0"][l][None, :]
    w["rw_a0"] = p["rw_a0"][l][None, :]
    lo = jnp.zeros((128, 256), F32)
    w["rw_w2h"], w["rw_w2l"] = _hilo(lo.at[0:RW_WR].set(p["rw_w2"][l]))
    w["rw_a2h"], w["rw_a2l"] = _hilo(lo.at[RW_WR:RW_WR + RW_AR].set(p["rw_a2"][l]))
    w["rw_g2h"], w["rw_g2l"] = _hilo(lo.at[RW_WR + RW_AR:128].set(p["rw_g2"][l]))
    for nm in ("rw_kk", "rw_ka", "rw_rk", "rw_ng"):
        w[nm] = p[nm][l][None, :]
    return w


def _mix_and_ffn(x, mod, w, consts, P, o_att, st_gla0, st_s50, st_rw0, prev, batch, seq, lvalid, tm, tiles):
    o_gla, st_gla = _gla(P, st_gla0, w, consts, batch, seq, tiles["gla"], tiles["chunk"], lvalid)
    o_s5, st_s5 = _s5(P, st_s50, w, batch, seq, tiles["s5"], lvalid)
    o_rw, st_rw = _rwkv(P.reshape(batch, seq, NP_COLS), st_rw0, prev, w, consts, tiles["nb"], tiles["rw"], lvalid)
    x = _merge(x, mod, w["norm1_g"], (o_gla, o_att, o_s5, o_rw.reshape(batch * seq, 256)), w, tm, seq)
    x = _ffn(x, mod, w["norm2_g"], w, tm, seq)
    return x, st_gla, st_s5, st_rw


def _gla_state_out(st):
    n = st.shape[0]
    s = st.reshape(n, GLA_H, GLA_DV, GLA_H, GLA_DK)
    s = jnp.stack([s[:, h, :, h, :] for h in range(GLA_H)], axis=1)
    return s.transpose(0, 1, 3, 2)


def _gla_state_in(s):
    eye = jnp.eye(GLA_H, dtype=s.dtype)
    n = s.shape[0]
    return jnp.einsum('nhkv,hg->nhvgk', s, eye).reshape(n, GLA_H * GLA_DV, GLA_H * GLA_DK)


def _rw_state_out(st):
    n = st.shape[0]
    return st.reshape(n, RW_N, RW_H, RW_N).transpose(0, 2, 1, 3)


def _rw_state_in(s):
    n = s.shape[0]
    return s.transpose(0, 2, 1, 3).reshape(n, RW_N, RW_H * RW_N)


def _forward(x_prompt, x_sample, c_prompt, c_sample, cache_k, cache_v, cache_idx, state_gla,
             state_s5_re, state_s5_im, state_rwkv, state_shift, page_table, p):
    B, S, _ = x_prompt.shape
    N = x_sample.shape[0]
    depth = p["w_in"].shape[0]
    past = page_table.shape[1] * PAGE_SIZE
    consts = _constants()
    mod_all = _modulation(jnp.concatenate([c_prompt, c_sample], axis=0), p["ada_w"], p["ada_b"])
    tabs_p = _rope_tables(jnp.arange(S), True)
    tabs_s = _rope_tables(jnp.full((N * SAMPLE_PAD,), past), False)
    ck = cache_k.reshape(*cache_k.shape[:3], ATT_H * HD)
    cv = cache_v.reshape(*cache_v.shape[:3], ATT_H * HD)

    tm_p = min(512, S)
    tm_s = min(128, N * SAMPLE_PAD)
    tiles_p = dict(gla=min(256, S), chunk=16, s5=min(256, S), rw=min(128, S), nb=min(4, B))
    tiles_s = dict(gla=SAMPLE_PAD, chunk=SAMPLE_PAD, s5=SAMPLE_PAD, rw=SAMPLE_PAD, nb=8)
    tq = min(128, S)

    xp = x_prompt.reshape(B * S, D_MODEL)
    xs = jnp.pad(x_sample, ((0, 0), (0, SAMPLE_PAD - 1), (0, 0))).reshape(N * SAMPLE_PAD, D_MODEL)
    outs_p, outs_s = [], []
    for l in range(depth):
        w = _layer_weights(l, p)
        mod = mod_all[l, :B].reshape(B, 1, 6 * D_MODEL)
        P = _inproj(xp, mod, w["norm1_g"], w["w_mix"], tm_p, S)
        qn, kn, qir, kwr = _dsa_prep(P, tabs_p, consts, w["att_qg"], w["att_kg"], tm_p, S)
        o_att = _dsa_attn(qn, kn, qir, kwr, P, B, S, tq)
        xp, st_gla, st_s5, st_rw = _mix_and_ffn(
            xp, mod, w, consts, P, o_att,
            jnp.zeros((B, 256, 128), F32), jnp.zeros((B, 1, 2 * S5_N), F32),
            jnp.zeros((B, RW_N, 256), F32), jnp.zeros((B, 1, RW_COLS), F32), B, S, S, tm_p, tiles_p)
        P3 = P.reshape(B, S, NP_COLS)
        outs_p.append((kn.reshape(B, S, ATT_H, HD), P3[:, :, C_DV:C_DV + 256].reshape(B, S, ATT_H, HD),
                       kwr.reshape(B, S, 128)[:, :, :IDX_D], _gla_state_out(st_gla),
                       st_s5[:, 0, :S5_N].reshape(B, S5_G, S5_P), st_s5[:, 0, S5_N:].reshape(B, S5_G, S5_P),
                       _rw_state_out(st_rw), P3[:, S - 1, C_RW:C_RW + RW_COLS]))
        mod = jnp.repeat(mod_all[l, B:], SAMPLE_PAD, axis=0)
        P = _inproj(xs, mod, w["norm1_g"], w["w_mix"], tm_s, SAMPLE_PAD)
        qn, kn, qir, kwr = _dsa_prep(P, tabs_s, consts, w["att_qg"], w["att_kg"], tm_s, SAMPLE_PAD)
        first = lambda a: a.reshape(N, SAMPLE_PAD, a.shape[-1])[:, 0]
        qn1, kn1, qir1, kwr1, P1 = first(qn), first(kn), first(qir), first(kwr), first(P)
        v1 = P1[:, C_DV:C_DV + 256]
        scores = _ds_scores(page_table, qir1.reshape(N, IDX_H, IDX_D),
                            kwr1[:, IDX_D:IDX_D + IDX_H].reshape(N, IDX_H, 1),
                            kwr1[:, :IDX_D].reshape(N, 1, IDX_D), cache_idx, l)
        ncols = scores.shape[1] * 128
        sel = _ds_select(scores.reshape(N, ncols), min(TOPK_MAX, (past + 1) // 4), past + 1)
        o1 = _ds_attn(page_table, qn1.reshape(N, 1, 256), kn1.reshape(N, 1, 256), v1.reshape(N, 1, 256),
                      sel.reshape(N, ncols // 128, 128), ck, cv, l, consts["hm8"])
        o_att = jnp.pad(o1, ((0, 0), (0, SAMPLE_PAD - 1), (0, 0))).reshape(N * SAMPLE_PAD, 256)
        x0 = jnp.concatenate([state_s5_re[l].reshape(N, 1, S5_N), state_s5_im[l].reshape(N, 1, S5_N)], axis=2)
        xs, st_gla, st_s5, st_rw = _mix_and_ffn(
            xs, mod, w, consts, P, o_att, _gla_state_in(state_gla[l]), x0, _rw_state_in(state_rwkv[l]),
            state_shift[l].reshape(N, 1, RW_COLS), N, SAMPLE_PAD, 1, tm_s, tiles_s)
        outs_s.append((kn1.reshape(N, 1, ATT_H, HD), v1.reshape(N, 1, ATT_H, HD), kwr1[:, None, :IDX_D],
                       _gla_state_out(st_gla), st_s5[:, 0, :S5_N].reshape(N, S5_G, S5_P),
                       st_s5[:, 0, S5_N:].reshape(N, S5_G, S5_P), _rw_state_out(st_rw),
                       P1[:, C_RW:C_RW + RW_COLS]))
    yp = xp.reshape(B, S, D_MODEL)
    ys = xs.reshape(N, SAMPLE_PAD, D_MODEL)[:, 0:1]
    stack = lambda lst, i: jnp.stack([s[i] for s in lst])
    return (yp, ys) + tuple(stack(outs_p, i) for i in range(8)) + tuple(stack(outs_s, i) for i in range(8))


def kernel(x_prompt, x_sample, c_prompt, c_sample, cache_k, cache_v, cache_idx, state_gla, state_s5_re, state_s5_im, state_rwkv, state_shift, page_table, ada_w, ada_b, norm1_g, norm2_g, w_in, gla_a2, gla_ab, gla_ng, att_qg, att_kg, s5_a_re, s5_a_im, s5_log_dt, s5_b_re, s5_b_im, s5_c_re, s5_c_im, s5_d, s5_glu_w, s5_glu_b, rw_mu, rw_w0, rw_w2, rw_a0, rw_a2, rw_g2, rw_kk, rw_ka, rw_rk, rw_ng, w_br, w_o, w_ff1, w_ff2):
    p = dict(ada_w=ada_w, ada_b=ada_b, norm1_g=norm1_g, norm2_g=norm2_g, w_in=w_in, gla_a2=gla_a2,
             gla_ab=gla_ab, gla_ng=gla_ng, att_qg=att_qg, att_kg=att_kg, s5_a_re=s5_a_re, s5_a_im=s5_a_im,
             s5_log_dt=s5_log_dt, s5_b_re=s5_b_re, s5_b_im=s5_b_im, s5_c_re=s5_c_re, s5_c_im=s5_c_im,
             s5_d=s5_d, s5_glu_w=s5_glu_w, s5_glu_b=s5_glu_b, rw_mu=rw_mu, rw_w0=rw_w0, rw_w2=rw_w2,
             rw_a0=rw_a0, rw_a2=rw_a2, rw_g2=rw_g2, rw_kk=rw_kk, rw_ka=rw_ka, rw_rk=rw_rk, rw_ng=rw_ng,
             w_br=w_br, w_o=w_o, w_ff1=w_ff1, w_ff2=w_ff2)
    return _forward(x_prompt, x_sample, c_prompt, c_sample, cache_k, cache_v, cache_idx, state_gla,
                    state_s5_re, state_s5_im, state_rwkv, state_shift, page_table, p)
```

```python
import functools
import math

import numpy as np
import jax
import jax.numpy as jnp
from jax import lax
from jax.experimental import pallas as pl
from jax.experimental.pallas import tpu as pltpu

F32 = jnp.float32
BF16 = jnp.bfloat16
I32 = jnp.int32

D_MODEL = 1024
BR_W = 256
GLA_H, GLA_DK, GLA_DV, GLA_RANK, GLA_TAU = 4, 32, 64, 16, 16.0
ATT_H, HD, ROT = 4, 64, 16
IDX_H, IDX_D, IDX_ROT = 8, 32, 8
TOPK_MAX = 256
ROPE_THETA = 500000.0
S5_G, S5_P, S5_CH = 16, 64, 16
S5_N = S5_G * S5_P
RW_H, RW_N, RW_WR, RW_AR, RW_GR = 4, 64, 32, 32, 64
RW_COLS = 896
D_FF = 4096
EPS = 1e-6
RW_GN_EPS = 64e-5
PAGE_SIZE = 128
INT_MIN = -(2 ** 31)
SAMPLE_PAD = 8

C_GV, C_GR, C_GQK = 0, 256, 512
C_DQ, C_DK, C_DV, C_DQI = 768, 1024, 1280, 1536
C_S5 = 1792
C_RW = 2048
C_RWLO = 2816
C_GA = 2944
C_DKW = 3072
NP_COLS = 3200
VMEM_LIMIT = 56 * 1024 * 1024


def _cparams(n_axes):
    return pltpu.CompilerParams(dimension_semantics=("arbitrary",) * n_axes,
                                vmem_limit_bytes=VMEM_LIMIT)


def _split_dot(x, w, terms):
    acc = None
    r = x
    for i in range(terms):
        hi = r.astype(BF16)
        d = jnp.dot(hi, w, preferred_element_type=F32)
        acc = d if acc is None else acc + d
        if i + 1 < terms:
            r = r - hi.astype(F32)
    return acc


def _dot3(x, w_hi, w_lo):
    x_hi = x.astype(BF16)
    x_lo = (x - x_hi.astype(F32)).astype(BF16)
    return (jnp.dot(x_hi, w_hi, preferred_element_type=F32)
            + jnp.dot(x_hi, w_lo, preferred_element_type=F32)
            + jnp.dot(x_lo, w_hi, preferred_element_type=F32))


def _hilo(w):
    hi = w.astype(BF16)
    return hi, (w - hi.astype(F32)).astype(BF16)


def _sigmoid(x):
    return 1.0 / (1.0 + jnp.exp(-x))


def _softplus(x):
    return jnp.maximum(x, 0.0) + jnp.log1p(jnp.exp(-jnp.abs(x)))


def _norm_mod(x, g, sc, sh):
    ms = jnp.mean(x * x, axis=-1, keepdims=True)
    return (x * lax.rsqrt(ms + EPS) * g) * (1.0 + sc) + sh


def _mod_spec(mod, j, tm, seq):
    if mod.ndim == 3:
        return pl.BlockSpec((None, 1, D_MODEL), lambda i: ((i * tm) // seq, 0, j))
    return pl.BlockSpec((tm, D_MODEL), lambda i: (i, j))


def _const_spec(a):
    nd = a.ndim
    return pl.BlockSpec(a.shape, lambda *_: (0,) * nd)


def _mod_kernel(c_ref, w_ref, b_ref, o_ref):
    o_ref[...] = jnp.dot(c_ref[...], w_ref[...].astype(BF16), preferred_element_type=F32) + b_ref[...]


def _modulation(c_all, ada_w, ada_b):
    depth = ada_w.shape[0]
    rows = c_all.shape[0]
    tn = 1536
    return pl.pallas_call(
        _mod_kernel,
        grid=(depth, 6 * D_MODEL // tn),
        in_specs=[pl.BlockSpec((rows, D_MODEL), lambda l, j: (0, 0)),
                  pl.BlockSpec((None, D_MODEL, tn), lambda l, j: (l, 0, j)),
                  pl.BlockSpec((None, 1, tn), lambda l, j: (l, 0, j))],
        out_specs=pl.BlockSpec((None, rows, tn), lambda l, j: (l, 0, j)),
        out_shape=jax.ShapeDtypeStruct((depth, rows, 6 * D_MODEL), F32),
        compiler_params=_cparams(2), name="modulation",
    )(c_all.astype(BF16), ada_w, ada_b.reshape(depth, 1, 6 * D_MODEL))


def _inproj_kernel(x_ref, sc_ref, sh_ref, g_ref, w_ref, o_ref):
    h = _norm_mod(x_ref[...], g_ref[...], sc_ref[...], sh_ref[...])
    o_ref[...] = jnp.dot(h.astype(BF16), w_ref[...], preferred_element_type=F32)


def _inproj(x, mod, g, w_mix, tm, seq):
    rows = x.shape[0]
    return pl.pallas_call(
        _inproj_kernel,
        grid=(rows // tm,),
        in_specs=[pl.BlockSpec((tm, D_MODEL), lambda i: (i, 0)),
                  _mod_spec(mod, 1, tm, seq), _mod_spec(mod, 0, tm, seq),
                  _const_spec(g), _const_spec(w_mix)],
        out_specs=pl.BlockSpec((tm, NP_COLS), lambda i: (i, 0)),
        out_shape=jax.ShapeDtypeStruct((rows, NP_COLS), F32),
        compiler_params=_cparams(1), name="inproj",
    )(x, mod, mod, g, w_mix)


def _rope_apply(x, cos, sn, left, shift):
    n = x.shape[-1]
    rot = jnp.where(left > 0.0, pltpu.roll(x, n - shift, 1), pltpu.roll(x, shift, 1))
    return x * cos + rot * sn


def _dsa_prep_kernel(q_ref, k_ref, v_ref, qi_ref, kw_ref, cq_ref, sq_ref, ci_ref, si_ref, ckw_ref, skw_ref,
                     lq_ref, li_ref, lkw_ref, qg_ref, kg_ref, wavg_ref,
                     qn_ref, kn_ref, qir_ref, kwr_ref, qh_ref, kh_ref, vh_ref, qih_ref, kib_ref):
    wavg = wavg_ref[...]

    def headnorm(x, g):
        ms = _split_dot(x * x, wavg, 3)
        return x * lax.rsqrt(ms + EPS) * g

    cq, sq, lq = cq_ref[...], sq_ref[...], lq_ref[...]
    qn = _rope_apply(headnorm(q_ref[...], qg_ref[...]), cq, sq, lq, ROT // 2)
    kn = _rope_apply(headnorm(k_ref[...], kg_ref[...]), cq, sq, lq, ROT // 2)
    qir = _rope_apply(qi_ref[...], ci_ref[...], si_ref[...], li_ref[...], IDX_ROT // 2)
    kwr = _rope_apply(kw_ref[...], ckw_ref[...], skw_ref[...], lkw_ref[...], IDX_ROT // 2)
    qn_ref[...] = qn
    kn_ref[...] = kn
    qir_ref[...] = qir
    kwr_ref[...] = kwr
    v = v_ref[...]
    for h in range(ATT_H):
        hs = slice(h * HD, (h + 1) * HD)
        qh_ref[h] = qn[:, hs].astype(BF16)
        kh_ref[h] = kn[:, hs].astype(BF16)
        vh_ref[h] = v[:, hs].astype(BF16)
    for h in range(IDX_H):
        qih_ref[h] = qir[:, h * IDX_D:(h + 1) * IDX_D].astype(BF16)
    kib_ref[...] = kwr.astype(BF16)


def _dsa_prep(P, tabs, consts, qg, kg, tm, seq):
    rows = P.shape[0]
    nt = seq // tm if tabs["periodic"] else None

    def tab_spec(w):
        if tabs["periodic"]:
            return pl.BlockSpec((tm, w), lambda i: (i % nt, 0))
        return pl.BlockSpec((tm, w), lambda i: (i, 0))

    def col(off, w):
        return pl.BlockSpec((tm, w), lambda i: (i, off // w))

    out256 = jax.ShapeDtypeStruct((rows, 256), F32)
    heads = lambda n, d: (pl.BlockSpec((n, tm, d), lambda i: (0, i, 0)), jax.ShapeDtypeStruct((n, rows, d), BF16))
    hq, hi = heads(ATT_H, HD), heads(IDX_H, IDX_D)
    return pl.pallas_call(
        _dsa_prep_kernel,
        grid=(rows // tm,),
        in_specs=[col(C_DQ, 256), col(C_DK, 256), col(C_DV, 256), col(C_DQI, 256), col(C_DKW, 128),
                  tab_spec(256), tab_spec(256), tab_spec(256), tab_spec(256), tab_spec(128), tab_spec(128),
                  _const_spec(consts["left_q"]), _const_spec(consts["left_i"]), _const_spec(consts["left_kw"]),
                  _const_spec(qg), _const_spec(kg), _const_spec(consts["wavg"])],
        out_specs=[pl.BlockSpec((tm, 256), lambda i: (i, 0))] * 3 + [pl.BlockSpec((tm, 128), lambda i: (i, 0))]
                  + [hq[0], hq[0], hq[0], hi[0], pl.BlockSpec((tm, 128), lambda i: (i, 0))],
        out_shape=[out256, out256, out256, jax.ShapeDtypeStruct((rows, 128), F32),
                   hq[1], hq[1], hq[1], hi[1], jax.ShapeDtypeStruct((rows, 128), BF16)],
        compiler_params=_cparams(1), name="dsa_prep",
    )(P, P, P, P, P, tabs["cq"], tabs["sq"], tabs["ci"], tabs["si"], tabs["ckw"], tabs["skw"],
      consts["left_q"], consts["left_i"], consts["left_kw"], qg, kg, consts["wavg"])


def _score_keys(scores, valid):
    s = jnp.where(scores == 0.0, 0.0, scores)
    bits = pltpu.bitcast(s, I32)
    key = bits ^ (jnp.right_shift(bits, 31) & 0x7FFFFFFF)
    return jnp.where(valid, key, INT_MIN)


def _topk_select(key_ref, k, col):
    rows, cols = key_ref.shape
    kf = float(k)
    nbits = max(1, int(math.ceil(math.log2(cols))))

    ng = 4 if rows % 32 == 0 else 1
    rg = rows // ng

    def count_ge(g, c):
        return jnp.sum(jnp.where(key_ref[g * rg:(g + 1) * rg, :] >= c, 1.0, 0.0), axis=-1, keepdims=True)

    bases = tuple(jnp.where(count_ge(g, jnp.zeros((rg, 1), I32)) >= kf, 0, INT_MIN).astype(I32)
                  for g in range(ng))

    def bit_step(i, bases):
        bit = lax.shift_left(jnp.int32(1), 30 - i)
        return tuple(jnp.where(count_ge(g, b | bit) >= kf, b | bit, b) for g, b in enumerate(bases))

    bases = lax.fori_loop(0, 31, bit_step, bases)
    thr = bases[0] if ng == 1 else jnp.concatenate(bases, axis=0)
    key = key_ref[...]
    need = kf - jnp.sum(jnp.where(key > thr, 1.0, 0.0), axis=-1, keepdims=True)
    excess = jnp.sum(jnp.where(key == thr, 1.0, 0.0), axis=-1, keepdims=True) > need

    def pos_step(i, pos):
        cand = pos + lax.shift_left(jnp.int32(1), nbits - 1 - i)
        hit = jnp.where(key_ref[...] == thr, jnp.where(col < cand, 1.0, 0.0), 0.0)
        return jnp.where(jnp.sum(hit, axis=-1, keepdims=True) < need, cand, pos)

    n_iter = jnp.where(jnp.max(jnp.where(excess, 1, 0)) > 0, nbits, 0)
    pos = lax.fori_loop(0, n_iter, pos_step, jnp.zeros((rows, 1), I32))
    pos = jnp.where(excess, pos, cols)
    return jnp.where(key > thr, 1.0, jnp.where(key == thr, jnp.where(col <= pos, 1.0, 0.0), 0.0))


def _dsa_attn_kernel(qh_ref, qih_ref, kwq_ref, kh_ref, vh_ref, kib_ref, o_ref, key_ref, *, tq, topk, nvar):
    it = pl.program_id(1)
    t0 = it * tq
    seq = kh_ref.shape[1]
    step = seq // nvar
    per = step // tq

    def body(klen):
        ki = kib_ref[0:klen, 0:IDX_D]
        kwq = kwq_ref[...]
        scores = jnp.zeros((tq, klen), F32)
        for h in range(IDX_H):
            s = lax.dot_general(qih_ref[h], ki, _LANES, preferred_element_type=F32)
            w = kwq[:, IDX_D + h:IDX_D + h + 1] * (IDX_D ** -0.5)
            scores = scores + w * jnp.maximum(s, 0.0)
        col = lax.broadcasted_iota(I32, (tq, klen), 1)
        row = t0 + lax.broadcasted_iota(I32, (tq, klen), 0)
        causal = col <= row
        keys = key_ref.at[:, pl.ds(0, klen)]
        keys[...] = _score_keys(scores, causal)
        sel = jnp.where(causal, _topk_select(keys, topk, col), 0.0) > 0.0
        for h in range(ATT_H):
            lg = lax.dot_general(qh_ref[h], kh_ref[h, 0:klen, :], _LANES,
                                 preferred_element_type=F32) * (HD ** -0.5)
            lg = jnp.where(sel, lg, -jnp.inf)
            m = jnp.max(lg, axis=-1, keepdims=True)
            p = jnp.exp(lg - m)
            l = jnp.sum(p, axis=-1, keepdims=True)
            o = jnp.dot(p.astype(BF16), vh_ref[h, 0:klen, :], preferred_element_type=F32)
            o_ref[:, h * HD:(h + 1) * HD] = o / l

    for var in range(nvar):
        pl.when(it // per == var)(functools.partial(body, (var + 1) * step))


def _dsa_attn(qh, kh, vh, qih, kwr, kib, batch, seq, tq):
    topk = min(TOPK_MAX, seq // 4)
    nq = seq // tq
    nvar = nq
    while seq // nvar < topk:
        nvar //= 2
    return pl.pallas_call(
        functools.partial(_dsa_attn_kernel, tq=tq, topk=topk, nvar=nvar),
        grid=(batch, nq),
        in_specs=[pl.BlockSpec((ATT_H, tq, HD), lambda b, i: (0, b * nq + i, 0)),
                  pl.BlockSpec((IDX_H, tq, IDX_D), lambda b, i: (0, b * nq + i, 0)),
                  pl.BlockSpec((tq, 128), lambda b, i: (b * nq + i, 0)),
                  pl.BlockSpec((ATT_H, seq, HD), lambda b, i: (0, b, 0)),
                  pl.BlockSpec((ATT_H, seq, HD), lambda b, i: (0, b, 0)),
                  pl.BlockSpec((seq, 128), lambda b, i: (b, 0))],
        out_specs=pl.BlockSpec((tq, 256), lambda b, i: (b * nq + i, 0)),
        out_shape=jax.ShapeDtypeStruct((batch * seq, 256), F32),
        scratch_shapes=[pltpu.VMEM((tq, seq), I32)],
        compiler_params=_cparams(2), name="dsa_attn",
    )(qh, qih, kwr, kh, vh, kib)


def _split3_rhs(m, x):
    hi = x.astype(BF16)
    r = x - hi.astype(F32)
    mid = r.astype(BF16)
    lo = (r - mid.astype(F32)).astype(BF16)
    return (jnp.dot(m, hi, preferred_element_type=F32) + jnp.dot(m, mid, preferred_element_type=F32)
            + jnp.dot(m, lo, preferred_element_type=F32))


_LANES = (((1,), (1,)), ((), ()))
_ROWS = (((0,), (0,)), ((), ()))


def _gla_kernel(v_ref, r_ref, qk_ref, a_ref, st0_ref, a2h_ref, a2l_ref, ab_ref, ng_ref,
                tri_ref, ones_ref, amask_ref, hm128_ref, hm256_ref, bd_ref, wavg_ref,
                o_ref, st_ref, *, tile, chunk, lvalid):
    @pl.when(pl.program_id(1) == 0)
    def _():
        st_ref[...] = st0_ref[...]

    nch = tile // chunk
    z = _dot3(a_ref[...], a2h_ref[...], a2l_ref[...]) + ab_ref[...]
    la = (jnp.minimum(z, 0.0) - jnp.log1p(jnp.exp(-jnp.abs(z)))) * (1.0 / GLA_TAU)
    q = qk_ref[:, 0:128] * (GLA_DK ** -0.5)
    k = qk_ref[:, 128:256]
    v = v_ref[...]
    if lvalid < tile:
        keep = lax.broadcasted_iota(I32, (tile, 128), 0) < lvalid
        la = jnp.where(keep, la, 0.0)
        k = jnp.where(keep, k, 0.0)
        v = jnp.where(lax.broadcasted_iota(I32, (tile, 256), 0) < lvalid, v, 0.0)
    b = _split3_rhs(tri_ref[...], la)
    tot = _split3_rhs(ones_ref[...], la)
    qe = q * jnp.exp(b)
    kinv = (k * jnp.exp(-b)).astype(BF16)
    kd = (k * jnp.exp(tot - b)).astype(BF16)
    vb = v.astype(BF16)
    hm128 = hm128_ref[...]
    hm256 = hm256_ref[...]
    qblk = jnp.concatenate([qe[c * chunk:(c + 1) * chunk] * hm128[h:h + 1]
                            for c in range(nch) for h in range(GLA_H)], axis=0).astype(BF16)
    att = lax.dot_general(qblk, kinv, _LANES, preferred_element_type=F32) * amask_ref[...]
    intra = jnp.dot(att.astype(BF16), vb, preferred_element_type=F32)
    qeb = qe.astype(BF16)
    st = st_ref[...]
    for c in range(nch):
        rows = slice(c * chunk, (c + 1) * chunk)
        o = lax.dot_general(qeb[rows], st.astype(BF16), _LANES, preferred_element_type=F32)
        for h in range(GLA_H):
            r0 = (c * GLA_H + h) * chunk
            o = o + intra[r0:r0 + chunk] * hm256[h:h + 1]
        o_ref[rows, :] = o
        upd = lax.dot_general(vb[rows], kd[rows], _ROWS, preferred_element_type=F32)
        st = st * jnp.exp(tot[c * chunk:c * chunk + 1]) + upd * bd_ref[...]
    st_ref[...] = st
    o = o_ref[...]
    ms = _split_dot(o * o, wavg_ref[...], 2)
    r = r_ref[...]
    o_ref[...] = o * lax.rsqrt(ms + EPS) * ng_ref[...] * (r * _sigmoid(r))


def _gla(P, st0, w, consts, batch, seq, tg, chunk, lvalid):
    nt = seq // tg

    def col(off, wd):
        return pl.BlockSpec((tg, wd), lambda b, j: (b * nt + j, off // wd))

    gm = consts["gla"][(tg, chunk)]
    cs = [w["gla_a2h"], w["gla_a2l"], w["gla_ab"], w["gla_ng"],
          gm["tri"], gm["ones"], gm["amask"], consts["hm128"], consts["hm8"], consts["bd"], consts["wavg"]]
    return pl.pallas_call(
        functools.partial(_gla_kernel, tile=tg, chunk=chunk, lvalid=lvalid),
        grid=(batch, nt),
        in_specs=[col(C_GV, 256), col(C_GR, 256), col(C_GQK, 256), col(C_GA, 128),
                  pl.BlockSpec((None, 256, 128), lambda b, j: (b, 0, 0))] + [_const_spec(a) for a in cs],
        out_specs=[pl.BlockSpec((tg, 256), lambda b, j: (b * nt + j, 0)),
                   pl.BlockSpec((None, 256, 128), lambda b, j: (b, 0, 0))],
        out_shape=[jax.ShapeDtypeStruct((batch * seq, 256), F32),
                   jax.ShapeDtypeStruct((batch, 256, 128), F32)],
        compiler_params=_cparams(2), name="gla",
    )(P, P, P, P, st0, *cs)


def _gelu_tanh(x):
    return 0.5 * x * (1.0 + jnp.tanh(math.sqrt(2.0 / math.pi) * (x + 0.044715 * (x * x * x))))


def _s5_kernel(u_ref, x0_ref, a_ref, bh_ref, c_ref, d_ref, gw_ref, gb_ref,
               o_ref, xf_ref, st_s, bur_s, bui_s, xr_s, xi_s, *, tile, last_row):
    @pl.when(pl.program_id(1) == 0)
    def _():
        st_s[...] = x0_ref[...]

    u = u_ref[...]
    bu = jnp.dot(u.astype(BF16), bh_ref[...], preferred_element_type=F32)
    bur_s[...] = bu[:, 0:S5_N]
    bui_s[...] = bu[:, S5_N:2 * S5_N]
    ar = a_ref[:, 0:S5_N]
    ai = a_ref[:, S5_N:2 * S5_N]

    def step(t, carry):
        xr, xi = carry
        row = pl.ds(t, 1)
        nr = ar * xr - ai * xi + bur_s[row, :]
        ni = ar * xi + ai * xr + bui_s[row, :]
        xr_s[row, :] = nr
        xi_s[row, :] = ni
        return nr, ni

    xr, xi = lax.fori_loop(0, tile, step, (st_s[:, 0:S5_N], st_s[:, S5_N:2 * S5_N]))
    st_s[:, 0:S5_N] = xr
    st_s[:, S5_N:2 * S5_N] = xi
    y = (jnp.dot(xr_s[...].astype(BF16), c_ref[0:S5_N, :], preferred_element_type=F32)
         + jnp.dot(xi_s[...].astype(BF16), c_ref[S5_N:2 * S5_N, :], preferred_element_type=F32)
         + d_ref[...] * u)
    z = _gelu_tanh(y)
    gate = jnp.dot(z.astype(BF16), gw_ref[...], preferred_element_type=F32) + gb_ref[...]
    o_ref[...] = z * _sigmoid(gate)
    xf_ref[:, 0:S5_N] = xr_s[last_row:last_row + 1, :]
    xf_ref[:, S5_N:2 * S5_N] = xi_s[last_row:last_row + 1, :]


def _s5(P, x0, w, batch, seq, tile, lvalid):
    nt = seq // tile
    last_row = (lvalid - 1) % tile
    cs = [w["s5_a"], w["s5_bh"], w["s5_c"], w["s5_d"], w["s5_gw"], w["s5_gb"]]
    return pl.pallas_call(
        functools.partial(_s5_kernel, tile=tile, last_row=last_row),
        grid=(batch, nt),
        in_specs=[pl.BlockSpec((tile, 256), lambda b, j: (b * nt + j, C_S5 // 256)),
                  pl.BlockSpec((None, 1, 2 * S5_N), lambda b, j: (b, 0, 0))] + [_const_spec(a) for a in cs],
        out_specs=[pl.BlockSpec((tile, 256), lambda b, j: (b * nt + j, 0)),
                   pl.BlockSpec((None, 1, 2 * S5_N), lambda b, j: (b, 0, 0))],
        out_shape=[jax.ShapeDtypeStruct((batch * seq, 256), F32),
                   jax.ShapeDtypeStruct((batch, 1, 2 * S5_N), F32)],
        scratch_shapes=[pltpu.VMEM((1, 2 * S5_N), F32)] + [pltpu.VMEM((tile, S5_N), F32)] * 4,
        compiler_params=_cparams(2), name="s5",
    )(P, x0, *cs)


RW_CH = 16
RW_SC = 4 * RW_CH


def _rwkv_chunked(sf_ref, r_s, k_s, v_s, al_s, be_s, lw_s, y_s, sbd_s, hm, tri, onesb, strict, incl, bd,
                  nb, tile):
    nh = RW_H

    def blk(x):
        return jnp.concatenate([x[RW_CH * c:RW_CH * (c + 1)] * hm[h:h + 1]
                                for c in range(4) for h in range(nh)], axis=0)

    def rep(x):
        return jnp.concatenate([x[RW_CH * c:RW_CH * (c + 1)] for c in range(4) for _ in range(nh)], axis=0)

    def stack_heads(x):
        return jnp.concatenate([x[RW_CH * c:RW_CH * (c + 1), RW_N * h:RW_N * (h + 1)]
                                for c in range(4) for h in range(nh)], axis=0)

    def mm(a, b):
        return jnp.dot(a, b, preferred_element_type=F32)

    for b in range(nb):
        sbd_s[b] = sf_ref[b]

    def superchunk(sc, carry):
        r0 = pl.multiple_of(sc * RW_SC, RW_SC)
        rows = pl.ds(r0, RW_SC)
        for b in range(nb):
            lw = lw_s[b, rows, :]
            cum = _split3_rhs(tri, lw)
            tot = _split3_rhs(onesb, lw)
            rr, kx, vv = r_s[b, rows, :], k_s[b, rows, :], v_s[b, rows, :]
            al, be = al_s[b, rows, :], be_s[b, rows, :]
            pinv = jnp.exp(-cum)
            pend = jnp.exp(tot - cum)
            ab = al * jnp.exp(cum - lw)
            rb = rr * jnp.exp(cum)
            bt, kt, bp, kp = be * pinv, kx * pinv, be * pend, kx * pend
            ablk = blk(ab)
            lhs = jnp.concatenate([ablk, blk(rb)], axis=0).astype(BF16)
            rhs = jnp.concatenate([rep(bt), rep(kt)], axis=0).astype(BF16)
            g = lax.dot_general(lhs, rhs, _LANES, preferred_element_type=F32)
            mb = g[0:256, 0:256] * strict
            mk = g[0:256, 256:512] * strict
            myb = (g[256:512, 0:256] * incl).astype(BF16)
            myk = (g[256:512, 256:512] * incl).astype(BF16)
            u, mp = mb, mb
            for _ in range(3):
                mpb = mp.astype(BF16)
                mp = mm(mpb, mpb)
                u = u + mp + mm(u.astype(BF16), mp.astype(BF16))
            ub = u.astype(BF16)
            ab1 = (ablk + mm(ub, ablk.astype(BF16))).astype(BF16)
            rblk = lhs[256:512]
            vst = stack_heads(vv).astype(BF16)
            w0 = mm(mk.astype(BF16), vst)
            z0 = w0 + mm(ub, w0.astype(BF16))
            y0 = mm(myk, vst)
            bpk = jnp.concatenate([blk(bp), blk(kp)], axis=1).astype(BF16)
            S = sbd_s[b]
            nr = nh * RW_CH
            for c in range(4):
                d0 = nr * c
                lc = jnp.concatenate([ab1[d0:d0 + nr], rblk[d0:d0 + nr]], axis=0)
                s_hi = S.astype(BF16)
                s_lo = (S - s_hi.astype(F32)).astype(BF16)
                x = (lax.dot_general(lc, s_hi, _LANES, preferred_element_type=F32)
                     + lax.dot_general(lc, s_lo, _LANES, preferred_element_type=F32))
                zst = x[0:nr] + z0[d0:d0 + nr]
                zb = zst.astype(BF16)
                yst = x[nr:2 * nr] + y0[d0:d0 + nr] + mm(myb[d0:d0 + nr, d0:d0 + nr], zb)
                y_s[b, pl.ds(r0 + RW_CH * c, RW_CH), :] = jnp.concatenate(
                    [yst[RW_CH * h:RW_CH * (h + 1)] for h in range(nh)], axis=1)
                upd = (lax.dot_general(zb, bpk[d0:d0 + nr, 0:256], _ROWS, preferred_element_type=F32)
                       + lax.dot_general(vst[d0:d0 + nr], bpk[d0:d0 + nr, 256:512], _ROWS,
                                         preferred_element_type=F32))
                S = S * jnp.exp(tot[RW_CH * c:RW_CH * c + 1]) + upd
            sbd_s[b] = S
        return carry

    lax.fori_loop(0, tile // RW_SC, superchunk, 0)
    for b in range(nb):
        sf_ref[b] = sbd_s[b]


def _rwkv_kernel(r_ref, k_ref, v_ref, lo_ref, s0_ref, prev_ref, mu_ref, w0_ref, a0_ref,
                 w2h_ref, w2l_ref, a2h_ref, a2l_ref, g2h_ref, g2l_ref, kkp_ref, ka_ref, rk_ref, ng_ref,
                 wones_ref, wavg_ref, idt_ref, hm_ref, tri_ref, ones_ref, strict_ref, incl_ref, bd_ref,
                 o_ref, sf_ref, r_s, k_s, v_s, kk_s, ka_s, w_s, y_s, g_s, bo_s, prev_s, sbd_s,
                 *, nb, tile, nsteps, chunked):
    @pl.when(pl.program_id(1) == 0)
    def _():
        sf_ref[...] = s0_ref[...]
        prev_s[...] = prev_ref[...]

    wones = wones_ref[...]
    wavg = wavg_ref[...]
    idt = idt_ref[...]
    row0_256 = lax.broadcasted_iota(I32, (tile, 256), 0) == 0
    row0_128 = lax.broadcasted_iota(I32, (tile, 128), 0) == 0

    def shift_mix(p, prev_row, mu, row0):
        sh = jnp.where(row0, prev_row, pltpu.roll(p, 1, 0))
        return p + (sh - p) * mu

    for b in range(nb):
        pr, pk, pv, plo = r_ref[b], k_ref[b], v_ref[b], lo_ref[b]
        r = shift_mix(pr, prev_s[b, :, 0:256], mu_ref[:, 0:256], row0_256)
        k = shift_mix(pk, prev_s[b, :, 256:512], mu_ref[:, 256:512], row0_256)
        v = shift_mix(pv, prev_s[b, :, 512:768], mu_ref[:, 512:768], row0_256)
        lo = shift_mix(plo, prev_s[b, :, 768:896], mu_ref[:, 768:896], row0_128)
        prev_s[b, :, 0:256] = pr[tile - 1:tile, :]
        prev_s[b, :, 256:512] = pk[tile - 1:tile, :]
        prev_s[b, :, 512:768] = pv[tile - 1:tile, :]
        prev_s[b, :, 768:896] = plo[tile - 1:tile, :]
        wl = w0_ref[...] + _dot3(jnp.tanh(lo), w2h_ref[...], w2l_ref[...])
        w = -_softplus(-wl) - 0.5
        a = _sigmoid(a0_ref[...] + _dot3(lo, a2h_ref[...], a2l_ref[...]))
        g = _dot3(_sigmoid(lo), g2h_ref[...], g2l_ref[...])
        kk = k * kkp_ref[...]
        kk = kk * lax.rsqrt(_split_dot(kk * kk, wones, 3) + EPS)
        k2 = k * (1.0 + (a - 1.0) * ka_ref[...])
        bonus = _split_dot(r * k2 * rk_ref[...], wones, 3) * v
        r_s[b] = r
        k_s[b] = k2
        v_s[b] = v
        kk_s[b] = -kk
        ka_s[b] = kk * a
        w_s[b] = -jnp.exp(w)
        if nsteps < tile:
            y_s[b] = jnp.zeros((tile, 256), F32)
        g_s[b] = g
        bo_s[b] = bonus

    def step(t, carry):
        for b in range(nb):
            row = pl.ds(t, 1)
            S = sf_ref[b]
            sa = _split_dot(S * kk_s[b, row, :], wones, 2)
            vcol = _split_dot(idt * v_s[b, row, :], wones, 2)
            Sn = S * jnp.exp(w_s[b, row, :]) + sa * ka_s[b, row, :] + vcol * k_s[b, row, :]
            yb = jnp.dot((Sn * r_s[b, row, :]).astype(BF16), wones, preferred_element_type=F32)
            y_s[b, row, :] = jnp.sum(yb * idt, axis=0, keepdims=True)
            sf_ref[b] = Sn
        return carry

    if chunked:
        _rwkv_chunked(sf_ref, r_s, k_s, v_s, kk_s, ka_s, w_s, y_s, sbd_s, hm_ref[...], tri_ref[...],
                      ones_ref[...], strict_ref[...], incl_ref[...], bd_ref[...], nb, tile)
    else:
        lax.fori_loop(0, nsteps, step, 0)
    for b in range(nb):
        y = y_s[b]
        mu = _split_dot(y, wavg, 3)
        yc = y - mu
        var = _split_dot(yc * yc, wavg, 3)
        o_ref[b] = (yc * lax.rsqrt(var + RW_GN_EPS) * ng_ref[...] + bo_s[b]) * g_s[b]


def _rwkv(P3, s0, prev, w, consts, nb, tile, lvalid):
    batch, seq, _ = P3.shape
    nt = seq // tile
    nsteps = tile if lvalid >= seq else lvalid
    cs = [w["rw_mu"], w["rw_w0"], w["rw_a0"], w["rw_w2h"], w["rw_w2l"], w["rw_a2h"], w["rw_a2l"],
          w["rw_g2h"], w["rw_g2l"], w["rw_kk"], w["rw_ka"], w["rw_rk"], w["rw_ng"],
          consts["wones"], consts["wavg"], consts["idt"], consts["hm8"], consts["rw_tri"], consts["rw_ones"],
          consts["rw_strict"], consts["rw_incl"], consts["bd256"]]
    chunked = nsteps == tile and tile % RW_SC == 0

    def col(off, wd):
        return pl.BlockSpec((nb, tile, wd), lambda g, j: (g, j, off // wd))

    big = lambda: pltpu.VMEM((nb, tile, 256), F32)
    return pl.pallas_call(
        functools.partial(_rwkv_kernel, nb=nb, tile=tile, nsteps=nsteps, chunked=chunked),
        grid=(batch // nb, nt),
        in_specs=[col(C_RW, 256), col(C_RW + 256, 256), col(C_RW + 512, 256), col(C_RWLO, 128),
                  pl.BlockSpec((nb, RW_N, 256), lambda g, j: (g, 0, 0)),
                  pl.BlockSpec((nb, 1, RW_COLS), lambda g, j: (g, 0, 0))] + [_const_spec(a) for a in cs],
        out_specs=[pl.BlockSpec((nb, tile, 256), lambda g, j: (g, j, 0)),
                   pl.BlockSpec((nb, RW_N, 256), lambda g, j: (g, 0, 0))],
        out_shape=[jax.ShapeDtypeStruct((batch, seq, 256), F32),
                   jax.ShapeDtypeStruct((batch, RW_N, 256), F32)],
        scratch_shapes=[big(), big(), big(), big(), big(), big(), big(), big(), big(),
                        pltpu.VMEM((nb, 1, RW_COLS), F32),
                        pltpu.VMEM((nb, RW_N, 256) if chunked else (1, 8, 128), F32)],
        compiler_params=_cparams(2), name="rwkv",
    )(P3, P3, P3, P3, s0, prev, *cs)


def _merge_kernel(x_ref, sc_ref, sh_ref, gt_ref, g_ref, og_ref, oa_ref, os_ref, or_ref,
                  wg_ref, wbr_ref, wo_ref, o_ref):
    x = x_ref[...]
    h = _norm_mod(x, g_ref[...], sc_ref[...], sh_ref[...]).astype(BF16)
    merged = None
    for b, oref in enumerate((og_ref, oa_ref, os_ref, or_ref)):
        gate = _sigmoid(jnp.dot(h, wg_ref[:, b * D_MODEL:(b + 1) * D_MODEL], preferred_element_type=F32))
        proj = jnp.dot(oref[...].astype(BF16), wbr_ref[b], preferred_element_type=F32)
        merged = gate * proj if merged is None else merged + gate * proj
    y = jnp.dot(merged.astype(BF16), wo_ref[...], preferred_element_type=F32)
    o_ref[...] = x + gt_ref[...] * y


def _merge(x, mod, g, outs, w, tm, seq):
    rows = x.shape[0]
    row256 = pl.BlockSpec((tm, 256), lambda i: (i, 0))
    return pl.pallas_call(
        _merge_kernel,
        grid=(rows // tm,),
        in_specs=[pl.BlockSpec((tm, D_MODEL), lambda i: (i, 0)),
                  _mod_spec(mod, 1, tm, seq), _mod_spec(mod, 0, tm, seq), _mod_spec(mod, 2, tm, seq),
                  _const_spec(g), row256, row256, row256, row256,
                  _const_spec(w["w_gates"]), _const_spec(w["w_br"]), _const_spec(w["w_o"])],
        out_specs=pl.BlockSpec((tm, D_MODEL), lambda i: (i, 0)),
        out_shape=jax.ShapeDtypeStruct((rows, D_MODEL), F32),
        compiler_params=_cparams(1), name="merge",
    )(x, mod, mod, mod, g, *outs, w["w_gates"], w["w_br"], w["w_o"])


def _ffn_kernel(x_ref, sc_ref, sh_ref, gt_ref, g_ref, w1_ref, w2_ref, o_ref):
    x = x_ref[...]
    h = _norm_mod(x, g_ref[...], sc_ref[...], sh_ref[...]).astype(BF16)
    acc = None
    for c in range(D_FF // D_MODEL):
        cs = slice(c * D_MODEL, (c + 1) * D_MODEL)
        u = jnp.maximum(jnp.dot(h, w1_ref[:, cs], preferred_element_type=F32), 0.0)
        d = jnp.dot((u * u).astype(BF16), w2_ref[cs, :], preferred_element_type=F32)
        acc = d if acc is None else acc + d
    o_ref[...] = x + gt_ref[...] * acc


def _ffn(x, mod, g, w, tm, seq):
    rows = x.shape[0]
    return pl.pallas_call(
        _ffn_kernel,
        grid=(rows // tm,),
        in_specs=[pl.BlockSpec((tm, D_MODEL), lambda i: (i, 0)),
                  _mod_spec(mod, 4, tm, seq), _mod_spec(mod, 3, tm, seq), _mod_spec(mod, 5, tm, seq),
                  _const_spec(g), _const_spec(w["w_ff1"]), _const_spec(w["w_ff2"])],
        out_specs=pl.BlockSpec((tm, D_MODEL), lambda i: (i, 0)),
        out_shape=jax.ShapeDtypeStruct((rows, D_MODEL), F32),
        compiler_params=_cparams(1), name="ffn",
    )(x, mod, mod, mod, g, w["w_ff1"], w["w_ff2"])


def _ds_scores_kernel(pt_ref, q8_ref, w8_ref, kcur_ref, *refs, npages):
    pages, o_ref = refs[:npages], refs[npages]
    q8 = q8_ref[...]
    q8b = q8.astype(BF16)
    w8 = w8_ref[...] * (IDX_D ** -0.5)
    for p in range(npages):
        s = jnp.dot(q8b, pages[p][...].astype(BF16), preferred_element_type=F32)
        o_ref[p:p + 1, :] = jnp.sum(w8 * jnp.maximum(s, 0.0), axis=0, keepdims=True)
    s_cur = jnp.sum(q8 * kcur_ref[...], axis=-1, keepdims=True)
    i_cur = jnp.sum(w8 * jnp.maximum(s_cur, 0.0), axis=0, keepdims=True)
    lane = lax.broadcasted_iota(I32, (8, 128), 1)
    rowi = lax.broadcasted_iota(I32, (8, 128), 0)
    o_ref[npages:npages + 8, :] = jnp.where((lane == 0) & (rowi == 0), i_cur, -jnp.inf)


def _ds_scores(page_table, q8, w8, kcur, cache_idx, layer):
    n, npages = page_table.shape
    page_specs = [pl.BlockSpec((None, None, IDX_D, PAGE_SIZE), lambda i, pt, p=p: (layer, pt[i, p], 0, 0))
                  for p in range(npages)]
    return pl.pallas_call(
        functools.partial(_ds_scores_kernel, npages=npages),
        grid_spec=pltpu.PrefetchScalarGridSpec(
            num_scalar_prefetch=1, grid=(n,),
            in_specs=[pl.BlockSpec((None, IDX_H, IDX_D), lambda i, pt: (i, 0, 0)),
                      pl.BlockSpec((None, IDX_H, 1), lambda i, pt: (i, 0, 0)),
                      pl.BlockSpec((None, 1, IDX_D), lambda i, pt: (i, 0, 0))] + page_specs,
            out_specs=pl.BlockSpec((None, npages + 8, 128), lambda i, pt: (i, 0, 0))),
        out_shape=jax.ShapeDtypeStruct((n, npages + 8, 128), F32),
        compiler_params=_cparams(1), name="ds_scores",
    )(page_table, q8, w8, kcur, *([cache_idx] * npages))


def _ds_select_kernel(s_ref, o_ref, key_ref, *, topk, nvalid):
    rows, cols = s_ref.shape
    col = lax.broadcasted_iota(I32, (rows, cols), 1)
    valid = col < nvalid
    key_ref[...] = _score_keys(s_ref[...], valid)
    o_ref[...] = jnp.where(valid, _topk_select(key_ref, topk, col), 0.0)


def _ds_select(scores, topk, nvalid):
    rows, cols = scores.shape
    return pl.pallas_call(
        functools.partial(_ds_select_kernel, topk=topk, nvalid=nvalid),
        grid=(1,),
        in_specs=[pl.BlockSpec((rows, cols), lambda i: (0, 0))],
        out_specs=pl.BlockSpec((rows, cols), lambda i: (0, 0)),
        out_shape=jax.ShapeDtypeStruct((rows, cols), F32),
        scratch_shapes=[pltpu.VMEM((rows, cols), I32)],
        compiler_params=_cparams(1), name="ds_select",
    )(scores)


def _ds_attn_kernel(pt_ref, q_ref, kcur_ref, vcur_ref, m_ref, hm_ref, *refs, npages):
    kp, vp, o_ref = refs[:npages], refs[npages:2 * npages], refs[2 * npages]
    hm = hm_ref[...]
    qf = q_ref[...] * hm
    qb = qf.astype(BF16)
    sc = HD ** -0.5
    lg_cur = jnp.sum(qf * kcur_ref[...], axis=-1, keepdims=True) * sc
    cur_sel = m_ref[npages:npages + 1, 0:1] > 0.0
    mx = lg_cur
    lgs = []
    for p in range(npages):
        lg = jnp.dot(qb, kp[p][...].astype(BF16), preferred_element_type=F32) * sc
        lg = jnp.where(m_ref[p:p + 1, :] > 0.0, lg, -jnp.inf)
        lgs.append(lg)
        mx = jnp.maximum(mx, jnp.max(lg, axis=-1, keepdims=True))
    pc = jnp.where(cur_sel, jnp.exp(lg_cur - mx), 0.0)
    l = pc
    acc = pc * vcur_ref[...]
    for p in range(npages):
        pe = jnp.exp(lgs[p] - mx)
        l = l + jnp.sum(pe, axis=-1, keepdims=True)
        acc = acc + lax.dot_general(pe.astype(BF16), vp[p][...].astype(BF16), _LANES,
                                    preferred_element_type=F32)
    o_ref[...] = jnp.sum((acc / l) * hm, axis=0, keepdims=True)


def _ds_attn(page_table, q, kcur, vcur, mask, cache_k, cache_v, layer, headmask):
    n, npages = page_table.shape
    pspec = lambda p: pl.BlockSpec((None, None, 256, PAGE_SIZE), lambda i, pt, p=p: (layer, pt[i, p], 0, 0))
    row = pl.BlockSpec((None, 1, 256), lambda i, pt: (i, 0, 0))
    return pl.pallas_call(
        functools.partial(_ds_attn_kernel, npages=npages),
        grid_spec=pltpu.PrefetchScalarGridSpec(
            num_scalar_prefetch=1, grid=(n,),
            in_specs=[row, row, row,
                      pl.BlockSpec((None, npages + 8, 128), lambda i, pt: (i, 0, 0)),
                      pl.BlockSpec((8, 256), lambda i, pt: (0, 0))]
                     + [pspec(p) for p in range(npages)] + [pspec(p) for p in range(npages)],
            out_specs=row),
        out_shape=jax.ShapeDtypeStruct((n, 1, 256), F32),
        compiler_params=_cparams(1), name="ds_attn",
    )(page_table, q, kcur, vcur, mask, headmask, *([cache_k] * npages), *([cache_v] * npages))


def _constants():
    lane256 = np.arange(256)
    head = lane256 // 64
    wones = (head[:, None] == head[None, :]).astype(np.float32)
    idt = (np.arange(64)[:, None] == (lane256 % 64)[None, :]).astype(np.float32)
    e2 = ((np.arange(128) // 32)[:, None] == head[None, :]).astype(np.float32)
    bd = (head[:, None] == (np.arange(128) // 32)[None, :]).astype(np.float32)
    hm8 = (np.arange(8)[:, None] == head[None, :]).astype(np.float32)
    hm128 = (np.arange(8)[:, None] == (np.arange(128) // 32)[None, :]).astype(np.float32)

    def chunk_masks(tile, chunk, heads):
        t = np.arange(tile)
        same = (t[:, None] // chunk) == (t[None, :] // chunk)
        rows = np.arange(heads * tile)
        amask = ((rows // (heads * chunk))[:, None] == (t // chunk)[None, :]) & \
                ((t % chunk)[None, :] <= (rows % chunk)[:, None])
        return dict(tri=jnp.asarray(same & (t[None, :] <= t[:, None]), BF16), ones=jnp.asarray(same, BF16),
                    amask=jnp.asarray(amask, F32))

    gla = {key: chunk_masks(key[0], key[1], GLA_H) for key in ((256, 16), (SAMPLE_PAD, SAMPLE_PAD))}
    rwm = chunk_masks(RW_SC, RW_CH, RW_H)
    i256 = np.arange(256)
    same16 = (i256[:, None] // RW_CH) == (i256[None, :] // RW_CH)
    strict = same16 & ((i256 % RW_CH)[None, :] < (i256 % RW_CH)[:, None])
    incl = same16 & ((i256 % RW_CH)[None, :] <= (i256 % RW_CH)[:, None])

    def left(width, group, half):
        return jnp.asarray(((np.arange(width) % group) < half).astype(np.float32)[None, :])

    return dict(wones=jnp.asarray(wones, BF16), wavg=jnp.asarray(wones / 64.0, BF16), idt=jnp.asarray(idt),
                e2=jnp.asarray(e2, BF16), bd=jnp.asarray(bd), hm8=jnp.asarray(hm8), hm128=jnp.asarray(hm128),
                gla=gla, rw_tri=rwm["tri"], rw_ones=rwm["ones"], rw_strict=jnp.asarray(strict, F32),
                rw_incl=jnp.asarray(incl, F32), bd256=jnp.asarray(wones, F32),
                left_q=left(256, HD, ROT // 2), left_i=left(256, IDX_D, IDX_ROT // 2),
                left_kw=left(128, 128, IDX_ROT // 2))


def _rope_tables(pos, periodic):
    pos = pos.astype(F32)[:, None]

    def build(width, group, rot, extra=None):
        half = rot // 2
        freq = ROPE_THETA ** (-jnp.arange(half, dtype=F32) * (2.0 / rot))
        ang = pos * freq
        cos, sin = jnp.cos(ang), jnp.sin(ang)
        n = pos.shape[0]
        ones = jnp.ones((n, group - rot), F32)
        zeros = jnp.zeros((n, group - rot), F32)
        cg = jnp.concatenate([cos, cos, ones], axis=1)
        sg = jnp.concatenate([-sin, sin, zeros], axis=1)
        reps = width // group
        c, s = jnp.tile(cg, (1, reps)), jnp.tile(sg, (1, reps))
        if extra is not None:
            c, s = extra(c, s)
        return c, s

    cq, sq = build(256, HD, ROT)
    ci, si = build(256, IDX_D, IDX_ROT)

    def kw_extra(c, s):
        lane = jnp.arange(128)
        scale = jnp.where((lane >= IDX_D) & (lane < IDX_D + IDX_H), IDX_H ** -0.5, 1.0)
        keep = (lane < IDX_D)
        return jnp.where(keep, c, scale[None, :]), jnp.where(keep, s, 0.0)

    ckw, skw = build(128, IDX_D, IDX_ROT, kw_extra)
    return dict(cq=cq, sq=sq, ci=ci, si=si, ckw=ckw, skw=skw, periodic=periodic)


def _blockdiag(blocks):
    g, r, c = blocks.shape
    eye = jnp.eye(g, dtype=blocks.dtype)
    return jnp.einsum('grc,gh->grhc', blocks, eye).reshape(g * r, g * c)


def _layer_weights(l, p):
    w_in = p["w_in"][l]
    z = lambda n: jnp.zeros((D_MODEL, n), F32)
    w_mix = jnp.concatenate([
        w_in[:, 256:512], w_in[:, 528:784], w_in[:, 0:256],
        w_in[:, 784:1808], w_in[:, 1848:2104], w_in[:, 2104:3000],
        w_in[:, 512:528], z(112), w_in[:, 1808:1848], z(88)], axis=1).astype(BF16)
    w = dict(w_mix=w_mix, w_gates=w_in[:, 3000:7096].astype(BF16),
             w_br=p["w_br"][l].astype(BF16), w_o=p["w_o"][l].astype(BF16),
             w_ff1=p["w_ff1"][l].astype(BF16), w_ff2=p["w_ff2"][l].astype(BF16),
             norm1_g=p["norm1_g"][l][None, :], norm2_g=p["norm2_g"][l][None, :])
    a2 = jnp.zeros((128, 128), F32).at[0:GLA_RANK].set(p["gla_a2"][l])
    w["gla_a2h"], w["gla_a2l"] = _hilo(a2)
    w["gla_ab"] = p["gla_ab"][l][None, :]
    w["gla_ng"] = jnp.tile(p["gla_ng"][l], GLA_H)[None, :]
    w["att_qg"] = jnp.tile(p["att_qg"][l], ATT_H)[None, :]
    w["att_kg"] = jnp.tile(p["att_kg"][l], ATT_H)[None, :]
    dt = jnp.exp(p["s5_log_dt"][l])[:, None]
    lr = jnp.minimum(p["s5_a_re"][l], -1e-4)
    li = p["s5_a_im"][l]
    mag = jnp.exp(lr * dt)
    abr, abi = mag * jnp.cos(li * dt), mag * jnp.sin(li * dt)
    den = lr * lr + li * li
    fr = ((abr - 1.0) * lr + abi * li) / den
    fi = (abi * lr - (abr - 1.0) * li) / den
    b_re, b_im = p["s5_b_re"][l], p["s5_b_im"][l]
    bbr = fr[..., None] * b_re - fi[..., None] * b_im
    bbi = fr[..., None] * b_im + fi[..., None] * b_re
    bmat = jnp.concatenate([_blockdiag(bbr.transpose(0, 2, 1)), _blockdiag(bbi.transpose(0, 2, 1))], axis=1)
    w["s5_bh"] = bmat.astype(BF16)
    w["s5_a"] = jnp.concatenate([abr.reshape(1, -1), abi.reshape(1, -1)], axis=1)
    w["s5_c"] = jnp.concatenate([_blockdiag(p["s5_c_re"][l].transpose(0, 2, 1)),
                                 -_blockdiag(p["s5_c_im"][l].transpose(0, 2, 1))], axis=0).astype(BF16)
    w["s5_d"] = p["s5_d"][l].reshape(1, -1)
    w["s5_gw"] = p["s5_glu_w"][l].astype(BF16)
    w["s5_gb"] = p["s5_glu_b"][l][None, :]
    w["rw_mu"] = p["rw_mu"][l][None, :]
    w["rw_w0"] = p["rw_w0"][l][None, :]
    w["rw_a0"] = p["rw_a0"][l][None, :]
    lo = jnp.zeros((128, 256), F32)
    w["rw_w2h"], w["rw_w2l"] = _hilo(lo.at[0:RW_WR].set(p["rw_w2"][l]))
    w["rw_a2h"], w["rw_a2l"] = _hilo(lo.at[RW_WR:RW_WR + RW_AR].set(p["rw_a2"][l]))
    w["rw_g2h"], w["rw_g2l"] = _hilo(lo.at[RW_WR + RW_AR:128].set(p["rw_g2"][l]))
    for nm in ("rw_kk", "rw_ka", "rw_rk", "rw_ng"):
        w[nm] = p[nm][l][None, :]
    return w


def _mix_and_ffn(x, mod, w, consts, P, o_att, st_gla0, st_s50, st_rw0, prev, batch, seq, lvalid, tm, tiles):
    o_gla, st_gla = _gla(P, st_gla0, w, consts, batch, seq, tiles["gla"], tiles["chunk"], lvalid)
    o_s5, st_s5 = _s5(P, st_s50, w, batch, seq, tiles["s5"], lvalid)
    o_rw, st_rw = _rwkv(P.reshape(batch, seq, NP_COLS), st_rw0, prev, w, consts, tiles["nb"], tiles["rw"], lvalid)
    x = _merge(x, mod, w["norm1_g"], (o_gla, o_att, o_s5, o_rw.reshape(batch * seq, 256)), w, tm, seq)
    x = _ffn(x, mod, w["norm2_g"], w, tm, seq)
    return x, st_gla, st_s5, st_rw


def _gla_state_out(st):
    n = st.shape[0]
    s = st.reshape(n, GLA_H, GLA_DV, GLA_H, GLA_DK)
    s = jnp.stack([s[:, h, :, h, :] for h in range(GLA_H)], axis=1)
    return s.transpose(0, 1, 3, 2)


def _gla_state_in(s):
    eye = jnp.eye(GLA_H, dtype=s.dtype)
    n = s.shape[0]
    return jnp.einsum('nhkv,hg->nhvgk', s, eye).reshape(n, GLA_H * GLA_DV, GLA_H * GLA_DK)


def _rw_state_out(st):
    n = st.shape[0]
    return st.reshape(n, RW_N, RW_H, RW_N).transpose(0, 2, 1, 3)


def _rw_state_in(s):
    n = s.shape[0]
    return s.transpose(0, 2, 1, 3).reshape(n, RW_N, RW_H * RW_N)


def _forward(x_prompt, x_sample, c_prompt, c_sample, cache_k, cache_v, cache_idx, state_gla,
             state_s5_re, state_s5_im, state_rwkv, state_shift, page_table, p):
    B, S, _ = x_prompt.shape
    N = x_sample.shape[0]
    depth = p["w_in"].shape[0]
    past = page_table.shape[1] * PAGE_SIZE
    consts = _constants()
    mod_all = _modulation(jnp.concatenate([c_prompt, c_sample], axis=0), p["ada_w"], p["ada_b"])
    tabs_p = _rope_tables(jnp.arange(S), True)
    tabs_s = _rope_tables(jnp.full((N * SAMPLE_PAD,), past), False)
    ck = cache_k.transpose(0, 1, 3, 4, 2).reshape(*cache_k.shape[:2], ATT_H * HD, PAGE_SIZE)
    cv = cache_v.transpose(0, 1, 3, 4, 2).reshape(*cache_v.shape[:2], ATT_H * HD, PAGE_SIZE)
    ci = cache_idx.transpose(0, 1, 3, 2)

    tm_p = min(512, S)
    tm_s = min(128, N * SAMPLE_PAD)
    tiles_p = dict(gla=min(256, S), chunk=16, s5=min(256, S), rw=min(128, S), nb=min(4, B))
    tiles_s = dict(gla=SAMPLE_PAD, chunk=SAMPLE_PAD, s5=SAMPLE_PAD, rw=SAMPLE_PAD, nb=8)
    tq = min(256, S)

    xp = x_prompt.reshape(B * S, D_MODEL)
    xs = jnp.pad(x_sample, ((0, 0), (0, SAMPLE_PAD - 1), (0, 0))).reshape(N * SAMPLE_PAD, D_MODEL)
    outs_p, outs_s = [], []
    for l in range(depth):
        w = _layer_weights(l, p)
        mod = mod_all[l, :B].reshape(B, 1, 6 * D_MODEL)
        P = _inproj(xp, mod, w["norm1_g"], w["w_mix"], tm_p, S)
        qn, kn, qir, kwr, qh, kh, vh, qih, kib = _dsa_prep(P, tabs_p, consts, w["att_qg"], w["att_kg"], tm_p, S)
        o_att = _dsa_attn(qh, kh, vh, qih, kwr, kib, B, S, tq)
        xp, st_gla, st_s5, st_rw = _mix_and_ffn(
            xp, mod, w, consts, P, o_att,
            jnp.zeros((B, 256, 128), F32), jnp.zeros((B, 1, 2 * S5_N), F32),
            jnp.zeros((B, RW_N, 256), F32), jnp.zeros((B, 1, RW_COLS), F32), B, S, S, tm_p, tiles_p)
        P3 = P.reshape(B, S, NP_COLS)
        outs_p.append((kn.reshape(B, S, ATT_H, HD), P3[:, :, C_DV:C_DV + 256].reshape(B, S, ATT_H, HD),
                       kwr.reshape(B, S, 128)[:, :, :IDX_D], _gla_state_out(st_gla),
                       st_s5[:, 0, :S5_N].reshape(B, S5_G, S5_P), st_s5[:, 0, S5_N:].reshape(B, S5_G, S5_P),
                       _rw_state_out(st_rw), P3[:, S - 1, C_RW:C_RW + RW_COLS]))
        mod = jnp.repeat(mod_all[l, B:], SAMPLE_PAD, axis=0)
        P = _inproj(xs, mod, w["norm1_g"], w["w_mix"], tm_s, SAMPLE_PAD)
        qn, kn, qir, kwr = _dsa_prep(P, tabs_s, consts, w["att_qg"], w["att_kg"], tm_s, SAMPLE_PAD)[:4]
        first = lambda a: a.reshape(N, SAMPLE_PAD, a.shape[-1])[:, 0]
        qn1, kn1, qir1, kwr1, P1 = first(qn), first(kn), first(qir), first(kwr), first(P)
        v1 = P1[:, C_DV:C_DV + 256]
        scores = _ds_scores(page_table, qir1.reshape(N, IDX_H, IDX_D),
                            kwr1[:, IDX_D:IDX_D + IDX_H].reshape(N, IDX_H, 1),
                            kwr1[:, :IDX_D].reshape(N, 1, IDX_D), ci, l)
        ncols = scores.shape[1] * 128
        sel = _ds_select(scores.reshape(N, ncols), min(TOPK_MAX, (past + 1) // 4), past + 1)
        o1 = _ds_attn(page_table, qn1.reshape(N, 1, 256), kn1.reshape(N, 1, 256), v1.reshape(N, 1, 256),
                      sel.reshape(N, ncols // 128, 128), ck, cv, l, consts["hm8"])
        o_att = jnp.pad(o1, ((0, 0), (0, SAMPLE_PAD - 1), (0, 0))).reshape(N * SAMPLE_PAD, 256)
        x0 = jnp.concatenate([state_s5_re[l].reshape(N, 1, S5_N), state_s5_im[l].reshape(N, 1, S5_N)], axis=2)
        xs, st_gla, st_s5, st_rw = _mix_and_ffn(
            xs, mod, w, consts, P, o_att, _gla_state_in(state_gla[l]), x0, _rw_state_in(state_rwkv[l]),
            state_shift[l].reshape(N, 1, RW_COLS), N, SAMPLE_PAD, 1, tm_s, tiles_s)
        outs_s.append((kn1.reshape(N, 1, ATT_H, HD), v1.reshape(N, 1, ATT_H, HD), kwr1[:, None, :IDX_D],
                       _gla_state_out(st_gla), st_s5[:, 0, :S5_N].reshape(N, S5_G, S5_P),
                       st_s5[:, 0, S5_N:].reshape(N, S5_G, S5_P), _rw_state_out(st_rw),
                       P1[:, C_RW:C_RW + RW_COLS]))
    yp = xp.reshape(B, S, D_MODEL)
    ys = xs.reshape(N, SAMPLE_PAD, D_MODEL)[:, 0:1]
    stack = lambda lst, i: jnp.stack([s[i] for s in lst])
    return (yp, ys) + tuple(stack(outs_p, i) for i in range(8)) + tuple(stack(outs_s, i) for i in range(8))


def kernel(x_prompt, x_sample, c_prompt, c_sample, cache_k, cache_v, cache_idx, state_gla, state_s5_re, state_s5_im, state_rwkv, state_shift, page_table, ada_w, ada_b, norm1_g, norm2_g, w_in, gla_a2, gla_ab, gla_ng, att_qg, att_kg, s5_a_re, s5_a_im, s5_log_dt, s5_b_re, s5_b_im, s5_c_re, s5_c_im, s5_d, s5_glu_w, s5_glu_b, rw_mu, rw_w0, rw_w2, rw_a0, rw_a2, rw_g2, rw_kk, rw_ka, rw_rk, rw_ng, w_br, w_o, w_ff1, w_ff2):
    p = dict(ada_w=ada_w, ada_b=ada_b, norm1_g=norm1_g, norm2_g=norm2_g, w_in=w_in, gla_a2=gla_a2,
             gla_ab=gla_ab, gla_ng=gla_ng, att_qg=att_qg, att_kg=att_kg, s5_a_re=s5_a_re, s5_a_im=s5_a_im,
             s5_log_dt=s5_log_dt, s5_b_re=s5_b_re, s5_b_im=s5_b_im, s5_c_re=s5_c_re, s5_c_im=s5_c_im,
             s5_d=s5_d, s5_glu_w=s5_glu_w, s5_glu_b=s5_glu_b, rw_mu=rw_mu, rw_w0=rw_w0, rw_w2=rw_w2,
             rw_a0=rw_a0, rw_a2=rw_a2, rw_g2=rw_g2, rw_kk=rw_kk, rw_ka=rw_ka, rw_rk=rw_rk, rw_ng=rw_ng,
             w_br=w_br, w_o=w_o, w_ff1=w_ff1, w_ff2=w_ff2)
    return _forward(x_prompt, x_sample, c_prompt, c_sample, cache_k, cache_v, cache_idx, state_gla,
                    state_s5_re, state_s5_im, state_rwkv, state_shift, page_table, p)
```

```python
import functools
import math

import numpy as np
import jax
import jax.numpy as jnp
from jax import lax
from jax.experimental import pallas as pl
from jax.experimental.pallas import tpu as pltpu

F32 = jnp.float32
BF16 = jnp.bfloat16
I32 = jnp.int32

D_MODEL = 1024
BR_W = 256
GLA_H, GLA_DK, GLA_DV, GLA_RANK, GLA_TAU = 4, 32, 64, 16, 16.0
ATT_H, HD, ROT = 4, 64, 16
IDX_H, IDX_D, IDX_ROT = 8, 32, 8
TOPK_MAX = 256
ROPE_THETA = 500000.0
S5_G, S5_P, S5_CH = 16, 64, 16
S5_N = S5_G * S5_P
RW_H, RW_N, RW_WR, RW_AR, RW_GR = 4, 64, 32, 32, 64
RW_COLS = 896
D_FF = 4096
EPS = 1e-6
RW_GN_EPS = 64e-5
PAGE_SIZE = 128
INT_MIN = -(2 ** 31)
SAMPLE_PAD = 8

C_GV, C_GR, C_GQK = 0, 256, 512
C_DQ, C_DK, C_DV, C_DQI = 768, 1024, 1280, 1536
C_S5 = 1792
C_RW = 2048
C_RWLO = 2816
C_GA = 2944
C_DKW = 3072
NP_COLS = 3200
VMEM_LIMIT = 56 * 1024 * 1024


def _cparams(n_axes):
    return pltpu.CompilerParams(dimension_semantics=("arbitrary",) * n_axes,
                                vmem_limit_bytes=VMEM_LIMIT)


def _split_dot(x, w, terms):
    acc = None
    r = x
    for i in range(terms):
        hi = r.astype(BF16)
        d = jnp.dot(hi, w, preferred_element_type=F32)
        acc = d if acc is None else acc + d
        if i + 1 < terms:
            r = r - hi.astype(F32)
    return acc


def _dot3(x, w_hi, w_lo):
    x_hi = x.astype(BF16)
    x_lo = (x - x_hi.astype(F32)).astype(BF16)
    return (jnp.dot(x_hi, w_hi, preferred_element_type=F32)
            + jnp.dot(x_hi, w_lo, preferred_element_type=F32)
            + jnp.dot(x_lo, w_hi, preferred_element_type=F32))


def _hilo(w):
    hi = w.astype(BF16)
    return hi, (w - hi.astype(F32)).astype(BF16)


def _sigmoid(x):
    return 1.0 / (1.0 + jnp.exp(-x))


def _softplus(x):
    return jnp.maximum(x, 0.0) + jnp.log1p(jnp.exp(-jnp.abs(x)))


def _norm_mod(x, g, sc, sh):
    ms = jnp.mean(x * x, axis=-1, keepdims=True)
    return (x * lax.rsqrt(ms + EPS) * g) * (1.0 + sc) + sh


def _mod_spec(mod, j, tm, seq):
    if mod.ndim == 3:
        return pl.BlockSpec((None, 1, D_MODEL), lambda i: ((i * tm) // seq, 0, j))
    return pl.BlockSpec((tm, D_MODEL), lambda i: (i, j))


def _const_spec(a):
    nd = a.ndim
    return pl.BlockSpec(a.shape, lambda *_: (0,) * nd)


def _mod_kernel(c_ref, w_ref, b_ref, o_ref):
    o_ref[...] = jnp.dot(c_ref[...], w_ref[...].astype(BF16), preferred_element_type=F32) + b_ref[...]


def _modulation(c_all, ada_w, ada_b):
    depth = ada_w.shape[0]
    rows = c_all.shape[0]
    tn = 1536
    return pl.pallas_call(
        _mod_kernel,
        grid=(depth, 6 * D_MODEL // tn),
        in_specs=[pl.BlockSpec((rows, D_MODEL), lambda l, j: (0, 0)),
                  pl.BlockSpec((None, D_MODEL, tn), lambda l, j: (l, 0, j)),
                  pl.BlockSpec((None, 1, tn), lambda l, j: (l, 0, j))],
        out_specs=pl.BlockSpec((None, rows, tn), lambda l, j: (l, 0, j)),
        out_shape=jax.ShapeDtypeStruct((depth, rows, 6 * D_MODEL), F32),
        compiler_params=_cparams(2), name="modulation",
    )(c_all.astype(BF16), ada_w, ada_b.reshape(depth, 1, 6 * D_MODEL))


def _inproj_kernel(x_ref, sc_ref, sh_ref, g_ref, w_ref, o_ref):
    h = _norm_mod(x_ref[...], g_ref[...], sc_ref[...], sh_ref[...])
    o_ref[...] = jnp.dot(h.astype(BF16), w_ref[...], preferred_element_type=F32)


def _inproj(x, mod, g, w_mix, tm, seq):
    rows = x.shape[0]
    return pl.pallas_call(
        _inproj_kernel,
        grid=(rows // tm,),
        in_specs=[pl.BlockSpec((tm, D_MODEL), lambda i: (i, 0)),
                  _mod_spec(mod, 1, tm, seq), _mod_spec(mod, 0, tm, seq),
                  _const_spec(g), _const_spec(w_mix)],
        out_specs=pl.BlockSpec((tm, NP_COLS), lambda i: (i, 0)),
        out_shape=jax.ShapeDtypeStruct((rows, NP_COLS), F32),
        compiler_params=_cparams(1), name="inproj",
    )(x, mod, mod, g, w_mix)


def _rope_apply(x, cos, sn, left, shift):
    n = x.shape[-1]
    rot = jnp.where(left > 0.0, pltpu.roll(x, n - shift, 1), pltpu.roll(x, shift, 1))
    return x * cos + rot * sn


def _dsa_prep_kernel(q_ref, k_ref, v_ref, qi_ref, kw_ref, cq_ref, sq_ref, ci_ref, si_ref, ckw_ref, skw_ref,
                     lq_ref, li_ref, lkw_ref, qg_ref, kg_ref, wavg_ref,
                     qn_ref, kn_ref, qir_ref, kwr_ref, qh_ref, kh_ref, vh_ref, qih_ref, kib_ref):
    wavg = wavg_ref[...]

    def headnorm(x, g):
        ms = _split_dot(x * x, wavg, 3)
        return x * lax.rsqrt(ms + EPS) * g

    cq, sq, lq = cq_ref[...], sq_ref[...], lq_ref[...]
    qn = _rope_apply(headnorm(q_ref[...], qg_ref[...]), cq, sq, lq, ROT // 2)
    kn = _rope_apply(headnorm(k_ref[...], kg_ref[...]), cq, sq, lq, ROT // 2)
    qir = _rope_apply(qi_ref[...], ci_ref[...], si_ref[...], li_ref[...], IDX_ROT // 2)
    kwr = _rope_apply(kw_ref[...], ckw_ref[...], skw_ref[...], lkw_ref[...], IDX_ROT // 2)
    qn_ref[...] = qn
    kn_ref[...] = kn
    qir_ref[...] = qir
    kwr_ref[...] = kwr
    v = v_ref[...]
    for h in range(ATT_H):
        hs = slice(h * HD, (h + 1) * HD)
        qh_ref[h] = qn[:, hs].astype(BF16)
        kh_ref[h] = kn[:, hs].astype(BF16)
        vh_ref[h] = v[:, hs].astype(BF16)
    for h in range(IDX_H):
        qih_ref[h] = qir[:, h * IDX_D:(h + 1) * IDX_D].astype(BF16)
    kib_ref[...] = kwr.astype(BF16)


def _dsa_prep(P, tabs, consts, qg, kg, tm, seq):
    rows = P.shape[0]
    nt = seq // tm if tabs["periodic"] else None

    def tab_spec(w):
        if tabs["periodic"]:
            return pl.BlockSpec((tm, w), lambda i: (i % nt, 0))
        return pl.BlockSpec((tm, w), lambda i: (i, 0))

    def col(off, w):
        return pl.BlockSpec((tm, w), lambda i: (i, off // w))

    out256 = jax.ShapeDtypeStruct((rows, 256), F32)
    heads = lambda n, d: (pl.BlockSpec((n, tm, d), lambda i: (0, i, 0)), jax.ShapeDtypeStruct((n, rows, d), BF16))
    hq, hi = heads(ATT_H, HD), heads(IDX_H, IDX_D)
    return pl.pallas_call(
        _dsa_prep_kernel,
        grid=(rows // tm,),
        in_specs=[col(C_DQ, 256), col(C_DK, 256), col(C_DV, 256), col(C_DQI, 256), col(C_DKW, 128),
                  tab_spec(256), tab_spec(256), tab_spec(256), tab_spec(256), tab_spec(128), tab_spec(128),
                  _const_spec(consts["left_q"]), _const_spec(consts["left_i"]), _const_spec(consts["left_kw"]),
                  _const_spec(qg), _const_spec(kg), _const_spec(consts["wavg"])],
        out_specs=[pl.BlockSpec((tm, 256), lambda i: (i, 0))] * 3 + [pl.BlockSpec((tm, 128), lambda i: (i, 0))]
                  + [hq[0], hq[0], hq[0], hi[0], pl.BlockSpec((tm, 128), lambda i: (i, 0))],
        out_shape=[out256, out256, out256, jax.ShapeDtypeStruct((rows, 128), F32),
                   hq[1], hq[1], hq[1], hi[1], jax.ShapeDtypeStruct((rows, 128), BF16)],
        compiler_params=_cparams(1), name="dsa_prep",
    )(P, P, P, P, P, tabs["cq"], tabs["sq"], tabs["ci"], tabs["si"], tabs["ckw"], tabs["skw"],
      consts["left_q"], consts["left_i"], consts["left_kw"], qg, kg, consts["wavg"])


def _score_keys(scores, valid):
    s = jnp.where(scores == 0.0, 0.0, scores)
    bits = pltpu.bitcast(s, I32)
    key = bits ^ (jnp.right_shift(bits, 31) & 0x7FFFFFFF)
    return jnp.where(valid, key, INT_MIN)


def _topk_select(key_ref, k, col):
    rows, cols = key_ref.shape
    kf = float(k)
    nbits = max(1, int(math.ceil(math.log2(cols))))

    ng = 4 if rows % 32 == 0 else 1
    rg = rows // ng

    def count_ge(g, c):
        return jnp.sum(jnp.where(key_ref[g * rg:(g + 1) * rg, :] >= c, 1.0, 0.0), axis=-1, keepdims=True)

    bases = tuple(jnp.where(count_ge(g, jnp.zeros((rg, 1), I32)) >= kf, 0, INT_MIN).astype(I32)
                  for g in range(ng))

    def bit_step(i, bases):
        bit = lax.shift_left(jnp.int32(1), 30 - i)
        return tuple(jnp.where(count_ge(g, b | bit) >= kf, b | bit, b) for g, b in enumerate(bases))

    bases = lax.fori_loop(0, 31, bit_step, bases, unroll=4)
    thr = bases[0] if ng == 1 else jnp.concatenate(bases, axis=0)
    key = key_ref[...]
    need = kf - jnp.sum(jnp.where(key > thr, 1.0, 0.0), axis=-1, keepdims=True)
    excess = jnp.sum(jnp.where(key == thr, 1.0, 0.0), axis=-1, keepdims=True) > need

    def pos_step(i, pos):
        cand = pos + lax.shift_left(jnp.int32(1), nbits - 1 - i)
        hit = jnp.where(key_ref[...] == thr, jnp.where(col < cand, 1.0, 0.0), 0.0)
        return jnp.where(jnp.sum(hit, axis=-1, keepdims=True) < need, cand, pos)

    n_iter = jnp.where(jnp.max(jnp.where(excess, 1, 0)) > 0, nbits, 0)
    pos = lax.fori_loop(0, n_iter, pos_step, jnp.zeros((rows, 1), I32))
    pos = jnp.where(excess, pos, cols)
    return jnp.where(key > thr, 1.0, jnp.where(key == thr, jnp.where(col <= pos, 1.0, 0.0), 0.0))


def _dsa_attn_kernel(qh_ref, qih_ref, kwq_ref, kh_ref, vh_ref, kib_ref, o_ref, key_ref, *, tq, topk, nvar):
    it = pl.program_id(1)
    t0 = it * tq
    seq = kh_ref.shape[1]
    step = seq // nvar
    per = step // tq

    def body(klen):
        ki = kib_ref[0:klen, 0:IDX_D]
        kwq = kwq_ref[...]
        scores = jnp.zeros((tq, klen), F32)
        for h in range(IDX_H):
            s = lax.dot_general(qih_ref[h], ki, _LANES, preferred_element_type=F32)
            w = kwq[:, IDX_D + h:IDX_D + h + 1] * (IDX_D ** -0.5)
            scores = scores + w * jnp.maximum(s, 0.0)
        col = lax.broadcasted_iota(I32, (tq, klen), 1)
        row = t0 + lax.broadcasted_iota(I32, (tq, klen), 0)
        causal = col <= row
        keys = key_ref.at[:, pl.ds(0, klen)]
        keys[...] = _score_keys(scores, causal)
        sel = jnp.where(causal, _topk_select(keys, topk, col), 0.0) > 0.0
        for h in range(ATT_H):
            lg = lax.dot_general(qh_ref[h], kh_ref[h, 0:klen, :], _LANES,
                                 preferred_element_type=F32) * (HD ** -0.5)
            lg = jnp.where(sel, lg, -jnp.inf)
            m = jnp.max(lg, axis=-1, keepdims=True)
            p = jnp.exp(lg - m)
            l = jnp.sum(p, axis=-1, keepdims=True)
            o = jnp.dot(p.astype(BF16), vh_ref[h, 0:klen, :], preferred_element_type=F32)
            o_ref[:, h * HD:(h + 1) * HD] = o / l

    for var in range(nvar):
        pl.when(it // per == var)(functools.partial(body, (var + 1) * step))


def _dsa_attn(qh, kh, vh, qih, kwr, kib, batch, seq, tq):
    topk = min(TOPK_MAX, seq // 4)
    nq = seq // tq
    nvar = min(4, nq)
    while seq // nvar < topk:
        nvar //= 2
    return pl.pallas_call(
        functools.partial(_dsa_attn_kernel, tq=tq, topk=topk, nvar=nvar),
        grid=(batch, nq),
        in_specs=[pl.BlockSpec((ATT_H, tq, HD), lambda b, i: (0, b * nq + i, 0)),
                  pl.BlockSpec((IDX_H, tq, IDX_D), lambda b, i: (0, b * nq + i, 0)),
                  pl.BlockSpec((tq, 128), lambda b, i: (b * nq + i, 0)),
                  pl.BlockSpec((ATT_H, seq, HD), lambda b, i: (0, b, 0)),
                  pl.BlockSpec((ATT_H, seq, HD), lambda b, i: (0, b, 0)),
                  pl.BlockSpec((seq, 128), lambda b, i: (b, 0))],
        out_specs=pl.BlockSpec((tq, 256), lambda b, i: (b * nq + i, 0)),
        out_shape=jax.ShapeDtypeStruct((batch * seq, 256), F32),
        scratch_shapes=[pltpu.VMEM((tq, seq), I32)],
        compiler_params=_cparams(2), name="dsa_attn",
    )(qh, qih, kwr, kh, vh, kib)


def _split3_rhs(m, x):
    hi = x.astype(BF16)
    r = x - hi.astype(F32)
    mid = r.astype(BF16)
    lo = (r - mid.astype(F32)).astype(BF16)
    return (jnp.dot(m, hi, preferred_element_type=F32) + jnp.dot(m, mid, preferred_element_type=F32)
            + jnp.dot(m, lo, preferred_element_type=F32))


_LANES = (((1,), (1,)), ((), ()))
_ROWS = (((0,), (0,)), ((), ()))


def _gla_kernel(v_ref, r_ref, qk_ref, a_ref, st0_ref, a2h_ref, a2l_ref, ab_ref, ng_ref,
                tri_ref, ones_ref, amask_ref, hm128_ref, hm256_ref, bd_ref, wavg_ref,
                o_ref, st_ref, *, tile, chunk, lvalid):
    @pl.when(pl.program_id(1) == 0)
    def _():
        st_ref[...] = st0_ref[...]

    nch = tile // chunk
    z = _dot3(a_ref[...], a2h_ref[...], a2l_ref[...]) + ab_ref[...]
    la = (jnp.minimum(z, 0.0) - jnp.log1p(jnp.exp(-jnp.abs(z)))) * (1.0 / GLA_TAU)
    q = qk_ref[:, 0:128] * (GLA_DK ** -0.5)
    k = qk_ref[:, 128:256]
    v = v_ref[...]
    if lvalid < tile:
        keep = lax.broadcasted_iota(I32, (tile, 128), 0) < lvalid
        la = jnp.where(keep, la, 0.0)
        k = jnp.where(keep, k, 0.0)
        v = jnp.where(lax.broadcasted_iota(I32, (tile, 256), 0) < lvalid, v, 0.0)
    b = _split3_rhs(tri_ref[...], la)
    tot = _split3_rhs(ones_ref[...], la)
    qe = q * jnp.exp(b)
    kinv = (k * jnp.exp(-b)).astype(BF16)
    kd = (k * jnp.exp(tot - b)).astype(BF16)
    vb = v.astype(BF16)
    hm128 = hm128_ref[...]
    hm256 = hm256_ref[...]
    qblk = jnp.concatenate([qe[c * chunk:(c + 1) * chunk] * hm128[h:h + 1]
                            for c in range(nch) for h in range(GLA_H)], axis=0).astype(BF16)
    att = lax.dot_general(qblk, kinv, _LANES, preferred_element_type=F32) * amask_ref[...]
    intra = jnp.dot(att.astype(BF16), vb, preferred_element_type=F32)
    qeb = qe.astype(BF16)
    st = st_ref[...]
    for c in range(nch):
        rows = slice(c * chunk, (c + 1) * chunk)
        o = lax.dot_general(qeb[rows], st.astype(BF16), _LANES, preferred_element_type=F32)
        for h in range(GLA_H):
            r0 = (c * GLA_H + h) * chunk
            o = o + intra[r0:r0 + chunk] * hm256[h:h + 1]
        o_ref[rows, :] = o
        upd = lax.dot_general(vb[rows], kd[rows], _ROWS, preferred_element_type=F32)
        st = st * jnp.exp(tot[c * chunk:c * chunk + 1]) + upd * bd_ref[...]
    st_ref[...] = st
    o = o_ref[...]
    ms = _split_dot(o * o, wavg_ref[...], 2)
    r = r_ref[...]
    o_ref[...] = o * lax.rsqrt(ms + EPS) * ng_ref[...] * (r * _sigmoid(r))


def _gla(P, st0, w, consts, batch, seq, tg, chunk, lvalid):
    nt = seq // tg

    def col(off, wd):
        return pl.BlockSpec((tg, wd), lambda b, j: (b * nt + j, off // wd))

    gm = consts["gla"][(tg, chunk)]
    cs = [w["gla_a2h"], w["gla_a2l"], w["gla_ab"], w["gla_ng"],
          gm["tri"], gm["ones"], gm["amask"], consts["hm128"], consts["hm8"], consts["bd"], consts["wavg"]]
    return pl.pallas_call(
        functools.partial(_gla_kernel, tile=tg, chunk=chunk, lvalid=lvalid),
        grid=(batch, nt),
        in_specs=[col(C_GV, 256), col(C_GR, 256), col(C_GQK, 256), col(C_GA, 128),
                  pl.BlockSpec((None, 256, 128), lambda b, j: (b, 0, 0))] + [_const_spec(a) for a in cs],
        out_specs=[pl.BlockSpec((tg, 256), lambda b, j: (b * nt + j, 0)),
                   pl.BlockSpec((None, 256, 128), lambda b, j: (b, 0, 0))],
        out_shape=[jax.ShapeDtypeStruct((batch * seq, 256), F32),
                   jax.ShapeDtypeStruct((batch, 256, 128), F32)],
        compiler_params=_cparams(2), name="gla",
    )(P, P, P, P, st0, *cs)


def _gelu_tanh(x):
    return 0.5 * x * (1.0 + jnp.tanh(math.sqrt(2.0 / math.pi) * (x + 0.044715 * (x * x * x))))


def _s5_kernel(u_ref, x0_ref, a_ref, bh_ref, c_ref, d_ref, gw_ref, gb_ref,
               o_ref, xf_ref, st_s, bur_s, bui_s, xr_s, xi_s, *, tile, last_row):
    @pl.when(pl.program_id(1) == 0)
    def _():
        st_s[...] = x0_ref[...]

    u = u_ref[...]
    bu = jnp.dot(u.astype(BF16), bh_ref[...], preferred_element_type=F32)
    bur_s[...] = bu[:, 0:S5_N]
    bui_s[...] = bu[:, S5_N:2 * S5_N]
    ar = a_ref[:, 0:S5_N]
    ai = a_ref[:, S5_N:2 * S5_N]

    def step(t, carry):
        xr, xi = carry
        row = pl.ds(t, 1)
        nr = ar * xr - ai * xi + bur_s[row, :]
        ni = ar * xi + ai * xr + bui_s[row, :]
        xr_s[row, :] = nr
        xi_s[row, :] = ni
        return nr, ni

    xr, xi = lax.fori_loop(0, tile, step, (st_s[:, 0:S5_N], st_s[:, S5_N:2 * S5_N]))
    st_s[:, 0:S5_N] = xr
    st_s[:, S5_N:2 * S5_N] = xi
    y = (jnp.dot(xr_s[...].astype(BF16), c_ref[0:S5_N, :], preferred_element_type=F32)
         + jnp.dot(xi_s[...].astype(BF16), c_ref[S5_N:2 * S5_N, :], preferred_element_type=F32)
         + d_ref[...] * u)
    z = _gelu_tanh(y)
    gate = jnp.dot(z.astype(BF16), gw_ref[...], preferred_element_type=F32) + gb_ref[...]
    o_ref[...] = z * _sigmoid(gate)
    xf_ref[:, 0:S5_N] = xr_s[last_row:last_row + 1, :]
    xf_ref[:, S5_N:2 * S5_N] = xi_s[last_row:last_row + 1, :]


def _s5(P, x0, w, batch, seq, tile, lvalid):
    nt = seq // tile
    last_row = (lvalid - 1) % tile
    cs = [w["s5_a"], w["s5_bh"], w["s5_c"], w["s5_d"], w["s5_gw"], w["s5_gb"]]
    return pl.pallas_call(
        functools.partial(_s5_kernel, tile=tile, last_row=last_row),
        grid=(batch, nt),
        in_specs=[pl.BlockSpec((tile, 256), lambda b, j: (b * nt + j, C_S5 // 256)),
                  pl.BlockSpec((None, 1, 2 * S5_N), lambda b, j: (b, 0, 0))] + [_const_spec(a) for a in cs],
        out_specs=[pl.BlockSpec((tile, 256), lambda b, j: (b * nt + j, 0)),
                   pl.BlockSpec((None, 1, 2 * S5_N), lambda b, j: (b, 0, 0))],
        out_shape=[jax.ShapeDtypeStruct((batch * seq, 256), F32),
                   jax.ShapeDtypeStruct((batch, 1, 2 * S5_N), F32)],
        scratch_shapes=[pltpu.VMEM((1, 2 * S5_N), F32)] + [pltpu.VMEM((tile, S5_N), F32)] * 4,
        compiler_params=_cparams(2), name="s5",
    )(P, x0, *cs)


def _s5_step_kernel(u_ref, x0_ref, a_ref, bh_ref, c_ref, d_ref, gw_ref, gb_ref, o_ref, xf_ref):
    u = u_ref[...]
    bu = jnp.dot(u.astype(BF16), bh_ref[...], preferred_element_type=F32)
    ar, ai = a_ref[:, 0:S5_N], a_ref[:, S5_N:2 * S5_N]
    xr0, xi0 = x0_ref[:, 0:S5_N], x0_ref[:, S5_N:2 * S5_N]
    xr = ar * xr0 - ai * xi0 + bu[:, 0:S5_N]
    xi = ar * xi0 + ai * xr0 + bu[:, S5_N:2 * S5_N]
    y = (jnp.dot(xr.astype(BF16), c_ref[0:S5_N, :], preferred_element_type=F32)
         + jnp.dot(xi.astype(BF16), c_ref[S5_N:2 * S5_N, :], preferred_element_type=F32)
         + d_ref[...] * u)
    z = _gelu_tanh(y)
    gate = jnp.dot(z.astype(BF16), gw_ref[...], preferred_element_type=F32) + gb_ref[...]
    o_ref[...] = z * _sigmoid(gate)
    xf_ref[:, 0:S5_N] = xr
    xf_ref[:, S5_N:2 * S5_N] = xi


def _s5_step(P1, x0, w):
    n = P1.shape[0]
    cs = [w["s5_a"], w["s5_bh"], w["s5_c"], w["s5_d"], w["s5_gw"], w["s5_gb"]]
    return pl.pallas_call(
        _s5_step_kernel,
        grid=(1,),
        in_specs=[pl.BlockSpec((n, 256), lambda i: (0, C_S5 // 256)),
                  pl.BlockSpec((n, 2 * S5_N), lambda i: (0, 0))] + [_const_spec(a) for a in cs],
        out_specs=[pl.BlockSpec((n, 256), lambda i: (0, 0)), pl.BlockSpec((n, 2 * S5_N), lambda i: (0, 0))],
        out_shape=[jax.ShapeDtypeStruct((n, 256), F32), jax.ShapeDtypeStruct((n, 2 * S5_N), F32)],
        compiler_params=_cparams(1), name="s5_step",
    )(P1, x0, *cs)


RW_CH = 16
RW_SC = 4 * RW_CH


def _rwkv_chunked(sf_ref, r_s, k_s, v_s, al_s, be_s, lw_s, y_s, sbd_s, hm, tri, onesb, strict, incl, bd,
                  nb, tile):
    nh = RW_H

    def blk(x):
        return jnp.concatenate([x[RW_CH * c:RW_CH * (c + 1)] * hm[h:h + 1]
                                for c in range(4) for h in range(nh)], axis=0)

    def rep(x):
        return jnp.concatenate([x[RW_CH * c:RW_CH * (c + 1)] for c in range(4) for _ in range(nh)], axis=0)

    def stack_heads(x):
        return jnp.concatenate([x[RW_CH * c:RW_CH * (c + 1), RW_N * h:RW_N * (h + 1)]
                                for c in range(4) for h in range(nh)], axis=0)

    def mm(a, b):
        return jnp.dot(a, b, preferred_element_type=F32)

    for b in range(nb):
        sbd_s[b] = sf_ref[b]

    def superchunk(sc, carry):
        r0 = pl.multiple_of(sc * RW_SC, RW_SC)
        rows = pl.ds(r0, RW_SC)
        for b in range(nb):
            lw = lw_s[b, rows, :]
            cum = _split3_rhs(tri, lw)
            tot = _split3_rhs(onesb, lw)
            rr, kx, vv = r_s[b, rows, :], k_s[b, rows, :], v_s[b, rows, :]
            al, be = al_s[b, rows, :], be_s[b, rows, :]
            pinv = jnp.exp(-cum)
            pend = jnp.exp(tot - cum)
            ab = al * jnp.exp(cum - lw)
            rb = rr * jnp.exp(cum)
            bt, kt, bp, kp = be * pinv, kx * pinv, be * pend, kx * pend
            ablk = blk(ab)
            lhs = jnp.concatenate([ablk, blk(rb)], axis=0).astype(BF16)
            rhs = jnp.concatenate([rep(bt), rep(kt)], axis=0).astype(BF16)
            g = lax.dot_general(lhs, rhs, _LANES, preferred_element_type=F32)
            mb = g[0:256, 0:256] * strict
            mk = g[0:256, 256:512] * strict
            myb = (g[256:512, 0:256] * incl).astype(BF16)
            myk = (g[256:512, 256:512] * incl).astype(BF16)
            u, mp = mb, mb
            for _ in range(3):
                mpb = mp.astype(BF16)
                mp = mm(mpb, mpb)
                u = u + mp + mm(u.astype(BF16), mp.astype(BF16))
            ub = u.astype(BF16)
            ab1 = (ablk + mm(ub, ablk.astype(BF16))).astype(BF16)
            rblk = lhs[256:512]
            vst = stack_heads(vv).astype(BF16)
            w0 = mm(mk.astype(BF16), vst)
            z0 = w0 + mm(ub, w0.astype(BF16))
            y0 = mm(myk, vst)
            bpk = jnp.concatenate([blk(bp), blk(kp)], axis=1).astype(BF16)
            S = sbd_s[b]
            nr = nh * RW_CH
            for c in range(4):
                d0 = nr * c
                lc = jnp.concatenate([ab1[d0:d0 + nr], rblk[d0:d0 + nr]], axis=0)
                s_hi = S.astype(BF16)
                s_lo = (S - s_hi.astype(F32)).astype(BF16)
                x = (lax.dot_general(lc, s_hi, _LANES, preferred_element_type=F32)
                     + lax.dot_general(lc, s_lo, _LANES, preferred_element_type=F32))
                zst = x[0:nr] + z0[d0:d0 + nr]
                zb = zst.astype(BF16)
                yst = x[nr:2 * nr] + y0[d0:d0 + nr] + mm(myb[d0:d0 + nr, d0:d0 + nr], zb)
                y_s[b, pl.ds(r0 + RW_CH * c, RW_CH), :] = jnp.concatenate(
                    [yst[RW_CH * h:RW_CH * (h + 1)] for h in range(nh)], axis=1)
                upd = (lax.dot_general(zb, bpk[d0:d0 + nr, 0:256], _ROWS, preferred_element_type=F32)
                       + lax.dot_general(vst[d0:d0 + nr], bpk[d0:d0 + nr, 256:512], _ROWS,
                                         preferred_element_type=F32))
                S = S * jnp.exp(tot[RW_CH * c:RW_CH * c + 1]) + upd
            sbd_s[b] = S
        return carry

    lax.fori_loop(0, tile // RW_SC, superchunk, 0)
    for b in range(nb):
        sf_ref[b] = sbd_s[b]


def _rwkv_kernel(r_ref, k_ref, v_ref, lo_ref, s0_ref, prev_ref, mu_ref, w0_ref, a0_ref,
                 w2h_ref, w2l_ref, a2h_ref, a2l_ref, g2h_ref, g2l_ref, kkp_ref, ka_ref, rk_ref, ng_ref,
                 wones_ref, wavg_ref, idt_ref, hm_ref, tri_ref, ones_ref, strict_ref, incl_ref, bd_ref,
                 o_ref, sf_ref, r_s, k_s, v_s, kk_s, ka_s, w_s, y_s, g_s, bo_s, prev_s, sbd_s,
                 *, nb, tile, nsteps, chunked):
    @pl.when(pl.program_id(1) == 0)
    def _():
        sf_ref[...] = s0_ref[...]
        prev_s[...] = prev_ref[...]

    wones = wones_ref[...]
    wavg = wavg_ref[...]
    idt = idt_ref[...]
    row0_256 = lax.broadcasted_iota(I32, (tile, 256), 0) == 0
    row0_128 = lax.broadcasted_iota(I32, (tile, 128), 0) == 0

    def shift_mix(p, prev_row, mu, row0):
        sh = jnp.where(row0, prev_row, pltpu.roll(p, 1, 0))
        return p + (sh - p) * mu

    for b in range(nb):
        pr, pk, pv, plo = r_ref[b], k_ref[b], v_ref[b], lo_ref[b]
        r = shift_mix(pr, prev_s[b, :, 0:256], mu_ref[:, 0:256], row0_256)
        k = shift_mix(pk, prev_s[b, :, 256:512], mu_ref[:, 256:512], row0_256)
        v = shift_mix(pv, prev_s[b, :, 512:768], mu_ref[:, 512:768], row0_256)
        lo = shift_mix(plo, prev_s[b, :, 768:896], mu_ref[:, 768:896], row0_128)
        prev_s[b, :, 0:256] = pr[tile - 1:tile, :]
        prev_s[b, :, 256:512] = pk[tile - 1:tile, :]
        prev_s[b, :, 512:768] = pv[tile - 1:tile, :]
        prev_s[b, :, 768:896] = plo[tile - 1:tile, :]
        wl = w0_ref[...] + _dot3(jnp.tanh(lo), w2h_ref[...], w2l_ref[...])
        w = -_softplus(-wl) - 0.5
        a = _sigmoid(a0_ref[...] + _dot3(lo, a2h_ref[...], a2l_ref[...]))
        g = _dot3(_sigmoid(lo), g2h_ref[...], g2l_ref[...])
        kk = k * kkp_ref[...]
        kk = kk * lax.rsqrt(_split_dot(kk * kk, wones, 3) + EPS)
        k2 = k * (1.0 + (a - 1.0) * ka_ref[...])
        bonus = _split_dot(r * k2 * rk_ref[...], wones, 3) * v
        r_s[b] = r
        k_s[b] = k2
        v_s[b] = v
        kk_s[b] = -kk
        ka_s[b] = kk * a
        w_s[b] = -jnp.exp(w)
        if nsteps < tile:
            y_s[b] = jnp.zeros((tile, 256), F32)
        g_s[b] = g
        bo_s[b] = bonus

    def step(t, carry):
        for b in range(nb):
            row = pl.ds(t, 1)
            S = sf_ref[b]
            sa = _split_dot(S * kk_s[b, row, :], wones, 2)
            vcol = _split_dot(idt * v_s[b, row, :], wones, 2)
            Sn = S * jnp.exp(w_s[b, row, :]) + sa * ka_s[b, row, :] + vcol * k_s[b, row, :]
            yb = jnp.dot((Sn * r_s[b, row, :]).astype(BF16), wones, preferred_element_type=F32)
            y_s[b, row, :] = jnp.sum(yb * idt, axis=0, keepdims=True)
            sf_ref[b] = Sn
        return carry

    if chunked:
        _rwkv_chunked(sf_ref, r_s, k_s, v_s, kk_s, ka_s, w_s, y_s, sbd_s, hm_ref[...], tri_ref[...],
                      ones_ref[...], strict_ref[...], incl_ref[...], bd_ref[...], nb, tile)
    else:
        lax.fori_loop(0, nsteps, step, 0)
    for b in range(nb):
        y = y_s[b]
        mu = _split_dot(y, wavg, 3)
        yc = y - mu
        var = _split_dot(yc * yc, wavg, 3)
        o_ref[b] = (yc * lax.rsqrt(var + RW_GN_EPS) * ng_ref[...] + bo_s[b]) * g_s[b]


def _rwkv(P3, s0, prev, w, consts, nb, tile, lvalid):
    batch, seq, _ = P3.shape
    nt = seq // tile
    nsteps = tile if lvalid >= seq else lvalid
    cs = [w["rw_mu"], w["rw_w0"], w["rw_a0"], w["rw_w2h"], w["rw_w2l"], w["rw_a2h"], w["rw_a2l"],
          w["rw_g2h"], w["rw_g2l"], w["rw_kk"], w["rw_ka"], w["rw_rk"], w["rw_ng"],
          consts["wones"], consts["wavg"], consts["idt"], consts["hm8"], consts["rw_tri"], consts["rw_ones"],
          consts["rw_strict"], consts["rw_incl"], consts["bd256"]]
    chunked = nsteps == tile and tile % RW_SC == 0

    def col(off, wd):
        return pl.BlockSpec((nb, tile, wd), lambda g, j: (g, j, off // wd))

    big = lambda: pltpu.VMEM((nb, tile, 256), F32)
    return pl.pallas_call(
        functools.partial(_rwkv_kernel, nb=nb, tile=tile, nsteps=nsteps, chunked=chunked),
        grid=(batch // nb, nt),
        in_specs=[col(C_RW, 256), col(C_RW + 256, 256), col(C_RW + 512, 256), col(C_RWLO, 128),
                  pl.BlockSpec((nb, RW_N, 256), lambda g, j: (g, 0, 0)),
                  pl.BlockSpec((nb, 1, RW_COLS), lambda g, j: (g, 0, 0))] + [_const_spec(a) for a in cs],
        out_specs=[pl.BlockSpec((nb, tile, 256), lambda g, j: (g, j, 0)),
                   pl.BlockSpec((nb, RW_N, 256), lambda g, j: (g, 0, 0))],
        out_shape=[jax.ShapeDtypeStruct((batch, seq, 256), F32),
                   jax.ShapeDtypeStruct((batch, RW_N, 256), F32)],
        scratch_shapes=[big(), big(), big(), big(), big(), big(), big(), big(), big(),
                        pltpu.VMEM((nb, 1, RW_COLS), F32),
                        pltpu.VMEM((nb, RW_N, 256) if chunked else (1, 8, 128), F32)],
        compiler_params=_cparams(2), name="rwkv",
    )(P3, P3, P3, P3, s0, prev, *cs)


def _merge_kernel(x_ref, sc_ref, sh_ref, gt_ref, g_ref, og_ref, oa_ref, os_ref, or_ref,
                  wg_ref, wbr_ref, wo_ref, o_ref):
    x = x_ref[...]
    h = _norm_mod(x, g_ref[...], sc_ref[...], sh_ref[...]).astype(BF16)
    merged = None
    for b, oref in enumerate((og_ref, oa_ref, os_ref, or_ref)):
        gate = _sigmoid(jnp.dot(h, wg_ref[:, b * D_MODEL:(b + 1) * D_MODEL], preferred_element_type=F32))
        proj = jnp.dot(oref[...].astype(BF16), wbr_ref[b], preferred_element_type=F32)
        merged = gate * proj if merged is None else merged + gate * proj
    y = jnp.dot(merged.astype(BF16), wo_ref[...], preferred_element_type=F32)
    o_ref[...] = x + gt_ref[...] * y


def _merge(x, mod, g, outs, w, tm, seq):
    rows = x.shape[0]
    row256 = pl.BlockSpec((tm, 256), lambda i: (i, 0))
    return pl.pallas_call(
        _merge_kernel,
        grid=(rows // tm,),
        in_specs=[pl.BlockSpec((tm, D_MODEL), lambda i: (i, 0)),
                  _mod_spec(mod, 1, tm, seq), _mod_spec(mod, 0, tm, seq), _mod_spec(mod, 2, tm, seq),
                  _const_spec(g), row256, row256, row256, row256,
                  _const_spec(w["w_gates"]), _const_spec(w["w_br"]), _const_spec(w["w_o"])],
        out_specs=pl.BlockSpec((tm, D_MODEL), lambda i: (i, 0)),
        out_shape=jax.ShapeDtypeStruct((rows, D_MODEL), F32),
        compiler_params=_cparams(1), name="merge",
    )(x, mod, mod, mod, g, *outs, w["w_gates"], w["w_br"], w["w_o"])


def _ffn_kernel(x_ref, sc_ref, sh_ref, gt_ref, g_ref, w1_ref, w2_ref, o_ref):
    x = x_ref[...]
    h = _norm_mod(x, g_ref[...], sc_ref[...], sh_ref[...]).astype(BF16)
    acc = None
    for c in range(D_FF // D_MODEL):
        cs = slice(c * D_MODEL, (c + 1) * D_MODEL)
        u = jnp.maximum(jnp.dot(h, w1_ref[:, cs], preferred_element_type=F32), 0.0)
        d = jnp.dot((u * u).astype(BF16), w2_ref[cs, :], preferred_element_type=F32)
        acc = d if acc is None else acc + d
    o_ref[...] = x + gt_ref[...] * acc


def _ffn(x, mod, g, w, tm, seq):
    rows = x.shape[0]
    return pl.pallas_call(
        _ffn_kernel,
        grid=(rows // tm,),
        in_specs=[pl.BlockSpec((tm, D_MODEL), lambda i: (i, 0)),
                  _mod_spec(mod, 4, tm, seq), _mod_spec(mod, 3, tm, seq), _mod_spec(mod, 5, tm, seq),
                  _const_spec(g), _const_spec(w["w_ff1"]), _const_spec(w["w_ff2"])],
        out_specs=pl.BlockSpec((tm, D_MODEL), lambda i: (i, 0)),
        out_shape=jax.ShapeDtypeStruct((rows, D_MODEL), F32),
        compiler_params=_cparams(1), name="ffn",
    )(x, mod, mod, mod, g, w["w_ff1"], w["w_ff2"])


def _ds_scores_kernel(pt_ref, q8_ref, w8_ref, kcur_ref, *refs, npages):
    pages, o_ref = refs[:npages], refs[npages]
    q8 = q8_ref[...]
    q8b = q8.astype(BF16)
    w8 = w8_ref[...] * (IDX_D ** -0.5)
    for p in range(npages):
        s = jnp.dot(q8b, pages[p][...].astype(BF16), preferred_element_type=F32)
        o_ref[p:p + 1, :] = jnp.sum(w8 * jnp.maximum(s, 0.0), axis=0, keepdims=True)
    s_cur = jnp.sum(q8 * kcur_ref[...], axis=-1, keepdims=True)
    i_cur = jnp.sum(w8 * jnp.maximum(s_cur, 0.0), axis=0, keepdims=True)
    lane = lax.broadcasted_iota(I32, (8, 128), 1)
    rowi = lax.broadcasted_iota(I32, (8, 128), 0)
    o_ref[npages:npages + 8, :] = jnp.where((lane == 0) & (rowi == 0), i_cur, -jnp.inf)


def _ds_scores(page_table, q8, w8, kcur, cache_idx, layer):
    n, npages = page_table.shape
    page_specs = [pl.BlockSpec((None, None, IDX_D, PAGE_SIZE), lambda i, pt, p=p: (layer, pt[i, p], 0, 0))
                  for p in range(npages)]
    return pl.pallas_call(
        functools.partial(_ds_scores_kernel, npages=npages),
        grid_spec=pltpu.PrefetchScalarGridSpec(
            num_scalar_prefetch=1, grid=(n,),
            in_specs=[pl.BlockSpec((None, IDX_H, IDX_D), lambda i, pt: (i, 0, 0)),
                      pl.BlockSpec((None, IDX_H, 1), lambda i, pt: (i, 0, 0)),
                      pl.BlockSpec((None, 1, IDX_D), lambda i, pt: (i, 0, 0))] + page_specs,
            out_specs=pl.BlockSpec((None, npages + 8, 128), lambda i, pt: (i, 0, 0))),
        out_shape=jax.ShapeDtypeStruct((n, npages + 8, 128), F32),
        compiler_params=_cparams(1), name="ds_scores",
    )(page_table, q8, w8, kcur, *([cache_idx] * npages))


def _ds_select_kernel(s_ref, o_ref, key_ref, *, topk, nvalid):
    rows, cols = s_ref.shape
    col = lax.broadcasted_iota(I32, (rows, cols), 1)
    valid = col < nvalid
    key_ref[...] = _score_keys(s_ref[...], valid)
    o_ref[...] = jnp.where(valid, _topk_select(key_ref, topk, col), 0.0)


def _ds_select(scores, topk, nvalid):
    rows, cols = scores.shape
    return pl.pallas_call(
        functools.partial(_ds_select_kernel, topk=topk, nvalid=nvalid),
        grid=(1,),
        in_specs=[pl.BlockSpec((rows, cols), lambda i: (0, 0))],
        out_specs=pl.BlockSpec((rows, cols), lambda i: (0, 0)),
        out_shape=jax.ShapeDtypeStruct((rows, cols), F32),
        scratch_shapes=[pltpu.VMEM((rows, cols), I32)],
        compiler_params=_cparams(1), name="ds_select",
    )(scores)


def _ds_attn_kernel(pt_ref, q_ref, kcur_ref, vcur_ref, m_ref, hm_ref, *refs, npages):
    kp, vp, o_ref = refs[:npages], refs[npages:2 * npages], refs[2 * npages]
    hm = hm_ref[...]
    qf = q_ref[...] * hm
    qb = qf.astype(BF16)
    sc = HD ** -0.5
    lg_cur = jnp.sum(qf * kcur_ref[...], axis=-1, keepdims=True) * sc
    cur_sel = m_ref[npages:npages + 1, 0:1] > 0.0
    mx = lg_cur
    lgs = []
    for p in range(npages):
        lg = jnp.dot(qb, kp[p][...].astype(BF16), preferred_element_type=F32) * sc
        lg = jnp.where(m_ref[p:p + 1, :] > 0.0, lg, -jnp.inf)
        lgs.append(lg)
        mx = jnp.maximum(mx, jnp.max(lg, axis=-1, keepdims=True))
    pc = jnp.where(cur_sel, jnp.exp(lg_cur - mx), 0.0)
    l = pc
    acc = pc * vcur_ref[...]
    for p in range(npages):
        pe = jnp.exp(lgs[p] - mx)
        l = l + jnp.sum(pe, axis=-1, keepdims=True)
        acc = acc + lax.dot_general(pe.astype(BF16), vp[p][...].astype(BF16), _LANES,
                                    preferred_element_type=F32)
    o_ref[...] = jnp.sum((acc / l) * hm, axis=0, keepdims=True)


def _ds_attn(page_table, q, kcur, vcur, mask, cache_k, cache_v, layer, headmask):
    n, npages = page_table.shape
    pspec = lambda p: pl.BlockSpec((None, None, 256, PAGE_SIZE), lambda i, pt, p=p: (layer, pt[i, p], 0, 0))
    row = pl.BlockSpec((None, 1, 256), lambda i, pt: (i, 0, 0))
    return pl.pallas_call(
        functools.partial(_ds_attn_kernel, npages=npages),
        grid_spec=pltpu.PrefetchScalarGridSpec(
            num_scalar_prefetch=1, grid=(n,),
            in_specs=[row, row, row,
                      pl.BlockSpec((None, npages + 8, 128), lambda i, pt: (i, 0, 0)),
                      pl.BlockSpec((8, 256), lambda i, pt: (0, 0))]
                     + [pspec(p) for p in range(npages)] + [pspec(p) for p in range(npages)],
            out_specs=row),
        out_shape=jax.ShapeDtypeStruct((n, 1, 256), F32),
        compiler_params=_cparams(1), name="ds_attn",
    )(page_table, q, kcur, vcur, mask, headmask, *([cache_k] * npages), *([cache_v] * npages))


def _constants():
    lane256 = np.arange(256)
    head = lane256 // 64
    wones = (head[:, None] == head[None, :]).astype(np.float32)
    idt = (np.arange(64)[:, None] == (lane256 % 64)[None, :]).astype(np.float32)
    e2 = ((np.arange(128) // 32)[:, None] == head[None, :]).astype(np.float32)
    bd = (head[:, None] == (np.arange(128) // 32)[None, :]).astype(np.float32)
    hm8 = (np.arange(8)[:, None] == head[None, :]).astype(np.float32)
    hm128 = (np.arange(8)[:, None] == (np.arange(128) // 32)[None, :]).astype(np.float32)

    def chunk_masks(tile, chunk, heads):
        t = np.arange(tile)
        same = (t[:, None] // chunk) == (t[None, :] // chunk)
        rows = np.arange(heads * tile)
        amask = ((rows // (heads * chunk))[:, None] == (t // chunk)[None, :]) & \
                ((t % chunk)[None, :] <= (rows % chunk)[:, None])
        return dict(tri=jnp.asarray(same & (t[None, :] <= t[:, None]), BF16), ones=jnp.asarray(same, BF16),
                    amask=jnp.asarray(amask, F32))

    gla = {key: chunk_masks(key[0], key[1], GLA_H) for key in ((256, 16), (SAMPLE_PAD, SAMPLE_PAD))}
    rwm = chunk_masks(RW_SC, RW_CH, RW_H)
    i256 = np.arange(256)
    same16 = (i256[:, None] // RW_CH) == (i256[None, :] // RW_CH)
    strict = same16 & ((i256 % RW_CH)[None, :] < (i256 % RW_CH)[:, None])
    incl = same16 & ((i256 % RW_CH)[None, :] <= (i256 % RW_CH)[:, None])

    def left(width, group, half):
        return jnp.asarray(((np.arange(width) % group) < half).astype(np.float32)[None, :])

    return dict(wones=jnp.asarray(wones, BF16), wavg=jnp.asarray(wones / 64.0, BF16), idt=jnp.asarray(idt),
                e2=jnp.asarray(e2, BF16), bd=jnp.asarray(bd), hm8=jnp.asarray(hm8), hm128=jnp.asarray(hm128),
                gla=gla, rw_tri=rwm["tri"], rw_ones=rwm["ones"], rw_strict=jnp.asarray(strict, F32),
                rw_incl=jnp.asarray(incl, F32), bd256=jnp.asarray(wones, F32),
                left_q=left(256, HD, ROT // 2), left_i=left(256, IDX_D, IDX_ROT // 2),
                left_kw=left(128, 128, IDX_ROT // 2))


def _rope_tables(pos, periodic):
    pos = pos.astype(F32)[:, None]

    def build(width, group, rot, extra=None):
        half = rot // 2
        freq = ROPE_THETA ** (-jnp.arange(half, dtype=F32) * (2.0 / rot))
        ang = pos * freq
        cos, sin = jnp.cos(ang), jnp.sin(ang)
        n = pos.shape[0]
        ones = jnp.ones((n, group - rot), F32)
        zeros = jnp.zeros((n, group - rot), F32)
        cg = jnp.concatenate([cos, cos, ones], axis=1)
        sg = jnp.concatenate([-sin, sin, zeros], axis=1)
        reps = width // group
        c, s = jnp.tile(cg, (1, reps)), jnp.tile(sg, (1, reps))
        if extra is not None:
            c, s = extra(c, s)
        return c, s

    cq, sq = build(256, HD, ROT)
    ci, si = build(256, IDX_D, IDX_ROT)

    def kw_extra(c, s):
        lane = jnp.arange(128)
        scale = jnp.where((lane >= IDX_D) & (lane < IDX_D + IDX_H), IDX_H ** -0.5, 1.0)
        keep = (lane < IDX_D)
        return jnp.where(keep, c, scale[None, :]), jnp.where(keep, s, 0.0)

    ckw, skw = build(128, IDX_D, IDX_ROT, kw_extra)
    return dict(cq=cq, sq=sq, ci=ci, si=si, ckw=ckw, skw=skw, periodic=periodic)


def _blockdiag(blocks):
    g, r, c = blocks.shape
    eye = jnp.eye(g, dtype=blocks.dtype)
    return jnp.einsum('grc,gh->grhc', blocks, eye).reshape(g * r, g * c)


def _layer_weights(l, p):
    w_in = p["w_in"][l]
    z = lambda n: jnp.zeros((D_MODEL, n), F32)
    w_mix = jnp.concatenate([
        w_in[:, 256:512], w_in[:, 528:784], w_in[:, 0:256],
        w_in[:, 784:1808], w_in[:, 1848:2104], w_in[:, 2104:3000],
        w_in[:, 512:528], z(112), w_in[:, 1808:1848], z(88)], axis=1).astype(BF16)
    w = dict(w_mix=w_mix, w_gates=w_in[:, 3000:7096].astype(BF16),
             w_br=p["w_br"][l].astype(BF16), w_o=p["w_o"][l].astype(BF16),
             w_ff1=p["w_ff1"][l].astype(BF16), w_ff2=p["w_ff2"][l].astype(BF16),
             norm1_g=p["norm1_g"][l][None, :], norm2_g=p["norm2_g"][l][None, :])
    a2 = jnp.zeros((128, 128), F32).at[0:GLA_RANK].set(p["gla_a2"][l])
    w["gla_a2h"], w["gla_a2l"] = _hilo(a2)
    w["gla_ab"] = p["gla_ab"][l][None, :]
    w["gla_ng"] = jnp.tile(p["gla_ng"][l], GLA_H)[None, :]
    w["att_qg"] = jnp.tile(p["att_qg"][l], ATT_H)[None, :]
    w["att_kg"] = jnp.tile(p["att_kg"][l], ATT_H)[None, :]
    dt = jnp.exp(p["s5_log_dt"][l])[:, None]
    lr = jnp.minimum(p["s5_a_re"][l], -1e-4)
    li = p["s5_a_im"][l]
    mag = jnp.exp(lr * dt)
    abr, abi = mag * jnp.cos(li * dt), mag * jnp.sin(li * dt)
    den = lr * lr + li * li
    fr = ((abr - 1.0) * lr + abi * li) / den
    fi = (abi * lr - (abr - 1.0) * li) / den
    b_re, b_im = p["s5_b_re"][l], p["s5_b_im"][l]
    bbr = fr[..., None] * b_re - fi[..., None] * b_im
    bbi = fr[..., None] * b_im + fi[..., None] * b_re
    bmat = jnp.concatenate([_blockdiag(bbr.transpose(0, 2, 1)), _blockdiag(bbi.transpose(0, 2, 1))], axis=1)
    w["s5_bh"] = bmat.astype(BF16)
    w["s5_a"] = jnp.concatenate([abr.reshape(1, -1), abi.reshape(1, -1)], axis=1)
    w["s5_c"] = jnp.concatenate([_blockdiag(p["s5_c_re"][l].transpose(0, 2, 1)),
                                 -_blockdiag(p["s5_c_im"][l].transpose(0, 2, 1))], axis=0).astype(BF16)
    w["s5_d"] = p["s5_d"][l].reshape(1, -1)
    w["s5_gw"] = p["s5_glu_w"][l].astype(BF16)
    w["s5_gb"] = p["s5_glu_b"][l][None, :]
    w["rw_mu"] = p["rw_mu"][l][None, :]
    w["rw_w0"] = p["rw_w0"][l][None, :]
    w["rw_a0"] = p["rw_a0"][l][None, :]
    lo = jnp.zeros((128, 256), F32)
    w["rw_w2h"], w["rw_w2l"] = _hilo(lo.at[0:RW_WR].set(p["rw_w2"][l]))
    w["rw_a2h"], w["rw_a2l"] = _hilo(lo.at[RW_WR:RW_WR + RW_AR].set(p["rw_a2"][l]))
    w["rw_g2h"], w["rw_g2l"] = _hilo(lo.at[RW_WR + RW_AR:128].set(p["rw_g2"][l]))
    for nm in ("rw_kk", "rw_ka", "rw_rk", "rw_ng"):
        w[nm] = p[nm][l][None, :]
    return w


def _mix_and_ffn(x, mod, w, consts, P, o_att, st_gla0, st_s50, st_rw0, prev, batch, seq, lvalid, tm, tiles):
    o_gla, st_gla = _gla(P, st_gla0, w, consts, batch, seq, tiles["gla"], tiles["chunk"], lvalid)
    if lvalid == 1:
        o1, st1 = _s5_step(P.reshape(batch, seq, NP_COLS)[:, 0], st_s50[:, 0], w)
        o_s5 = jnp.pad(o1[:, None], ((0, 0), (0, seq - 1), (0, 0))).reshape(batch * seq, 256)
        st_s5 = st1[:, None]
    else:
        o_s5, st_s5 = _s5(P, st_s50, w, batch, seq, tiles["s5"], lvalid)
    o_rw, st_rw = _rwkv(P.reshape(batch, seq, NP_COLS), st_rw0, prev, w, consts, tiles["nb"], tiles["rw"], lvalid)
    x = _merge(x, mod, w["norm1_g"], (o_gla, o_att, o_s5, o_rw.reshape(batch * seq, 256)), w, tm, seq)
    x = _ffn(x, mod, w["norm2_g"], w, tm, seq)
    return x, st_gla, st_s5, st_rw


def _gla_state_out(st):
    n = st.shape[0]
    s = st.reshape(n, GLA_H, GLA_DV, GLA_H, GLA_DK)
    s = jnp.stack([s[:, h, :, h, :] for h in range(GLA_H)], axis=1)
    return s.transpose(0, 1, 3, 2)


def _gla_state_in(s):
    eye = jnp.eye(GLA_H, dtype=s.dtype)
    n = s.shape[0]
    return jnp.einsum('nhkv,hg->nhvgk', s, eye).reshape(n, GLA_H * GLA_DV, GLA_H * GLA_DK)


def _rw_state_out(st):
    n = st.shape[0]
    return st.reshape(n, RW_N, RW_H, RW_N).transpose(0, 2, 1, 3)


def _rw_state_in(s):
    n = s.shape[0]
    return s.transpose(0, 2, 1, 3).reshape(n, RW_N, RW_H * RW_N)


def _forward(x_prompt, x_sample, c_prompt, c_sample, cache_k, cache_v, cache_idx, state_gla,
             state_s5_re, state_s5_im, state_rwkv, state_shift, page_table, p):
    B, S, _ = x_prompt.shape
    N = x_sample.shape[0]
    depth = p["w_in"].shape[0]
    past = page_table.shape[1] * PAGE_SIZE
    consts = _constants()
    mod_all = _modulation(jnp.concatenate([c_prompt, c_sample], axis=0), p["ada_w"], p["ada_b"])
    tabs_p = _rope_tables(jnp.arange(S), True)
    tabs_s = _rope_tables(jnp.full((N * SAMPLE_PAD,), past), False)
    ck = cache_k.transpose(0, 1, 3, 4, 2).reshape(*cache_k.shape[:2], ATT_H * HD, PAGE_SIZE)
    cv = cache_v.transpose(0, 1, 3, 4, 2).reshape(*cache_v.shape[:2], ATT_H * HD, PAGE_SIZE)
    ci = cache_idx.transpose(0, 1, 3, 2)

    tm_p = min(512, S)
    tm_s = min(128, N * SAMPLE_PAD)
    tiles_p = dict(gla=min(256, S), chunk=16, s5=min(256, S), rw=min(128, S), nb=min(4, B))
    tiles_s = dict(gla=SAMPLE_PAD, chunk=SAMPLE_PAD, s5=SAMPLE_PAD, rw=SAMPLE_PAD, nb=8)
    tq = min(128, S)

    xp = x_prompt.reshape(B * S, D_MODEL)
    xs = jnp.pad(x_sample, ((0, 0), (0, SAMPLE_PAD - 1), (0, 0))).reshape(N * SAMPLE_PAD, D_MODEL)
    outs_p, outs_s = [], []
    for l in range(depth):
        w = _layer_weights(l, p)
        mod = mod_all[l, :B].reshape(B, 1, 6 * D_MODEL)
        P = _inproj(xp, mod, w["norm1_g"], w["w_mix"], tm_p, S)
        qn, kn, qir, kwr, qh, kh, vh, qih, kib = _dsa_prep(P, tabs_p, consts, w["att_qg"], w["att_kg"], tm_p, S)
        o_att = _dsa_attn(qh, kh, vh, qih, kwr, kib, B, S, tq)
        xp, st_gla, st_s5, st_rw = _mix_and_ffn(
            xp, mod, w, consts, P, o_att,
            jnp.zeros((B, 256, 128), F32), jnp.zeros((B, 1, 2 * S5_N), F32),
            jnp.zeros((B, RW_N, 256), F32), jnp.zeros((B, 1, RW_COLS), F32), B, S, S, tm_p, tiles_p)
        P3 = P.reshape(B, S, NP_COLS)
        outs_p.append((kn.reshape(B, S, ATT_H, HD), P3[:, :, C_DV:C_DV + 256].reshape(B, S, ATT_H, HD),
                       kwr.reshape(B, S, 128)[:, :, :IDX_D], _gla_state_out(st_gla),
                       st_s5[:, 0, :S5_N].reshape(B, S5_G, S5_P), st_s5[:, 0, S5_N:].reshape(B, S5_G, S5_P),
                       _rw_state_out(st_rw), P3[:, S - 1, C_RW:C_RW + RW_COLS]))
        mod = jnp.repeat(mod_all[l, B:], SAMPLE_PAD, axis=0)
        P = _inproj(xs, mod, w["norm1_g"], w["w_mix"], tm_s, SAMPLE_PAD)
        qn, kn, qir, kwr = _dsa_prep(P, tabs_s, consts, w["att_qg"], w["att_kg"], tm_s, SAMPLE_PAD)[:4]
        first = lambda a: a.reshape(N, SAMPLE_PAD, a.shape[-1])[:, 0]
        qn1, kn1, qir1, kwr1, P1 = first(qn), first(kn), first(qir), first(kwr), first(P)
        v1 = P1[:, C_DV:C_DV + 256]
        scores = _ds_scores(page_table, qir1.reshape(N, IDX_H, IDX_D),
                            kwr1[:, IDX_D:IDX_D + IDX_H].reshape(N, IDX_H, 1),
                            kwr1[:, :IDX_D].reshape(N, 1, IDX_D), ci, l)
        ncols = scores.shape[1] * 128
        sel = _ds_select(scores.reshape(N, ncols), min(TOPK_MAX, (past + 1) // 4), past + 1)
        o1 = _ds_attn(page_table, qn1.reshape(N, 1, 256), kn1.reshape(N, 1, 256), v1.reshape(N, 1, 256),
                      sel.reshape(N, ncols // 128, 128), ck, cv, l, consts["hm8"])
        o_att = jnp.pad(o1, ((0, 0), (0, SAMPLE_PAD - 1), (0, 0))).reshape(N * SAMPLE_PAD, 256)
        x0 = jnp.concatenate([state_s5_re[l].reshape(N, 1, S5_N), state_s5_im[l].reshape(N, 1, S5_N)], axis=2)
        xs, st_gla, st_s5, st_rw = _mix_and_ffn(
            xs, mod, w, consts, P, o_att, _gla_state_in(state_gla[l]), x0, _rw_state_in(state_rwkv[l]),
            state_shift[l].reshape(N, 1, RW_COLS), N, SAMPLE_PAD, 1, tm_s, tiles_s)
        outs_s.append((kn1.reshape(N, 1, ATT_H, HD), v1.reshape(N, 1, ATT_H, HD), kwr1[:, None, :IDX_D],
                       _gla_state_out(st_gla), st_s5[:, 0, :S5_N].reshape(N, S5_G, S5_P),
                       st_s5[:, 0, S5_N:].reshape(N, S5_G, S5_P), _rw_state_out(st_rw),
                       P1[:, C_RW:C_RW + RW_COLS]))
    yp = xp.reshape(B, S, D_MODEL)
    ys = xs.reshape(N, SAMPLE_PAD, D_MODEL)[:, 0:1]
    stack = lambda lst, i: jnp.stack([s[i] for s in lst])
    return (yp, ys) + tuple(stack(outs_p, i) for i in range(8)) + tuple(stack(outs_s, i) for i in range(8))


def kernel(x_prompt, x_sample, c_prompt, c_sample, cache_k, cache_v, cache_idx, state_gla, state_s5_re, state_s5_im, state_rwkv, state_shift, page_table, ada_w, ada_b, norm1_g, norm2_g, w_in, gla_a2, gla_ab, gla_ng, att_qg, att_kg, s5_a_re, s5_a_im, s5_log_dt, s5_b_re, s5_b_im, s5_c_re, s5_c_im, s5_d, s5_glu_w, s5_glu_b, rw_mu, rw_w0, rw_w2, rw_a0, rw_a2, rw_g2, rw_kk, rw_ka, rw_rk, rw_ng, w_br, w_o, w_ff1, w_ff2):
    p = dict(ada_w=ada_w, ada_b=ada_b, norm1_g=norm1_g, norm2_g=norm2_g, w_in=w_in, gla_a2=gla_a2,
             gla_ab=gla_ab, gla_ng=gla_ng, att_qg=att_qg, att_kg=att_kg, s5_a_re=s5_a_re, s5_a_im=s5_a_im,
             s5_log_dt=s5_log_dt, s5_b_re=s5_b_re, s5_b_im=s5_b_im, s5_c_re=s5_c_re, s5_c_im=s5_c_im,
             s5_d=s5_d, s5_glu_w=s5_glu_w, s5_glu_b=s5_glu_b, rw_mu=rw_mu, rw_w0=rw_w0, rw_w2=rw_w2,
             rw_a0=rw_a0, rw_a2=rw_a2, rw_g2=rw_g2, rw_kk=rw_kk, rw_ka=rw_ka, rw_rk=rw_rk, rw_ng=rw_ng,
             w_br=w_br, w_o=w_o, w_ff1=w_ff1, w_ff2=w_ff2)
    return _forward(x_prompt, x_sample, c_prompt, c_sample, cache_k, cache_v, cache_idx, state_gla,
                    state_s5_re, state_s5_im, state_rwkv, state_shift, page_table, p)
```

```python
import functools
import math

import numpy as np
import jax
import jax.numpy as jnp
from jax import lax
from jax.experimental import pallas as pl
from jax.experimental.pallas import tpu as pltpu

F32 = jnp.float32
BF16 = jnp.bfloat16
I32 = jnp.int32

D_MODEL = 1024
BR_W = 256
GLA_H, GLA_DK, GLA_DV, GLA_RANK, GLA_TAU = 4, 32, 64, 16, 16.0
ATT_H, HD, ROT = 4, 64, 16
IDX_H, IDX_D, IDX_ROT = 8, 32, 8
TOPK_MAX = 256
ROPE_THETA = 500000.0
S5_G, S5_P, S5_CH = 16, 64, 16
S5_N = S5_G * S5_P
RW_H, RW_N, RW_WR, RW_AR, RW_GR = 4, 64, 32, 32, 64
RW_COLS = 896
D_FF = 4096
EPS = 1e-6
RW_GN_EPS = 64e-5
PAGE_SIZE = 128
INT_MIN = -(2 ** 31)
SAMPLE_PAD = 8

C_GV, C_GR, C_GQK = 0, 256, 512
C_DQ, C_DK, C_DV, C_DQI = 768, 1024, 1280, 1536
C_S5 = 1792
C_RW = 2048
C_RWLO = 2816
C_GA = 2944
C_DKW = 3072
NP_COLS = 3200
VMEM_LIMIT = 56 * 1024 * 1024


def _cparams(n_axes):
    return pltpu.CompilerParams(dimension_semantics=("arbitrary",) * n_axes,
                                vmem_limit_bytes=VMEM_LIMIT)


def _split_dot(x, w, terms):
    acc = None
    r = x
    for i in range(terms):
        hi = r.astype(BF16)
        d = jnp.dot(hi, w, preferred_element_type=F32)
        acc = d if acc is None else acc + d
        if i + 1 < terms:
            r = r - hi.astype(F32)
    return acc


def _dot3(x, w_hi, w_lo):
    x_hi = x.astype(BF16)
    x_lo = (x - x_hi.astype(F32)).astype(BF16)
    return (jnp.dot(x_hi, w_hi, preferred_element_type=F32)
            + jnp.dot(x_hi, w_lo, preferred_element_type=F32)
            + jnp.dot(x_lo, w_hi, preferred_element_type=F32))


def _hilo(w):
    hi = w.astype(BF16)
    return hi, (w - hi.astype(F32)).astype(BF16)


def _sigmoid(x):
    return 1.0 / (1.0 + jnp.exp(-x))


def _softplus(x):
    return jnp.maximum(x, 0.0) + jnp.log1p(jnp.exp(-jnp.abs(x)))


def _norm_mod(x, g, sc, sh):
    ms = jnp.mean(x * x, axis=-1, keepdims=True)
    return (x * lax.rsqrt(ms + EPS) * g) * (1.0 + sc) + sh


def _mod_spec(mod, j, tm, seq):
    if mod.ndim == 3:
        return pl.BlockSpec((None, 1, D_MODEL), lambda i: ((i * tm) // seq, 0, j))
    return pl.BlockSpec((tm, D_MODEL), lambda i: (i, j))


def _const_spec(a):
    nd = a.ndim
    return pl.BlockSpec(a.shape, lambda *_: (0,) * nd)


def _mod_kernel(c_ref, w_ref, b_ref, o_ref):
    o_ref[...] = jnp.dot(c_ref[...], w_ref[...].astype(BF16), preferred_element_type=F32) + b_ref[...]


def _modulation(c_all, ada_w, ada_b):
    depth = ada_w.shape[0]
    rows = c_all.shape[0]
    tn = 1536
    return pl.pallas_call(
        _mod_kernel,
        grid=(depth, 6 * D_MODEL // tn),
        in_specs=[pl.BlockSpec((rows, D_MODEL), lambda l, j: (0, 0)),
                  pl.BlockSpec((None, D_MODEL, tn), lambda l, j: (l, 0, j)),
                  pl.BlockSpec((None, 1, tn), lambda l, j: (l, 0, j))],
        out_specs=pl.BlockSpec((None, rows, tn), lambda l, j: (l, 0, j)),
        out_shape=jax.ShapeDtypeStruct((depth, rows, 6 * D_MODEL), F32),
        compiler_params=_cparams(2), name="modulation",
    )(c_all.astype(BF16), ada_w, ada_b.reshape(depth, 1, 6 * D_MODEL))


def _inproj_kernel(x_ref, sc_ref, sh_ref, g_ref, w_ref, o_ref):
    h = _norm_mod(x_ref[...], g_ref[...], sc_ref[...], sh_ref[...])
    o_ref[...] = jnp.dot(h.astype(BF16), w_ref[...], preferred_element_type=F32)


def _inproj(x, mod, g, w_mix, tm, seq):
    rows = x.shape[0]
    return pl.pallas_call(
        _inproj_kernel,
        grid=(rows // tm,),
        in_specs=[pl.BlockSpec((tm, D_MODEL), lambda i: (i, 0)),
                  _mod_spec(mod, 1, tm, seq), _mod_spec(mod, 0, tm, seq),
                  _const_spec(g), _const_spec(w_mix)],
        out_specs=pl.BlockSpec((tm, NP_COLS), lambda i: (i, 0)),
        out_shape=jax.ShapeDtypeStruct((rows, NP_COLS), F32),
        compiler_params=_cparams(1), name="inproj",
    )(x, mod, mod, g, w_mix)


def _rope_apply(x, cos, sn, left, shift):
    n = x.shape[-1]
    rot = jnp.where(left > 0.0, pltpu.roll(x, n - shift, 1), pltpu.roll(x, shift, 1))
    return x * cos + rot * sn


def _dsa_prep_kernel(q_ref, k_ref, v_ref, qi_ref, kw_ref, cq_ref, sq_ref, ci_ref, si_ref, ckw_ref, skw_ref,
                     lq_ref, li_ref, lkw_ref, qg_ref, kg_ref, wavg_ref,
                     qn_ref, kn_ref, qir_ref, kwr_ref, qh_ref, kh_ref, vh_ref, qih_ref, kib_ref):
    wavg = wavg_ref[...]

    def headnorm(x, g):
        ms = _split_dot(x * x, wavg, 3)
        return x * lax.rsqrt(ms + EPS) * g

    cq, sq, lq = cq_ref[...], sq_ref[...], lq_ref[...]
    qn = _rope_apply(headnorm(q_ref[...], qg_ref[...]), cq, sq, lq, ROT // 2)
    kn = _rope_apply(headnorm(k_ref[...], kg_ref[...]), cq, sq, lq, ROT // 2)
    qir = _rope_apply(qi_ref[...], ci_ref[...], si_ref[...], li_ref[...], IDX_ROT // 2)
    kwr = _rope_apply(kw_ref[...], ckw_ref[...], skw_ref[...], lkw_ref[...], IDX_ROT // 2)
    qn_ref[...] = qn
    kn_ref[...] = kn
    qir_ref[...] = qir
    kwr_ref[...] = kwr
    v = v_ref[...]
    for h in range(ATT_H):
        hs = slice(h * HD, (h + 1) * HD)
        qh_ref[h] = qn[:, hs].astype(BF16)
        kh_ref[h] = kn[:, hs].astype(BF16)
        vh_ref[h] = v[:, hs].astype(BF16)
    for h in range(IDX_H):
        qih_ref[h] = qir[:, h * IDX_D:(h + 1) * IDX_D].astype(BF16)
    kib_ref[...] = kwr.astype(BF16)


def _dsa_prep(P, tabs, consts, qg, kg, tm, seq):
    rows = P.shape[0]
    nt = seq // tm if tabs["periodic"] else None

    def tab_spec(w):
        if tabs["periodic"]:
            return pl.BlockSpec((tm, w), lambda i: (i % nt, 0))
        return pl.BlockSpec((tm, w), lambda i: (i, 0))

    def col(off, w):
        return pl.BlockSpec((tm, w), lambda i: (i, off // w))

    out256 = jax.ShapeDtypeStruct((rows, 256), F32)
    heads = lambda n, d: (pl.BlockSpec((n, tm, d), lambda i: (0, i, 0)), jax.ShapeDtypeStruct((n, rows, d), BF16))
    hq, hi = heads(ATT_H, HD), heads(IDX_H, IDX_D)
    return pl.pallas_call(
        _dsa_prep_kernel,
        grid=(rows // tm,),
        in_specs=[col(C_DQ, 256), col(C_DK, 256), col(C_DV, 256), col(C_DQI, 256), col(C_DKW, 128),
                  tab_spec(256), tab_spec(256), tab_spec(256), tab_spec(256), tab_spec(128), tab_spec(128),
                  _const_spec(consts["left_q"]), _const_spec(consts["left_i"]), _const_spec(consts["left_kw"]),
                  _const_spec(qg), _const_spec(kg), _const_spec(consts["wavg"])],
        out_specs=[pl.BlockSpec((tm, 256), lambda i: (i, 0))] * 3 + [pl.BlockSpec((tm, 128), lambda i: (i, 0))]
                  + [hq[0], hq[0], hq[0], hi[0], pl.BlockSpec((tm, 128), lambda i: (i, 0))],
        out_shape=[out256, out256, out256, jax.ShapeDtypeStruct((rows, 128), F32),
                   hq[1], hq[1], hq[1], hi[1], jax.ShapeDtypeStruct((rows, 128), BF16)],
        compiler_params=_cparams(1), name="dsa_prep",
    )(P, P, P, P, P, tabs["cq"], tabs["sq"], tabs["ci"], tabs["si"], tabs["ckw"], tabs["skw"],
      consts["left_q"], consts["left_i"], consts["left_kw"], qg, kg, consts["wavg"])


def _score_keys(scores, valid):
    s = jnp.where(scores == 0.0, 0.0, scores)
    bits = pltpu.bitcast(s, I32)
    key = bits ^ (jnp.right_shift(bits, 31) & 0x7FFFFFFF)
    return jnp.where(valid, key, INT_MIN)


def _topk_select(key_ref, k, col):
    rows, cols = key_ref.shape
    kf = float(k)
    nbits = max(1, int(math.ceil(math.log2(cols))))

    ng = 4 if rows % 32 == 0 else 1
    rg = rows // ng

    def count_ge(g, c):
        return jnp.sum(jnp.where(key_ref[g * rg:(g + 1) * rg, :] >= c, 1.0, 0.0), axis=-1, keepdims=True)

    bases = tuple(jnp.where(count_ge(g, jnp.zeros((rg, 1), I32)) >= kf, 0, INT_MIN).astype(I32)
                  for g in range(ng))

    def bit_step(i, bases):
        bit = lax.shift_left(jnp.int32(1), 30 - i)
        return tuple(jnp.where(count_ge(g, b | bit) >= kf, b | bit, b) for g, b in enumerate(bases))

    bases = lax.fori_loop(0, 31, bit_step, bases, unroll=4)
    thr = bases[0] if ng == 1 else jnp.concatenate(bases, axis=0)
    key = key_ref[...]
    need = kf - jnp.sum(jnp.where(key > thr, 1.0, 0.0), axis=-1, keepdims=True)
    excess = jnp.sum(jnp.where(key == thr, 1.0, 0.0), axis=-1, keepdims=True) > need

    def pos_step(i, pos):
        cand = pos + lax.shift_left(jnp.int32(1), nbits - 1 - i)
        hit = jnp.where(key_ref[...] == thr, jnp.where(col < cand, 1.0, 0.0), 0.0)
        return jnp.where(jnp.sum(hit, axis=-1, keepdims=True) < need, cand, pos)

    n_iter = jnp.where(jnp.max(jnp.where(excess, 1, 0)) > 0, nbits, 0)
    pos = lax.fori_loop(0, n_iter, pos_step, jnp.zeros((rows, 1), I32))
    pos = jnp.where(excess, pos, cols)
    return jnp.where(key > thr, 1.0, jnp.where(key == thr, jnp.where(col <= pos, 1.0, 0.0), 0.0))


def _dsa_attn_kernel(qh_ref, qih_ref, kwq_ref, kh_ref, vh_ref, kib_ref, o_ref, key_ref, *, tq, topk, nvar):
    it = pl.program_id(1)
    t0 = it * tq
    seq = kh_ref.shape[1]
    step = seq // nvar
    per = step // tq

    def body(klen):
        ki = kib_ref[0:klen, 0:IDX_D]
        kwq = kwq_ref[...]
        scores = jnp.zeros((tq, klen), F32)
        for h in range(IDX_H):
            s = lax.dot_general(qih_ref[h], ki, _LANES, preferred_element_type=F32)
            w = kwq[:, IDX_D + h:IDX_D + h + 1] * (IDX_D ** -0.5)
            scores = scores + w * jnp.maximum(s, 0.0)
        col = lax.broadcasted_iota(I32, (tq, klen), 1)
        row = t0 + lax.broadcasted_iota(I32, (tq, klen), 0)
        causal = col <= row
        keys = key_ref.at[:, pl.ds(0, klen)]
        keys[...] = _score_keys(scores, causal)
        sel = jnp.where(causal, _topk_select(keys, topk, col), 0.0) > 0.0
        def head(h):
            lg = lax.dot_general(qh_ref[h], kh_ref[h, 0:klen, :], _LANES,
                                 preferred_element_type=F32) * (HD ** -0.5)
            yield
            lg = jnp.where(sel, lg, -jnp.inf)
            m = jnp.max(lg, axis=-1, keepdims=True)
            yield
            p = jnp.exp(lg - m)
            l = jnp.sum(p, axis=-1, keepdims=True)
            o = jnp.dot(p.astype(BF16), vh_ref[h, 0:klen, :], preferred_element_type=F32)
            yield
            o_ref[:, h * HD:(h + 1) * HD] = o / l

        chains = [head(h) for h in range(ATT_H)]
        while chains:
            chains = [ch for ch in chains if next(ch, True) is None]

    for var in range(nvar):
        pl.when(it // per == var)(functools.partial(body, (var + 1) * step))


def _dsa_attn(qh, kh, vh, qih, kwr, kib, batch, seq, tq):
    topk = min(TOPK_MAX, seq // 4)
    nq = seq // tq
    nvar = min(4, nq)
    while seq // nvar < topk:
        nvar //= 2
    return pl.pallas_call(
        functools.partial(_dsa_attn_kernel, tq=tq, topk=topk, nvar=nvar),
        grid=(batch, nq),
        in_specs=[pl.BlockSpec((ATT_H, tq, HD), lambda b, i: (0, b * nq + i, 0)),
                  pl.BlockSpec((IDX_H, tq, IDX_D), lambda b, i: (0, b * nq + i, 0)),
                  pl.BlockSpec((tq, 128), lambda b, i: (b * nq + i, 0)),
                  pl.BlockSpec((ATT_H, seq, HD), lambda b, i: (0, b, 0)),
                  pl.BlockSpec((ATT_H, seq, HD), lambda b, i: (0, b, 0)),
                  pl.BlockSpec((seq, 128), lambda b, i: (b, 0))],
        out_specs=pl.BlockSpec((tq, 256), lambda b, i: (b * nq + i, 0)),
        out_shape=jax.ShapeDtypeStruct((batch * seq, 256), F32),
        scratch_shapes=[pltpu.VMEM((tq, seq), I32)],
        compiler_params=_cparams(2), name="dsa_attn",
    )(qh, qih, kwr, kh, vh, kib)


def _split3_rhs(m, x):
    hi = x.astype(BF16)
    r = x - hi.astype(F32)
    mid = r.astype(BF16)
    lo = (r - mid.astype(F32)).astype(BF16)
    return (jnp.dot(m, hi, preferred_element_type=F32) + jnp.dot(m, mid, preferred_element_type=F32)
            + jnp.dot(m, lo, preferred_element_type=F32))


_LANES = (((1,), (1,)), ((), ()))
_ROWS = (((0,), (0,)), ((), ()))


def _gla_kernel(v_ref, r_ref, qk_ref, a_ref, st0_ref, a2h_ref, a2l_ref, ab_ref, ng_ref,
                tri_ref, ones_ref, amask_ref, hm128_ref, hm256_ref, bd_ref, wavg_ref,
                o_ref, st_ref, *, tile, chunk, lvalid, independent):
    if not independent:
        @pl.when(pl.program_id(1) == 0)
        def _():
            st_ref[...] = st0_ref[...]

    nch = tile // chunk
    z = _dot3(a_ref[...], a2h_ref[...], a2l_ref[...]) + ab_ref[...]
    la = (jnp.minimum(z, 0.0) - jnp.log1p(jnp.exp(-jnp.abs(z)))) * (1.0 / GLA_TAU)
    q = qk_ref[:, 0:128] * (GLA_DK ** -0.5)
    k = qk_ref[:, 128:256]
    v = v_ref[...]
    if independent and lvalid < chunk:
        keep = (lax.broadcasted_iota(I32, (tile, 128), 0) % chunk) < lvalid
        la = jnp.where(keep, la, 0.0)
        k = jnp.where(keep, k, 0.0)
        v = jnp.where((lax.broadcasted_iota(I32, (tile, 256), 0) % chunk) < lvalid, v, 0.0)
    b = _split3_rhs(tri_ref[...], la)
    tot = _split3_rhs(ones_ref[...], la)
    qe = q * jnp.exp(b)
    kinv = (k * jnp.exp(-b)).astype(BF16)
    kd = (k * jnp.exp(tot - b)).astype(BF16)
    vb = v.astype(BF16)
    hm128 = hm128_ref[...]
    hm256 = hm256_ref[...]
    qblk = jnp.concatenate([qe[c * chunk:(c + 1) * chunk] * hm128[h:h + 1]
                            for c in range(nch) for h in range(GLA_H)], axis=0).astype(BF16)
    att = lax.dot_general(qblk, kinv, _LANES, preferred_element_type=F32) * amask_ref[...]
    intra = jnp.dot(att.astype(BF16), vb, preferred_element_type=F32)
    qeb = qe.astype(BF16)
    st = None if independent else st_ref[...]
    for c in range(nch):
        rows = slice(c * chunk, (c + 1) * chunk)
        if independent:
            st = st0_ref[c]
        o = lax.dot_general(qeb[rows], st.astype(BF16), _LANES, preferred_element_type=F32)
        for h in range(GLA_H):
            r0 = (c * GLA_H + h) * chunk
            o = o + intra[r0:r0 + chunk] * hm256[h:h + 1]
        o_ref[rows, :] = o
        upd = lax.dot_general(vb[rows], kd[rows], _ROWS, preferred_element_type=F32)
        st = st * jnp.exp(tot[c * chunk:c * chunk + 1]) + upd * bd_ref[...]
        if independent:
            st_ref[c] = st
    if not independent:
        st_ref[...] = st
    o = o_ref[...]
    ms = _split_dot(o * o, wavg_ref[...], 2)
    r = r_ref[...]
    o_ref[...] = o * lax.rsqrt(ms + EPS) * ng_ref[...] * (r * _sigmoid(r))


def _gla(P, st0, w, consts, batch, seq, tg, chunk, lvalid):
    independent = lvalid < seq
    if independent:
        assert seq == chunk
        nseq = tg // chunk
        nt, grid = 1, (batch // nseq, 1)
        st_spec = pl.BlockSpec((nseq, 256, 128), lambda b, j: (b, 0, 0))
    else:
        nt, grid = seq // tg, (batch, seq // tg)
        st_spec = pl.BlockSpec((None, 256, 128), lambda b, j: (b, 0, 0))

    def col(off, wd):
        return pl.BlockSpec((tg, wd), lambda b, j: (b * nt + j, off // wd))

    gm = consts["gla"][(tg, chunk)]
    cs = [w["gla_a2h"], w["gla_a2l"], w["gla_ab"], w["gla_ng"],
          gm["tri"], gm["ones"], gm["amask"], consts["hm128"], consts["hm8"], consts["bd"], consts["wavg"]]
    return pl.pallas_call(
        functools.partial(_gla_kernel, tile=tg, chunk=chunk, lvalid=lvalid, independent=independent),
        grid=grid,
        in_specs=[col(C_GV, 256), col(C_GR, 256), col(C_GQK, 256), col(C_GA, 128), st_spec]
                 + [_const_spec(a) for a in cs],
        out_specs=[pl.BlockSpec((tg, 256), lambda b, j: (b * nt + j, 0)), st_spec],
        out_shape=[jax.ShapeDtypeStruct((batch * seq, 256), F32),
                   jax.ShapeDtypeStruct((batch, 256, 128), F32)],
        compiler_params=_cparams(2), name="gla",
    )(P, P, P, P, st0, *cs)


def _gelu_tanh(x):
    return 0.5 * x * (1.0 + jnp.tanh(math.sqrt(2.0 / math.pi) * (x + 0.044715 * (x * x * x))))


def _s5_kernel(u_ref, x0_ref, a_ref, bh_ref, c_ref, d_ref, gw_ref, gb_ref,
               o_ref, xf_ref, st_s, bur_s, bui_s, xr_s, xi_s, *, tile, last_row):
    @pl.when(pl.program_id(1) == 0)
    def _():
        st_s[...] = x0_ref[...]

    u = u_ref[...]
    bu = jnp.dot(u.astype(BF16), bh_ref[...], preferred_element_type=F32)
    bur_s[...] = bu[:, 0:S5_N]
    bui_s[...] = bu[:, S5_N:2 * S5_N]
    ar = a_ref[:, 0:S5_N]
    ai = a_ref[:, S5_N:2 * S5_N]

    def step(t, carry):
        xr, xi = carry
        row = pl.ds(t, 1)
        nr = ar * xr - ai * xi + bur_s[row, :]
        ni = ar * xi + ai * xr + bui_s[row, :]
        xr_s[row, :] = nr
        xi_s[row, :] = ni
        return nr, ni

    xr, xi = lax.fori_loop(0, tile, step, (st_s[:, 0:S5_N], st_s[:, S5_N:2 * S5_N]))
    st_s[:, 0:S5_N] = xr
    st_s[:, S5_N:2 * S5_N] = xi
    y = (jnp.dot(xr_s[...].astype(BF16), c_ref[0:S5_N, :], preferred_element_type=F32)
         + jnp.dot(xi_s[...].astype(BF16), c_ref[S5_N:2 * S5_N, :], preferred_element_type=F32)
         + d_ref[...] * u)
    z = _gelu_tanh(y)
    gate = jnp.dot(z.astype(BF16), gw_ref[...], preferred_element_type=F32) + gb_ref[...]
    o_ref[...] = z * _sigmoid(gate)
    xf_ref[:, 0:S5_N] = xr_s[last_row:last_row + 1, :]
    xf_ref[:, S5_N:2 * S5_N] = xi_s[last_row:last_row + 1, :]


def _s5(P, x0, w, batch, seq, tile, lvalid):
    nt = seq // tile
    last_row = (lvalid - 1) % tile
    cs = [w["s5_a"], w["s5_bh"], w["s5_c"], w["s5_d"], w["s5_gw"], w["s5_gb"]]
    return pl.pallas_call(
        functools.partial(_s5_kernel, tile=tile, last_row=last_row),
        grid=(batch, nt),
        in_specs=[pl.BlockSpec((tile, 256), lambda b, j: (b * nt + j, C_S5 // 256)),
                  pl.BlockSpec((None, 1, 2 * S5_N), lambda b, j: (b, 0, 0))] + [_const_spec(a) for a in cs],
        out_specs=[pl.BlockSpec((tile, 256), lambda b, j: (b * nt + j, 0)),
                   pl.BlockSpec((None, 1, 2 * S5_N), lambda b, j: (b, 0, 0))],
        out_shape=[jax.ShapeDtypeStruct((batch * seq, 256), F32),
                   jax.ShapeDtypeStruct((batch, 1, 2 * S5_N), F32)],
        scratch_shapes=[pltpu.VMEM((1, 2 * S5_N), F32)] + [pltpu.VMEM((tile, S5_N), F32)] * 4,
        compiler_params=_cparams(2), name="s5",
    )(P, x0, *cs)


def _s5_step_kernel(u_ref, x0_ref, a_ref, bh_ref, c_ref, d_ref, gw_ref, gb_ref, o_ref, xf_ref):
    u = u_ref[...]
    bu = jnp.dot(u.astype(BF16), bh_ref[...], preferred_element_type=F32)
    ar, ai = a_ref[:, 0:S5_N], a_ref[:, S5_N:2 * S5_N]
    xr0, xi0 = x0_ref[:, 0:S5_N], x0_ref[:, S5_N:2 * S5_N]
    xr = ar * xr0 - ai * xi0 + bu[:, 0:S5_N]
    xi = ar * xi0 + ai * xr0 + bu[:, S5_N:2 * S5_N]
    y = (jnp.dot(xr.astype(BF16), c_ref[0:S5_N, :], preferred_element_type=F32)
         + jnp.dot(xi.astype(BF16), c_ref[S5_N:2 * S5_N, :], preferred_element_type=F32)
         + d_ref[...] * u)
    z = _gelu_tanh(y)
    gate = jnp.dot(z.astype(BF16), gw_ref[...], preferred_element_type=F32) + gb_ref[...]
    o_ref[...] = z * _sigmoid(gate)
    xf_ref[:, 0:S5_N] = xr
    xf_ref[:, S5_N:2 * S5_N] = xi


def _s5_step(P1, x0, w):
    n = P1.shape[0]
    cs = [w["s5_a"], w["s5_bh"], w["s5_c"], w["s5_d"], w["s5_gw"], w["s5_gb"]]
    return pl.pallas_call(
        _s5_step_kernel,
        grid=(1,),
        in_specs=[pl.BlockSpec((n, 256), lambda i: (0, C_S5 // 256)),
                  pl.BlockSpec((n, 2 * S5_N), lambda i: (0, 0))] + [_const_spec(a) for a in cs],
        out_specs=[pl.BlockSpec((n, 256), lambda i: (0, 0)), pl.BlockSpec((n, 2 * S5_N), lambda i: (0, 0))],
        out_shape=[jax.ShapeDtypeStruct((n, 256), F32), jax.ShapeDtypeStruct((n, 2 * S5_N), F32)],
        compiler_params=_cparams(1), name="s5_step",
    )(P1, x0, *cs)


RW_CH = 16
RW_SC = 4 * RW_CH


def _rwkv_chunked(sf_ref, r_s, k_s, v_s, al_s, be_s, lw_s, y_s, sbd_s, hm, tri, onesb, strict, incl, bd,
                  nb, tile):
    nh = RW_H

    def blk(x):
        return jnp.concatenate([x[RW_CH * c:RW_CH * (c + 1)] * hm[h:h + 1]
                                for c in range(4) for h in range(nh)], axis=0)

    def rep(x):
        return jnp.concatenate([x[RW_CH * c:RW_CH * (c + 1)] for c in range(4) for _ in range(nh)], axis=0)

    def stack_heads(x):
        return jnp.concatenate([x[RW_CH * c:RW_CH * (c + 1), RW_N * h:RW_N * (h + 1)]
                                for c in range(4) for h in range(nh)], axis=0)

    def mm(a, b):
        return jnp.dot(a, b, preferred_element_type=F32)

    for b in range(nb):
        sbd_s[b] = sf_ref[b]

    def superchunk(sc, carry):
        r0 = pl.multiple_of(sc * RW_SC, RW_SC)
        rows = pl.ds(r0, RW_SC)

        def chain(b):
            lw = lw_s[b, rows, :]
            cum = _split3_rhs(tri, lw)
            tot = _split3_rhs(onesb, lw)
            rr, kx, vv = r_s[b, rows, :], k_s[b, rows, :], v_s[b, rows, :]
            al, be = al_s[b, rows, :], be_s[b, rows, :]
            pinv = jnp.exp(-cum)
            pend = jnp.exp(tot - cum)
            ab = al * jnp.exp(cum - lw)
            rb = rr * jnp.exp(cum)
            bt, kt, bp, kp = be * pinv, kx * pinv, be * pend, kx * pend
            ablk = blk(ab)
            lhs = jnp.concatenate([ablk, blk(rb)], axis=0).astype(BF16)
            rhs = jnp.concatenate([rep(bt), rep(kt)], axis=0).astype(BF16)
            g = lax.dot_general(lhs, rhs, _LANES, preferred_element_type=F32)
            yield
            mb = g[0:256, 0:256] * strict
            mk = g[0:256, 256:512] * strict
            myb = (g[256:512, 0:256] * incl).astype(BF16)
            myk = (g[256:512, 256:512] * incl).astype(BF16)
            vst = stack_heads(vv).astype(BF16)
            w0 = mm(mk.astype(BF16), vst)
            y0 = mm(myk, vst)
            u, mp = mb, mb
            for _ in range(3):
                mpb = mp.astype(BF16)
                mp = mm(mpb, mpb)
                yield
                u = u + mp + mm(u.astype(BF16), mp.astype(BF16))
                yield
            ub = u.astype(BF16)
            ab1 = (ablk + mm(ub, ablk.astype(BF16))).astype(BF16)
            rblk = lhs[256:512]
            z0 = w0 + mm(ub, w0.astype(BF16))
            bpk = jnp.concatenate([blk(bp), blk(kp)], axis=1).astype(BF16)
            yield
            S = sbd_s[b]
            nr = nh * RW_CH
            for c in range(4):
                d0 = nr * c
                lc = jnp.concatenate([ab1[d0:d0 + nr], rblk[d0:d0 + nr]], axis=0)
                s_hi = S.astype(BF16)
                s_lo = (S - s_hi.astype(F32)).astype(BF16)
                x = (lax.dot_general(lc, s_hi, _LANES, preferred_element_type=F32)
                     + lax.dot_general(lc, s_lo, _LANES, preferred_element_type=F32))
                yield
                zst = x[0:nr] + z0[d0:d0 + nr]
                zb = zst.astype(BF16)
                yst = x[nr:2 * nr] + y0[d0:d0 + nr] + mm(myb[d0:d0 + nr, d0:d0 + nr], zb)
                y_s[b, pl.ds(r0 + RW_CH * c, RW_CH), :] = jnp.concatenate(
                    [yst[RW_CH * h:RW_CH * (h + 1)] for h in range(nh)], axis=1)
                upd = (lax.dot_general(zb, bpk[d0:d0 + nr, 0:256], _ROWS, preferred_element_type=F32)
                       + lax.dot_general(vst[d0:d0 + nr], bpk[d0:d0 + nr, 256:512], _ROWS,
                                         preferred_element_type=F32))
                S = S * jnp.exp(tot[RW_CH * c:RW_CH * c + 1]) + upd
                yield
            sbd_s[b] = S

        chains = [chain(b) for b in range(nb)]
        while chains:
            chains = [ch for ch in chains if next(ch, True) is None]
        return carry

    lax.fori_loop(0, tile // RW_SC, superchunk, 0)
    for b in range(nb):
        sf_ref[b] = sbd_s[b]


def _rwkv_kernel(r_ref, k_ref, v_ref, lo_ref, s0_ref, prev_ref, mu_ref, w0_ref, a0_ref,
                 w2h_ref, w2l_ref, a2h_ref, a2l_ref, g2h_ref, g2l_ref, kkp_ref, ka_ref, rk_ref, ng_ref,
                 wones_ref, wavg_ref, idt_ref, hm_ref, tri_ref, ones_ref, strict_ref, incl_ref, bd_ref,
                 o_ref, sf_ref, r_s, k_s, v_s, kk_s, ka_s, w_s, y_s, g_s, bo_s, prev_s, sbd_s,
                 *, nb, tile, nsteps, chunked):
    @pl.when(pl.program_id(1) == 0)
    def _():
        sf_ref[...] = s0_ref[...]
        prev_s[...] = prev_ref[...]

    wones = wones_ref[...]
    wavg = wavg_ref[...]
    idt = idt_ref[...]
    row0_256 = lax.broadcasted_iota(I32, (tile, 256), 0) == 0
    row0_128 = lax.broadcasted_iota(I32, (tile, 128), 0) == 0

    def shift_mix(p, prev_row, mu, row0):
        sh = jnp.where(row0, prev_row, pltpu.roll(p, 1, 0))
        return p + (sh - p) * mu

    for b in range(nb):
        pr, pk, pv, plo = r_ref[b], k_ref[b], v_ref[b], lo_ref[b]
        r = shift_mix(pr, prev_s[b, :, 0:256], mu_ref[:, 0:256], row0_256)
        k = shift_mix(pk, prev_s[b, :, 256:512], mu_ref[:, 256:512], row0_256)
        v = shift_mix(pv, prev_s[b, :, 512:768], mu_ref[:, 512:768], row0_256)
        lo = shift_mix(plo, prev_s[b, :, 768:896], mu_ref[:, 768:896], row0_128)
        prev_s[b, :, 0:256] = pr[tile - 1:tile, :]
        prev_s[b, :, 256:512] = pk[tile - 1:tile, :]
        prev_s[b, :, 512:768] = pv[tile - 1:tile, :]
        prev_s[b, :, 768:896] = plo[tile - 1:tile, :]
        wl = w0_ref[...] + _dot3(jnp.tanh(lo), w2h_ref[...], w2l_ref[...])
        w = -_softplus(-wl) - 0.5
        a = _sigmoid(a0_ref[...] + _dot3(lo, a2h_ref[...], a2l_ref[...]))
        g = _dot3(_sigmoid(lo), g2h_ref[...], g2l_ref[...])
        kk = k * kkp_ref[...]
        kk = kk * lax.rsqrt(_split_dot(kk * kk, wones, 3) + EPS)
        k2 = k * (1.0 + (a - 1.0) * ka_ref[...])
        bonus = _split_dot(r * k2 * rk_ref[...], wones, 3) * v
        r_s[b] = r
        k_s[b] = k2
        v_s[b] = v
        kk_s[b] = -kk
        ka_s[b] = kk * a
        w_s[b] = -jnp.exp(w)
        if nsteps < tile:
            y_s[b] = jnp.zeros((tile, 256), F32)
        g_s[b] = g
        bo_s[b] = bonus

    def step(t, carry):
        for b in range(nb):
            row = pl.ds(t, 1)
            S = sf_ref[b]
            sa = _split_dot(S * kk_s[b, row, :], wones, 2)
            vcol = _split_dot(idt * v_s[b, row, :], wones, 2)
            Sn = S * jnp.exp(w_s[b, row, :]) + sa * ka_s[b, row, :] + vcol * k_s[b, row, :]
            yb = jnp.dot((Sn * r_s[b, row, :]).astype(BF16), wones, preferred_element_type=F32)
            y_s[b, row, :] = jnp.sum(yb * idt, axis=0, keepdims=True)
            sf_ref[b] = Sn
        return carry

    if chunked:
        _rwkv_chunked(sf_ref, r_s, k_s, v_s, kk_s, ka_s, w_s, y_s, sbd_s, hm_ref[...], tri_ref[...],
                      ones_ref[...], strict_ref[...], incl_ref[...], bd_ref[...], nb, tile)
    else:
        lax.fori_loop(0, nsteps, step, 0)
    for b in range(nb):
        y = y_s[b]
        mu = _split_dot(y, wavg, 3)
        yc = y - mu
        var = _split_dot(yc * yc, wavg, 3)
        o_ref[b] = (yc * lax.rsqrt(var + RW_GN_EPS) * ng_ref[...] + bo_s[b]) * g_s[b]


def _rwkv(P3, s0, prev, w, consts, nb, tile, lvalid):
    batch, seq, _ = P3.shape
    nt = seq // tile
    nsteps = tile if lvalid >= seq else lvalid
    cs = [w["rw_mu"], w["rw_w0"], w["rw_a0"], w["rw_w2h"], w["rw_w2l"], w["rw_a2h"], w["rw_a2l"],
          w["rw_g2h"], w["rw_g2l"], w["rw_kk"], w["rw_ka"], w["rw_rk"], w["rw_ng"],
          consts["wones"], consts["wavg"], consts["idt"], consts["hm8"], consts["rw_tri"], consts["rw_ones"],
          consts["rw_strict"], consts["rw_incl"], consts["bd256"]]
    chunked = nsteps == tile and tile % RW_SC == 0

    def col(off, wd):
        return pl.BlockSpec((nb, tile, wd), lambda g, j: (g, j, off // wd))

    big = lambda: pltpu.VMEM((nb, tile, 256), F32)
    return pl.pallas_call(
        functools.partial(_rwkv_kernel, nb=nb, tile=tile, nsteps=nsteps, chunked=chunked),
        grid=(batch // nb, nt),
        in_specs=[col(C_RW, 256), col(C_RW + 256, 256), col(C_RW + 512, 256), col(C_RWLO, 128),
                  pl.BlockSpec((nb, RW_N, 256), lambda g, j: (g, 0, 0)),
                  pl.BlockSpec((nb, 1, RW_COLS), lambda g, j: (g, 0, 0))] + [_const_spec(a) for a in cs],
        out_specs=[pl.BlockSpec((nb, tile, 256), lambda g, j: (g, j, 0)),
                   pl.BlockSpec((nb, RW_N, 256), lambda g, j: (g, 0, 0))],
        out_shape=[jax.ShapeDtypeStruct((batch, seq, 256), F32),
                   jax.ShapeDtypeStruct((batch, RW_N, 256), F32)],
        scratch_shapes=[big(), big(), big(), big(), big(), big(), big(), big(), big(),
                        pltpu.VMEM((nb, 1, RW_COLS), F32),
                        pltpu.VMEM((nb, RW_N, 256) if chunked else (1, 8, 128), F32)],
        compiler_params=_cparams(2), name="rwkv",
    )(P3, P3, P3, P3, s0, prev, *cs)


def _merge_kernel(x_ref, sc_ref, sh_ref, gt_ref, g_ref, og_ref, oa_ref, os_ref, or_ref,
                  wg_ref, wbr_ref, wo_ref, o_ref):
    x = x_ref[...]
    h = _norm_mod(x, g_ref[...], sc_ref[...], sh_ref[...]).astype(BF16)
    merged = None
    for b, oref in enumerate((og_ref, oa_ref, os_ref, or_ref)):
        gate = _sigmoid(jnp.dot(h, wg_ref[:, b * D_MODEL:(b + 1) * D_MODEL], preferred_element_type=F32))
        proj = jnp.dot(oref[...].astype(BF16), wbr_ref[b], preferred_element_type=F32)
        merged = gate * proj if merged is None else merged + gate * proj
    y = jnp.dot(merged.astype(BF16), wo_ref[...], preferred_element_type=F32)
    o_ref[...] = x + gt_ref[...] * y


def _merge(x, mod, g, outs, w, tm, seq):
    rows = x.shape[0]
    row256 = pl.BlockSpec((tm, 256), lambda i: (i, 0))
    return pl.pallas_call(
        _merge_kernel,
        grid=(rows // tm,),
        in_specs=[pl.BlockSpec((tm, D_MODEL), lambda i: (i, 0)),
                  _mod_spec(mod, 1, tm, seq), _mod_spec(mod, 0, tm, seq), _mod_spec(mod, 2, tm, seq),
                  _const_spec(g), row256, row256, row256, row256,
                  _const_spec(w["w_gates"]), _const_spec(w["w_br"]), _const_spec(w["w_o"])],
        out_specs=pl.BlockSpec((tm, D_MODEL), lambda i: (i, 0)),
        out_shape=jax.ShapeDtypeStruct((rows, D_MODEL), F32),
        compiler_params=_cparams(1), name="merge",
    )(x, mod, mod, mod, g, *outs, w["w_gates"], w["w_br"], w["w_o"])


def _ffn_kernel(x_ref, sc_ref, sh_ref, gt_ref, g_ref, w1_ref, w2_ref, o_ref):
    x = x_ref[...]
    h = _norm_mod(x, g_ref[...], sc_ref[...], sh_ref[...]).astype(BF16)
    acc = None
    for c in range(D_FF // D_MODEL):
        cs = slice(c * D_MODEL, (c + 1) * D_MODEL)
        u = jnp.maximum(jnp.dot(h, w1_ref[:, cs], preferred_element_type=F32), 0.0)
        d = jnp.dot((u * u).astype(BF16), w2_ref[cs, :], preferred_element_type=F32)
        acc = d if acc is None else acc + d
    o_ref[...] = x + gt_ref[...] * acc


def _ffn(x, mod, g, w, tm, seq):
    rows = x.shape[0]
    return pl.pallas_call(
        _ffn_kernel,
        grid=(rows // tm,),
        in_specs=[pl.BlockSpec((tm, D_MODEL), lambda i: (i, 0)),
                  _mod_spec(mod, 4, tm, seq), _mod_spec(mod, 3, tm, seq), _mod_spec(mod, 5, tm, seq),
                  _const_spec(g), _const_spec(w["w_ff1"]), _const_spec(w["w_ff2"])],
        out_specs=pl.BlockSpec((tm, D_MODEL), lambda i: (i, 0)),
        out_shape=jax.ShapeDtypeStruct((rows, D_MODEL), F32),
        compiler_params=_cparams(1), name="ffn",
    )(x, mod, mod, mod, g, w["w_ff1"], w["w_ff2"])


def _ds_scores_kernel(pt_ref, q8_ref, w8_ref, kcur_ref, *refs, npages):
    pages, o_ref = refs[:npages], refs[npages]
    q8 = q8_ref[...]
    q8b = q8.astype(BF16)
    w8 = w8_ref[...] * (IDX_D ** -0.5)
    for p in range(npages):
        s = jnp.dot(q8b, pages[p][...].astype(BF16), preferred_element_type=F32)
        o_ref[p:p + 1, :] = jnp.sum(w8 * jnp.maximum(s, 0.0), axis=0, keepdims=True)
    s_cur = jnp.sum(q8 * kcur_ref[...], axis=-1, keepdims=True)
    i_cur = jnp.sum(w8 * jnp.maximum(s_cur, 0.0), axis=0, keepdims=True)
    lane = lax.broadcasted_iota(I32, (8, 128), 1)
    rowi = lax.broadcasted_iota(I32, (8, 128), 0)
    o_ref[npages:npages + 8, :] = jnp.where((lane == 0) & (rowi == 0), i_cur, -jnp.inf)


def _ds_scores(page_table, q8, w8, kcur, cache_idx, layer):
    n, npages = page_table.shape
    page_specs = [pl.BlockSpec((None, None, IDX_D, PAGE_SIZE), lambda i, pt, p=p: (layer, pt[i, p], 0, 0))
                  for p in range(npages)]
    return pl.pallas_call(
        functools.partial(_ds_scores_kernel, npages=npages),
        grid_spec=pltpu.PrefetchScalarGridSpec(
            num_scalar_prefetch=1, grid=(n,),
            in_specs=[pl.BlockSpec((None, IDX_H, IDX_D), lambda i, pt: (i, 0, 0)),
                      pl.BlockSpec((None, IDX_H, 1), lambda i, pt: (i, 0, 0)),
                      pl.BlockSpec((None, 1, IDX_D), lambda i, pt: (i, 0, 0))] + page_specs,
            out_specs=pl.BlockSpec((None, npages + 8, 128), lambda i, pt: (i, 0, 0))),
        out_shape=jax.ShapeDtypeStruct((n, npages + 8, 128), F32),
        compiler_params=_cparams(1), name="ds_scores",
    )(page_table, q8, w8, kcur, *([cache_idx] * npages))


def _ds_select_kernel(s_ref, o_ref, key_ref, *, topk, nvalid):
    rows, cols = s_ref.shape
    col = lax.broadcasted_iota(I32, (rows, cols), 1)
    valid = col < nvalid
    key_ref[...] = _score_keys(s_ref[...], valid)
    o_ref[...] = jnp.where(valid, _topk_select(key_ref, topk, col), 0.0)


def _ds_select(scores, topk, nvalid):
    rows, cols = scores.shape
    return pl.pallas_call(
        functools.partial(_ds_select_kernel, topk=topk, nvalid=nvalid),
        grid=(1,),
        in_specs=[pl.BlockSpec((rows, cols), lambda i: (0, 0))],
        out_specs=pl.BlockSpec((rows, cols), lambda i: (0, 0)),
        out_shape=jax.ShapeDtypeStruct((rows, cols), F32),
        scratch_shapes=[pltpu.VMEM((rows, cols), I32)],
        compiler_params=_cparams(1), name="ds_select",
    )(scores)


def _ds_attn_kernel(pt_ref, q_ref, kcur_ref, vcur_ref, m_ref, hm_ref, *refs, npages):
    kp, vp, o_ref = refs[:npages], refs[npages:2 * npages], refs[2 * npages]
    hm = hm_ref[...]
    qf = q_ref[...] * hm
    qb = qf.astype(BF16)
    sc = HD ** -0.5
    lg_cur = jnp.sum(qf * kcur_ref[...], axis=-1, keepdims=True) * sc
    cur_sel = m_ref[npages:npages + 1, 0:1] > 0.0
    mx = lg_cur
    lgs = []
    for p in range(npages):
        lg = jnp.dot(qb, kp[p][...].astype(BF16), preferred_element_type=F32) * sc
        lg = jnp.where(m_ref[p:p + 1, :] > 0.0, lg, -jnp.inf)
        lgs.append(lg)
        mx = jnp.maximum(mx, jnp.max(lg, axis=-1, keepdims=True))
    pc = jnp.where(cur_sel, jnp.exp(lg_cur - mx), 0.0)
    l = pc
    acc = pc * vcur_ref[...]
    for p in range(npages):
        pe = jnp.exp(lgs[p] - mx)
        l = l + jnp.sum(pe, axis=-1, keepdims=True)
        acc = acc + lax.dot_general(pe.astype(BF16), vp[p][...].astype(BF16), _LANES,
                                    preferred_element_type=F32)
    o_ref[...] = jnp.sum((acc / l) * hm, axis=0, keepdims=True)


def _ds_attn(page_table, q, kcur, vcur, mask, cache_k, cache_v, layer, headmask):
    n, npages = page_table.shape
    pspec = lambda p: pl.BlockSpec((None, None, 256, PAGE_SIZE), lambda i, pt, p=p: (layer, pt[i, p], 0, 0))
    row = pl.BlockSpec((None, 1, 256), lambda i, pt: (i, 0, 0))
    return pl.pallas_call(
        functools.partial(_ds_attn_kernel, npages=npages),
        grid_spec=pltpu.PrefetchScalarGridSpec(
            num_scalar_prefetch=1, grid=(n,),
            in_specs=[row, row, row,
                      pl.BlockSpec((None, npages + 8, 128), lambda i, pt: (i, 0, 0)),
                      pl.BlockSpec((8, 256), lambda i, pt: (0, 0))]
                     + [pspec(p) for p in range(npages)] + [pspec(p) for p in range(npages)],
            out_specs=row),
        out_shape=jax.ShapeDtypeStruct((n, 1, 256), F32),
        compiler_params=_cparams(1), name="ds_attn",
    )(page_table, q, kcur, vcur, mask, headmask, *([cache_k] * npages), *([cache_v] * npages))


def _constants(gla_keys):
    lane256 = np.arange(256)
    head = lane256 // 64
    wones = (head[:, None] == head[None, :]).astype(np.float32)
    idt = (np.arange(64)[:, None] == (lane256 % 64)[None, :]).astype(np.float32)
    e2 = ((np.arange(128) // 32)[:, None] == head[None, :]).astype(np.float32)
    bd = (head[:, None] == (np.arange(128) // 32)[None, :]).astype(np.float32)
    hm8 = (np.arange(8)[:, None] == head[None, :]).astype(np.float32)
    hm128 = (np.arange(8)[:, None] == (np.arange(128) // 32)[None, :]).astype(np.float32)

    def chunk_masks(tile, chunk, heads):
        t = np.arange(tile)
        same = (t[:, None] // chunk) == (t[None, :] // chunk)
        rows = np.arange(heads * tile)
        amask = ((rows // (heads * chunk))[:, None] == (t // chunk)[None, :]) & \
                ((t % chunk)[None, :] <= (rows % chunk)[:, None])
        return dict(tri=jnp.asarray(same & (t[None, :] <= t[:, None]), BF16), ones=jnp.asarray(same, BF16),
                    amask=jnp.asarray(amask, F32))

    gla = {key: chunk_masks(key[0], key[1], GLA_H) for key in gla_keys}
    rwm = chunk_masks(RW_SC, RW_CH, RW_H)
    i256 = np.arange(256)
    same16 = (i256[:, None] // RW_CH) == (i256[None, :] // RW_CH)
    strict = same16 & ((i256 % RW_CH)[None, :] < (i256 % RW_CH)[:, None])
    incl = same16 & ((i256 % RW_CH)[None, :] <= (i256 % RW_CH)[:, None])

    def left(width, group, half):
        return jnp.asarray(((np.arange(width) % group) < half).astype(np.float32)[None, :])

    return dict(wones=jnp.asarray(wones, BF16), wavg=jnp.asarray(wones / 64.0, BF16), idt=jnp.asarray(idt),
                e2=jnp.asarray(e2, BF16), bd=jnp.asarray(bd), hm8=jnp.asarray(hm8), hm128=jnp.asarray(hm128),
                gla=gla, rw_tri=rwm["tri"], rw_ones=rwm["ones"], rw_strict=jnp.asarray(strict, F32),
                rw_incl=jnp.asarray(incl, F32), bd256=jnp.asarray(wones, F32),
                left_q=left(256, HD, ROT // 2), left_i=left(256, IDX_D, IDX_ROT // 2),
                left_kw=left(128, 128, IDX_ROT // 2))


def _rope_tables(pos, periodic):
    pos = pos.astype(F32)[:, None]

    def build(width, group, rot, extra=None):
        half = rot // 2
        freq = ROPE_THETA ** (-jnp.arange(half, dtype=F32) * (2.0 / rot))
        ang = pos * freq
        cos, sin = jnp.cos(ang), jnp.sin(ang)
        n = pos.shape[0]
        ones = jnp.ones((n, group - rot), F32)
        zeros = jnp.zeros((n, group - rot), F32)
        cg = jnp.concatenate([cos, cos, ones], axis=1)
        sg = jnp.concatenate([-sin, sin, zeros], axis=1)
        reps = width // group
        c, s = jnp.tile(cg, (1, reps)), jnp.tile(sg, (1, reps))
        if extra is not None:
            c, s = extra(c, s)
        return c, s

    cq, sq = build(256, HD, ROT)
    ci, si = build(256, IDX_D, IDX_ROT)

    def kw_extra(c, s):
        lane = jnp.arange(128)
        scale = jnp.where((lane >= IDX_D) & (lane < IDX_D + IDX_H), IDX_H ** -0.5, 1.0)
        keep = (lane < IDX_D)
        return jnp.where(keep, c, scale[None, :]), jnp.where(keep, s, 0.0)

    ckw, skw = build(128, IDX_D, IDX_ROT, kw_extra)
    return dict(cq=cq, sq=sq, ci=ci, si=si, ckw=ckw, skw=skw, periodic=periodic)


def _blockdiag(blocks):
    g, r, c = blocks.shape
    eye = jnp.eye(g, dtype=blocks.dtype)
    return jnp.einsum('grc,gh->grhc', blocks, eye).reshape(g * r, g * c)


def _layer_weights(l, p):
    w_in = p["w_in"][l]
    z = lambda n: jnp.zeros((D_MODEL, n), F32)
    w_mix = jnp.concatenate([
        w_in[:, 256:512], w_in[:, 528:784], w_in[:, 0:256],
        w_in[:, 784:1808], w_in[:, 1848:2104], w_in[:, 2104:3000],
        w_in[:, 512:528], z(112), w_in[:, 1808:1848], z(88)], axis=1).astype(BF16)
    w = dict(w_mix=w_mix, w_gates=w_in[:, 3000:7096].astype(BF16),
             w_br=p["w_br"][l].astype(BF16), w_o=p["w_o"][l].astype(BF16),
             w_ff1=p["w_ff1"][l].astype(BF16), w_ff2=p["w_ff2"][l].astype(BF16),
             norm1_g=p["norm1_g"][l][None, :], norm2_g=p["norm2_g"][l][None, :])
    a2 = jnp.zeros((128, 128), F32).at[0:GLA_RANK].set(p["gla_a2"][l])
    w["gla_a2h"], w["gla_a2l"] = _hilo(a2)
    w["gla_ab"] = p["gla_ab"][l][None, :]
    w["gla_ng"] = jnp.tile(p["gla_ng"][l], GLA_H)[None, :]
    w["att_qg"] = jnp.tile(p["att_qg"][l], ATT_H)[None, :]
    w["att_kg"] = jnp.tile(p["att_kg"][l], ATT_H)[None, :]
    dt = jnp.exp(p["s5_log_dt"][l])[:, None]
    lr = jnp.minimum(p["s5_a_re"][l], -1e-4)
    li = p["s5_a_im"][l]
    mag = jnp.exp(lr * dt)
    abr, abi = mag * jnp.cos(li * dt), mag * jnp.sin(li * dt)
    den = lr * lr + li * li
    fr = ((abr - 1.0) * lr + abi * li) / den
    fi = (abi * lr - (abr - 1.0) * li) / den
    b_re, b_im = p["s5_b_re"][l], p["s5_b_im"][l]
    bbr = fr[..., None] * b_re - fi[..., None] * b_im
    bbi = fr[..., None] * b_im + fi[..., None] * b_re
    bmat = jnp.concatenate([_blockdiag(bbr.transpose(0, 2, 1)), _blockdiag(bbi.transpose(0, 2, 1))], axis=1)
    w["s5_bh"] = bmat.astype(BF16)
    w["s5_a"] = jnp.concatenate([abr.reshape(1, -1), abi.reshape(1, -1)], axis=1)
    w["s5_c"] = jnp.concatenate([_blockdiag(p["s5_c_re"][l].transpose(0, 2, 1)),
                                 -_blockdiag(p["s5_c_im"][l].transpose(0, 2, 1))], axis=0).astype(BF16)
    w["s5_d"] = p["s5_d"][l].reshape(1, -1)
    w["s5_gw"] = p["s5_glu_w"][l].astype(BF16)
    w["s5_gb"] = p["s5_glu_b"][l][None, :]
    w["rw_mu"] = p["rw_mu"][l][None, :]
    w["rw_w0"] = p["rw_w0"][l][None, :]
    w["rw_a0"] = p["rw_a0"][l][None, :]
    lo = jnp.zeros((128, 256), F32)
    w["rw_w2h"], w["rw_w2l"] = _hilo(lo.at[0:RW_WR].set(p["rw_w2"][l]))
    w["rw_a2h"], w["rw_a2l"] = _hilo(lo.at[RW_WR:RW_WR + RW_AR].set(p["rw_a2"][l]))
    w["rw_g2h"], w["rw_g2l"] = _hilo(lo.at[RW_WR + RW_AR:128].set(p["rw_g2"][l]))
    for nm in ("rw_kk", "rw_ka", "rw_rk", "rw_ng"):
        w[nm] = p[nm][l][None, :]
    return w


def _mix_and_ffn(x, mod, w, consts, P, o_att, st_gla0, st_s50, st_rw0, prev, batch, seq, lvalid, tm, tiles):
    o_gla, st_gla = _gla(P, st_gla0, w, consts, batch, seq, tiles["gla"], tiles["chunk"], lvalid)
    if lvalid == 1:
        o1, st1 = _s5_step(P.reshape(batch, seq, NP_COLS)[:, 0], st_s50[:, 0], w)
        o_s5 = jnp.pad(o1[:, None], ((0, 0), (0, seq - 1), (0, 0))).reshape(batch * seq, 256)
        st_s5 = st1[:, None]
    else:
        o_s5, st_s5 = _s5(P, st_s50, w, batch, seq, tiles["s5"], lvalid)
    o_rw, st_rw = _rwkv(P.reshape(batch, seq, NP_COLS), st_rw0, prev, w, consts, tiles["nb"], tiles["rw"], lvalid)
    x = _merge(x, mod, w["norm1_g"], (o_gla, o_att, o_s5, o_rw.reshape(batch * seq, 256)), w, tm, seq)
    x = _ffn(x, mod, w["norm2_g"], w, tm, seq)
    return x, st_gla, st_s5, st_rw


def _gla_state_out(st):
    n = st.shape[0]
    s = st.reshape(n, GLA_H, GLA_DV, GLA_H, GLA_DK)
    s = jnp.stack([s[:, h, :, h, :] for h in range(GLA_H)], axis=1)
    return s.transpose(0, 1, 3, 2)


def _gla_state_in(s):
    eye = jnp.eye(GLA_H, dtype=s.dtype)
    n = s.shape[0]
    return jnp.einsum('nhkv,hg->nhvgk', s, eye).reshape(n, GLA_H * GLA_DV, GLA_H * GLA_DK)


def _rw_state_out(st):
    n = st.shape[0]
    return st.reshape(n, RW_N, RW_H, RW_N).transpose(0, 2, 1, 3)


def _rw_state_in(s):
    n = s.shape[0]
    return s.transpose(0, 2, 1, 3).reshape(n, RW_N, RW_H * RW_N)


def _forward(x_prompt, x_sample, c_prompt, c_sample, cache_k, cache_v, cache_idx, state_gla,
             state_s5_re, state_s5_im, state_rwkv, state_shift, page_table, p):
    B, S, _ = x_prompt.shape
    N = x_sample.shape[0]
    depth = p["w_in"].shape[0]
    past = page_table.shape[1] * PAGE_SIZE
    tm_p = min(512, S)
    tm_s = min(128, N * SAMPLE_PAD)
    tiles_p = dict(gla=min(256, S), chunk=16, s5=min(256, S), rw=min(128, S), nb=min(4, B))
    tiles_s = dict(gla=min(128, N * SAMPLE_PAD), chunk=SAMPLE_PAD, s5=SAMPLE_PAD, rw=SAMPLE_PAD, nb=8)
    tq = min(128, S)
    consts = _constants({(t["gla"], t["chunk"]) for t in (tiles_p, tiles_s)})
    mod_all = _modulation(jnp.concatenate([c_prompt, c_sample], axis=0), p["ada_w"], p["ada_b"])
    tabs_p = _rope_tables(jnp.arange(S), True)
    tabs_s = _rope_tables(jnp.full((N * SAMPLE_PAD,), past), False)
    ck = cache_k.transpose(0, 1, 3, 4, 2).reshape(*cache_k.shape[:2], ATT_H * HD, PAGE_SIZE)
    cv = cache_v.transpose(0, 1, 3, 4, 2).reshape(*cache_v.shape[:2], ATT_H * HD, PAGE_SIZE)
    ci = cache_idx.transpose(0, 1, 3, 2)

    xp = x_prompt.reshape(B * S, D_MODEL)
    xs = jnp.pad(x_sample, ((0, 0), (0, SAMPLE_PAD - 1), (0, 0))).reshape(N * SAMPLE_PAD, D_MODEL)
    outs_p, outs_s = [], []
    for l in range(depth):
        w = _layer_weights(l, p)
        mod = mod_all[l, :B].reshape(B, 1, 6 * D_MODEL)
        P = _inproj(xp, mod, w["norm1_g"], w["w_mix"], tm_p, S)
        qn, kn, qir, kwr, qh, kh, vh, qih, kib = _dsa_prep(P, tabs_p, consts, w["att_qg"], w["att_kg"], tm_p, S)
        o_att = _dsa_attn(qh, kh, vh, qih, kwr, kib, B, S, tq)
        xp, st_gla, st_s5, st_rw = _mix_and_ffn(
            xp, mod, w, consts, P, o_att,
            jnp.zeros((B, 256, 128), F32), jnp.zeros((B, 1, 2 * S5_N), F32),
            jnp.zeros((B, RW_N, 256), F32), jnp.zeros((B, 1, RW_COLS), F32), B, S, S, tm_p, tiles_p)
        P3 = P.reshape(B, S, NP_COLS)
        outs_p.append((kn.reshape(B, S, ATT_H, HD), P3[:, :, C_DV:C_DV + 256].reshape(B, S, ATT_H, HD),
                       kwr.reshape(B, S, 128)[:, :, :IDX_D], _gla_state_out(st_gla),
                       st_s5[:, 0, :S5_N].reshape(B, S5_G, S5_P), st_s5[:, 0, S5_N:].reshape(B, S5_G, S5_P),
                       _rw_state_out(st_rw), P3[:, S - 1, C_RW:C_RW + RW_COLS]))
        mod = jnp.repeat(mod_all[l, B:], SAMPLE_PAD, axis=0)
        P = _inproj(xs, mod, w["norm1_g"], w["w_mix"], tm_s, SAMPLE_PAD)
        qn, kn, qir, kwr = _dsa_prep(P, tabs_s, consts, w["att_qg"], w["att_kg"], tm_s, SAMPLE_PAD)[:4]
        first = lambda a: a.reshape(N, SAMPLE_PAD, a.shape[-1])[:, 0]
        qn1, kn1, qir1, kwr1, P1 = first(qn), first(kn), first(qir), first(kwr), first(P)
        v1 = P1[:, C_DV:C_DV + 256]
        scores = _ds_scores(page_table, qir1.reshape(N, IDX_H, IDX_D),
                            kwr1[:, IDX_D:IDX_D + IDX_H].reshape(N, IDX_H, 1),
                            kwr1[:, :IDX_D].reshape(N, 1, IDX_D), ci, l)
        ncols = scores.shape[1] * 128
        sel = _ds_select(scores.reshape(N, ncols), min(TOPK_MAX, (past + 1) // 4), past + 1)
        o1 = _ds_attn(page_table, qn1.reshape(N, 1, 256), kn1.reshape(N, 1, 256), v1.reshape(N, 1, 256),
                      sel.reshape(N, ncols // 128, 128), ck, cv, l, consts["hm8"])
        o_att = jnp.pad(o1, ((0, 0), (0, SAMPLE_PAD - 1), (0, 0))).reshape(N * SAMPLE_PAD, 256)
        x0 = jnp.concatenate([state_s5_re[l].reshape(N, 1, S5_N), state_s5_im[l].reshape(N, 1, S5_N)], axis=2)
        xs, st_gla, st_s5, st_rw = _mix_and_ffn(
            xs, mod, w, consts, P, o_att, _gla_state_in(state_gla[l]), x0, _rw_state_in(state_rwkv[l]),
            state_shift[l].reshape(N, 1, RW_COLS), N, SAMPLE_PAD, 1, tm_s, tiles_s)
        outs_s.append((kn1.reshape(N, 1, ATT_H, HD), v1.reshape(N, 1, ATT_H, HD), kwr1[:, None, :IDX_D],
                       _gla_state_out(st_gla), st_s5[:, 0, :S5_N].reshape(N, S5_G, S5_P),
                       st_s5[:, 0, S5_N:].reshape(N, S5_G, S5_P), _rw_state_out(st_rw),
                       P1[:, C_RW:C_RW + RW_COLS]))
    yp = xp.reshape(B, S, D_MODEL)
    ys = xs.reshape(N, SAMPLE_PAD, D_MODEL)[:, 0:1]
    stack = lambda lst, i: jnp.stack([s[i] for s in lst])
    return (yp, ys) + tuple(stack(outs_p, i) for i in range(8)) + tuple(stack(outs_s, i) for i in range(8))


def kernel(x_prompt, x_sample, c_prompt, c_sample, cache_k, cache_v, cache_idx, state_gla, state_s5_re, state_s5_im, state_rwkv, state_shift, page_table, ada_w, ada_b, norm1_g, norm2_g, w_in, gla_a2, gla_ab, gla_ng, att_qg, att_kg, s5_a_re, s5_a_im, s5_log_dt, s5_b_re, s5_b_im, s5_c_re, s5_c_im, s5_d, s5_glu_w, s5_glu_b, rw_mu, rw_w0, rw_w2, rw_a0, rw_a2, rw_g2, rw_kk, rw_ka, rw_rk, rw_ng, w_br, w_o, w_ff1, w_ff2):
    p = dict(ada_w=ada_w, ada_b=ada_b, norm1_g=norm1_g, norm2_g=norm2_g, w_in=w_in, gla_a2=gla_a2,
             gla_ab=gla_ab, gla_ng=gla_ng, att_qg=att_qg, att_kg=att_kg, s5_a_re=s5_a_re, s5_a_im=s5_a_im,
             s5_log_dt=s5_log_dt, s5_b_re=s5_b_re, s5_b_im=s5_b_im, s5_c_re=s5_c_re, s5_c_im=s5_c_im,
             s5_d=s5_d, s5_glu_w=s5_glu_w, s5_glu_b=s5_glu_b, rw_mu=rw_mu, rw_w0=rw_w0, rw_w2=rw_w2,
             rw_a0=rw_a0, rw_a2=rw_a2, rw_g2=rw_g2, rw_kk=rw_kk, rw_ka=rw_ka, rw_rk=rw_rk, rw_ng=rw_ng,
             w_br=w_br, w_o=w_o, w_ff1=w_ff1, w_ff2=w_ff2)
    return _forward(x_prompt, x_sample, c_prompt, c_sample, cache_k, cache_v, cache_idx, state_gla,
                    state_s5_re, state_s5_im, state_rwkv, state_shift, page_table, p)
```

```python
import functools
import math

import numpy as np
import jax
import jax.numpy as jnp
from jax import lax
from jax.experimental import pallas as pl
from jax.experimental.pallas import tpu as pltpu

F32 = jnp.float32
BF16 = jnp.bfloat16
I32 = jnp.int32

D_MODEL = 1024
BR_W = 256
GLA_H, GLA_DK, GLA_DV, GLA_RANK, GLA_TAU = 4, 32, 64, 16, 16.0
ATT_H, HD, ROT = 4, 64, 16
IDX_H, IDX_D, IDX_ROT = 8, 32, 8
TOPK_MAX = 256
ROPE_THETA = 500000.0
S5_G, S5_P, S5_CH = 16, 64, 16
S5_N = S5_G * S5_P
RW_H, RW_N, RW_WR, RW_AR, RW_GR = 4, 64, 32, 32, 64
RW_COLS = 896
D_FF = 4096
EPS = 1e-6
RW_GN_EPS = 64e-5
PAGE_SIZE = 128
INT_MIN = -(2 ** 31)
SAMPLE_PAD = 8

C_GV, C_GR, C_GQK = 0, 256, 512
C_DQ, C_DK, C_DV, C_DQI = 768, 1024, 1280, 1536
C_S5 = 1792
C_RW = 2048
C_RWLO = 2816
C_GA = 2944
C_DKW = 3072
NP_COLS = 3200
VMEM_LIMIT = 56 * 1024 * 1024


def _cparams(n_axes):
    return pltpu.CompilerParams(dimension_semantics=("arbitrary",) * n_axes,
                                vmem_limit_bytes=VMEM_LIMIT)


def _split_dot(x, w, terms):
    acc = None
    r = x
    for i in range(terms):
        hi = r.astype(BF16)
        d = jnp.dot(hi, w, preferred_element_type=F32)
        acc = d if acc is None else acc + d
        if i + 1 < terms:
            r = r - hi.astype(F32)
    return acc


def _dot3(x, w_hi, w_lo):
    x_hi = x.astype(BF16)
    x_lo = (x - x_hi.astype(F32)).astype(BF16)
    return (jnp.dot(x_hi, w_hi, preferred_element_type=F32)
            + jnp.dot(x_hi, w_lo, preferred_element_type=F32)
            + jnp.dot(x_lo, w_hi, preferred_element_type=F32))


def _hilo(w):
    hi = w.astype(BF16)
    return hi, (w - hi.astype(F32)).astype(BF16)


def _sigmoid(x):
    return 1.0 / (1.0 + jnp.exp(-x))


def _softplus(x):
    return jnp.maximum(x, 0.0) + jnp.log1p(jnp.exp(-jnp.abs(x)))


def _norm_mod(x, g, sc, sh):
    ms = jnp.mean(x * x, axis=-1, keepdims=True)
    return (x * lax.rsqrt(ms + EPS) * g) * (1.0 + sc) + sh


def _mod_spec(mod, j, tm, seq):
    if mod.ndim == 3:
        return pl.BlockSpec((None, 1, D_MODEL), lambda i: ((i * tm) // seq, 0, j))
    return pl.BlockSpec((tm, D_MODEL), lambda i: (i, j))


def _const_spec(a):
    nd = a.ndim
    return pl.BlockSpec(a.shape, lambda *_: (0,) * nd)


def _mod_kernel(c_ref, w_ref, b_ref, o_ref):
    o_ref[...] = jnp.dot(c_ref[...], w_ref[...].astype(BF16), preferred_element_type=F32) + b_ref[...]


def _modulation(c_all, ada_w, ada_b):
    depth = ada_w.shape[0]
    rows = c_all.shape[0]
    tn = 1536
    return pl.pallas_call(
        _mod_kernel,
        grid=(depth, 6 * D_MODEL // tn),
        in_specs=[pl.BlockSpec((rows, D_MODEL), lambda l, j: (0, 0)),
                  pl.BlockSpec((None, D_MODEL, tn), lambda l, j: (l, 0, j)),
                  pl.BlockSpec((None, 1, tn), lambda l, j: (l, 0, j))],
        out_specs=pl.BlockSpec((None, rows, tn), lambda l, j: (l, 0, j)),
        out_shape=jax.ShapeDtypeStruct((depth, rows, 6 * D_MODEL), F32),
        compiler_params=_cparams(2), name="modulation",
    )(c_all.astype(BF16), ada_w, ada_b.reshape(depth, 1, 6 * D_MODEL))


def _inproj_kernel(x_ref, sc_ref, sh_ref, g_ref, w_ref, o_ref):
    h = _norm_mod(x_ref[...], g_ref[...], sc_ref[...], sh_ref[...])
    o_ref[...] = jnp.dot(h.astype(BF16), w_ref[...], preferred_element_type=F32)


def _inproj(x, mod, g, w_mix, tm, seq):
    rows = x.shape[0]
    return pl.pallas_call(
        _inproj_kernel,
        grid=(rows // tm,),
        in_specs=[pl.BlockSpec((tm, D_MODEL), lambda i: (i, 0)),
                  _mod_spec(mod, 1, tm, seq), _mod_spec(mod, 0, tm, seq),
                  _const_spec(g), _const_spec(w_mix)],
        out_specs=pl.BlockSpec((tm, NP_COLS), lambda i: (i, 0)),
        out_shape=jax.ShapeDtypeStruct((rows, NP_COLS), F32),
        compiler_params=_cparams(1), name="inproj",
    )(x, mod, mod, g, w_mix)


def _rope_apply(x, cos, sn, left, shift):
    n = x.shape[-1]
    rot = jnp.where(left > 0.0, pltpu.roll(x, n - shift, 1), pltpu.roll(x, shift, 1))
    return x * cos + rot * sn


def _dsa_prep_kernel(q_ref, k_ref, v_ref, qi_ref, kw_ref, cq_ref, sq_ref, ci_ref, si_ref, ckw_ref, skw_ref,
                     lq_ref, li_ref, lkw_ref, qg_ref, kg_ref, wavg_ref,
                     qn_ref, kn_ref, qir_ref, kwr_ref, qh_ref, kh_ref, vh_ref, qih_ref, kib_ref):
    wavg = wavg_ref[...]

    def headnorm(x, g):
        ms = _split_dot(x * x, wavg, 3)
        return x * lax.rsqrt(ms + EPS) * g

    cq, sq, lq = cq_ref[...], sq_ref[...], lq_ref[...]
    qn = _rope_apply(headnorm(q_ref[...], qg_ref[...]), cq, sq, lq, ROT // 2)
    kn = _rope_apply(headnorm(k_ref[...], kg_ref[...]), cq, sq, lq, ROT // 2)
    qir = _rope_apply(qi_ref[...], ci_ref[...], si_ref[...], li_ref[...], IDX_ROT // 2)
    kwr = _rope_apply(kw_ref[...], ckw_ref[...], skw_ref[...], lkw_ref[...], IDX_ROT // 2)
    qn_ref[...] = qn
    kn_ref[...] = kn
    qir_ref[...] = qir
    kwr_ref[...] = kwr
    v = v_ref[...]
    for h in range(ATT_H):
        hs = slice(h * HD, (h + 1) * HD)
        qh_ref[h] = qn[:, hs].astype(BF16)
        kh_ref[h] = kn[:, hs].astype(BF16)
        vh_ref[h] = v[:, hs].astype(BF16)
    for h in range(IDX_H):
        qih_ref[h] = qir[:, h * IDX_D:(h + 1) * IDX_D].astype(BF16)
    kib_ref[...] = kwr.astype(BF16)


def _dsa_prep(P, tabs, consts, qg, kg, tm, seq):
    rows = P.shape[0]
    nt = seq // tm if tabs["periodic"] else None

    def tab_spec(w):
        if tabs["periodic"]:
            return pl.BlockSpec((tm, w), lambda i: (i % nt, 0))
        return pl.BlockSpec((tm, w), lambda i: (i, 0))

    def col(off, w):
        return pl.BlockSpec((tm, w), lambda i: (i, off // w))

    out256 = jax.ShapeDtypeStruct((rows, 256), F32)
    heads = lambda n, d: (pl.BlockSpec((n, tm, d), lambda i: (0, i, 0)), jax.ShapeDtypeStruct((n, rows, d), BF16))
    hq, hi = heads(ATT_H, HD), heads(IDX_H, IDX_D)
    return pl.pallas_call(
        _dsa_prep_kernel,
        grid=(rows // tm,),
        in_specs=[col(C_DQ, 256), col(C_DK, 256), col(C_DV, 256), col(C_DQI, 256), col(C_DKW, 128),
                  tab_spec(256), tab_spec(256), tab_spec(256), tab_spec(256), tab_spec(128), tab_spec(128),
                  _const_spec(consts["left_q"]), _const_spec(consts["left_i"]), _const_spec(consts["left_kw"]),
                  _const_spec(qg), _const_spec(kg), _const_spec(consts["wavg"])],
        out_specs=[pl.BlockSpec((tm, 256), lambda i: (i, 0))] * 3 + [pl.BlockSpec((tm, 128), lambda i: (i, 0))]
                  + [hq[0], hq[0], hq[0], hi[0], pl.BlockSpec((tm, 128), lambda i: (i, 0))],
        out_shape=[out256, out256, out256, jax.ShapeDtypeStruct((rows, 128), F32),
                   hq[1], hq[1], hq[1], hi[1], jax.ShapeDtypeStruct((rows, 128), BF16)],
        compiler_params=_cparams(1), name="dsa_prep",
    )(P, P, P, P, P, tabs["cq"], tabs["sq"], tabs["ci"], tabs["si"], tabs["ckw"], tabs["skw"],
      consts["left_q"], consts["left_i"], consts["left_kw"], qg, kg, consts["wavg"])


def _score_keys(scores, valid):
    s = jnp.where(scores == 0.0, 0.0, scores)
    bits = pltpu.bitcast(s, I32)
    key = bits ^ (jnp.right_shift(bits, 31) & 0x7FFFFFFF)
    return jnp.where(valid, key, INT_MIN)


def _topk_select(key_ref, k, col, side=None, n_side=0):
    rows, cols = key_ref.shape
    kf = float(k)
    nbits = max(1, int(math.ceil(math.log2(cols))))

    ng = 4 if rows % 32 == 0 else 1
    rg = rows // ng

    def count_ge(g, c):
        return jnp.sum(jnp.where(key_ref[g * rg:(g + 1) * rg, :] >= c, 1.0, 0.0), axis=-1, keepdims=True)

    bases = tuple(jnp.where(count_ge(g, jnp.zeros((rg, 1), I32)) >= kf, 0, INT_MIN).astype(I32)
                  for g in range(ng))

    def bit_step(i, bases):
        bit = lax.shift_left(jnp.int32(1), 30 - i)
        return tuple(jnp.where(count_ge(g, b | bit) >= kf, b | bit, b) for g, b in enumerate(bases))

    if side is None:
        bases = lax.fori_loop(0, 31, bit_step, bases, unroll=4)
    else:
        per = -(-31 // n_side)

        def outer(j, bases):
            side(j)
            for t in range(per):
                s = j * per + t
                bit = jnp.where(s <= 30, lax.shift_left(jnp.int32(1), jnp.maximum(30 - s, 0)), 0)
                bases = tuple(jnp.where(count_ge(g, b | bit) >= kf, b | bit, b) for g, b in enumerate(bases))
            return bases

        bases = lax.fori_loop(0, n_side, outer, bases)
    thr = bases[0] if ng == 1 else jnp.concatenate(bases, axis=0)
    key = key_ref[...]
    need = kf - jnp.sum(jnp.where(key > thr, 1.0, 0.0), axis=-1, keepdims=True)
    excess = jnp.sum(jnp.where(key == thr, 1.0, 0.0), axis=-1, keepdims=True) > need

    def pos_step(i, pos):
        cand = pos + lax.shift_left(jnp.int32(1), nbits - 1 - i)
        hit = jnp.where(key_ref[...] == thr, jnp.where(col < cand, 1.0, 0.0), 0.0)
        return jnp.where(jnp.sum(hit, axis=-1, keepdims=True) < need, cand, pos)

    n_iter = jnp.where(jnp.max(jnp.where(excess, 1, 0)) > 0, nbits, 0)
    pos = lax.fori_loop(0, n_iter, pos_step, jnp.zeros((rows, 1), I32))
    pos = jnp.where(excess, pos, cols)
    return jnp.where(key > thr, 1.0, jnp.where(key == thr, jnp.where(col <= pos, 1.0, 0.0), 0.0))


def _dsa_attn_kernel(qh_ref, qih_ref, kwq_ref, kh_ref, vh_ref, kib_ref, o_ref, key_ref, lg_ref,
                     *, tq, topk, nvar):
    it = pl.program_id(1)
    t0 = it * tq
    seq = kh_ref.shape[1]
    step = seq // nvar
    per = step // tq

    def body(klen):
        ki = kib_ref[0:klen, 0:IDX_D]
        kwq = kwq_ref[...]
        scores = jnp.zeros((tq, klen), F32)
        for h in range(IDX_H):
            s = lax.dot_general(qih_ref[h], ki, _LANES, preferred_element_type=F32)
            w = kwq[:, IDX_D + h:IDX_D + h + 1] * (IDX_D ** -0.5)
            scores = scores + w * jnp.maximum(s, 0.0)
        col = lax.broadcasted_iota(I32, (tq, klen), 1)
        row = t0 + lax.broadcasted_iota(I32, (tq, klen), 0)
        causal = col <= row
        keys = key_ref.at[:, pl.ds(0, klen)]
        keys[...] = _score_keys(scores, causal)

        def logits(h):
            lg_ref[h, :, 0:klen] = lax.dot_general(qh_ref[h], kh_ref[h, 0:klen, :], _LANES,
                                                   preferred_element_type=F32) * (HD ** -0.5)

        sel = jnp.where(causal, _topk_select(keys, topk, col, logits, ATT_H), 0.0) > 0.0

        def head(h):
            lg = jnp.where(sel, lg_ref[h, :, 0:klen], -jnp.inf)
            m = jnp.max(lg, axis=-1, keepdims=True)
            yield
            p = jnp.exp(lg - m)
            l = jnp.sum(p, axis=-1, keepdims=True)
            o = jnp.dot(p.astype(BF16), vh_ref[h, 0:klen, :], preferred_element_type=F32)
            yield
            o_ref[:, h * HD:(h + 1) * HD] = o / l

        chains = [head(h) for h in range(ATT_H)]
        while chains:
            chains = [ch for ch in chains if next(ch, True) is None]

    for var in range(nvar):
        pl.when(it // per == var)(functools.partial(body, (var + 1) * step))


def _dsa_attn(qh, kh, vh, qih, kwr, kib, batch, seq, tq):
    topk = min(TOPK_MAX, seq // 4)
    nq = seq // tq
    nvar = min(4, nq)
    while seq // nvar < topk:
        nvar //= 2
    return pl.pallas_call(
        functools.partial(_dsa_attn_kernel, tq=tq, topk=topk, nvar=nvar),
        grid=(batch, nq),
        in_specs=[pl.BlockSpec((ATT_H, tq, HD), lambda b, i: (0, b * nq + i, 0)),
                  pl.BlockSpec((IDX_H, tq, IDX_D), lambda b, i: (0, b * nq + i, 0)),
                  pl.BlockSpec((tq, 128), lambda b, i: (b * nq + i, 0)),
                  pl.BlockSpec((ATT_H, seq, HD), lambda b, i: (0, b, 0)),
                  pl.BlockSpec((ATT_H, seq, HD), lambda b, i: (0, b, 0)),
                  pl.BlockSpec((seq, 128), lambda b, i: (b, 0))],
        out_specs=pl.BlockSpec((tq, 256), lambda b, i: (b * nq + i, 0)),
        out_shape=jax.ShapeDtypeStruct((batch * seq, 256), F32),
        scratch_shapes=[pltpu.VMEM((tq, seq), I32), pltpu.VMEM((ATT_H, tq, seq), F32)],
        compiler_params=_cparams(2), name="dsa_attn",
    )(qh, qih, kwr, kh, vh, kib)


def _split3_rhs(m, x):
    hi = x.astype(BF16)
    r = x - hi.astype(F32)
    mid = r.astype(BF16)
    lo = (r - mid.astype(F32)).astype(BF16)
    return (jnp.dot(m, hi, preferred_element_type=F32) + jnp.dot(m, mid, preferred_element_type=F32)
            + jnp.dot(m, lo, preferred_element_type=F32))


_LANES = (((1,), (1,)), ((), ()))
_ROWS = (((0,), (0,)), ((), ()))


def _gla_kernel(v_ref, r_ref, qk_ref, a_ref, st0_ref, a2h_ref, a2l_ref, ab_ref, ng_ref,
                tri_ref, ones_ref, amask_ref, hm128_ref, hm256_ref, bd_ref, wavg_ref,
                o_ref, st_ref, *, tile, chunk, lvalid, independent):
    if not independent:
        @pl.when(pl.program_id(1) == 0)
        def _():
            st_ref[...] = st0_ref[...]

    nch = tile // chunk
    z = _dot3(a_ref[...], a2h_ref[...], a2l_ref[...]) + ab_ref[...]
    la = (jnp.minimum(z, 0.0) - jnp.log1p(jnp.exp(-jnp.abs(z)))) * (1.0 / GLA_TAU)
    q = qk_ref[:, 0:128] * (GLA_DK ** -0.5)
    k = qk_ref[:, 128:256]
    v = v_ref[...]
    if independent and lvalid < chunk:
        keep = (lax.broadcasted_iota(I32, (tile, 128), 0) % chunk) < lvalid
        la = jnp.where(keep, la, 0.0)
        k = jnp.where(keep, k, 0.0)
        v = jnp.where((lax.broadcasted_iota(I32, (tile, 256), 0) % chunk) < lvalid, v, 0.0)
    b = _split3_rhs(tri_ref[...], la)
    tot = _split3_rhs(ones_ref[...], la)
    qe = q * jnp.exp(b)
    kinv = (k * jnp.exp(-b)).astype(BF16)
    kd = (k * jnp.exp(tot - b)).astype(BF16)
    vb = v.astype(BF16)
    hm128 = hm128_ref[...]
    hm256 = hm256_ref[...]
    qblk = jnp.concatenate([qe[c * chunk:(c + 1) * chunk] * hm128[h:h + 1]
                            for c in range(nch) for h in range(GLA_H)], axis=0).astype(BF16)
    att = lax.dot_general(qblk, kinv, _LANES, preferred_element_type=F32) * amask_ref[...]
    intra = jnp.dot(att.astype(BF16), vb, preferred_element_type=F32)
    qeb = qe.astype(BF16)
    st = None if independent else st_ref[...]
    for c in range(nch):
        rows = slice(c * chunk, (c + 1) * chunk)
        if independent:
            st = st0_ref[c]
        o = lax.dot_general(qeb[rows], st.astype(BF16), _LANES, preferred_element_type=F32)
        for h in range(GLA_H):
            r0 = (c * GLA_H + h) * chunk
            o = o + intra[r0:r0 + chunk] * hm256[h:h + 1]
        o_ref[rows, :] = o
        upd = lax.dot_general(vb[rows], kd[rows], _ROWS, preferred_element_type=F32)
        st = st * jnp.exp(tot[c * chunk:c * chunk + 1]) + upd * bd_ref[...]
        if independent:
            st_ref[c] = st
    if not independent:
        st_ref[...] = st
    o = o_ref[...]
    ms = _split_dot(o * o, wavg_ref[...], 2)
    r = r_ref[...]
    o_ref[...] = o * lax.rsqrt(ms + EPS) * ng_ref[...] * (r * _sigmoid(r))


def _gla(P, st0, w, consts, batch, seq, tg, chunk, lvalid):
    independent = lvalid < seq
    if independent:
        assert seq == chunk
        nseq = tg // chunk
        nt, grid = 1, (batch // nseq, 1)
        st_spec = pl.BlockSpec((nseq, 256, 128), lambda b, j: (b, 0, 0))
    else:
        nt, grid = seq // tg, (batch, seq // tg)
        st_spec = pl.BlockSpec((None, 256, 128), lambda b, j: (b, 0, 0))

    def col(off, wd):
        return pl.BlockSpec((tg, wd), lambda b, j: (b * nt + j, off // wd))

    gm = consts["gla"][(tg, chunk)]
    cs = [w["gla_a2h"], w["gla_a2l"], w["gla_ab"], w["gla_ng"],
          gm["tri"], gm["ones"], gm["amask"], consts["hm128"], consts["hm8"], consts["bd"], consts["wavg"]]
    return pl.pallas_call(
        functools.partial(_gla_kernel, tile=tg, chunk=chunk, lvalid=lvalid, independent=independent),
        grid=grid,
        in_specs=[col(C_GV, 256), col(C_GR, 256), col(C_GQK, 256), col(C_GA, 128), st_spec]
                 + [_const_spec(a) for a in cs],
        out_specs=[pl.BlockSpec((tg, 256), lambda b, j: (b * nt + j, 0)), st_spec],
        out_shape=[jax.ShapeDtypeStruct((batch * seq, 256), F32),
                   jax.ShapeDtypeStruct((batch, 256, 128), F32)],
        compiler_params=_cparams(2), name="gla",
    )(P, P, P, P, st0, *cs)


def _gelu_tanh(x):
    return 0.5 * x * (1.0 + jnp.tanh(math.sqrt(2.0 / math.pi) * (x + 0.044715 * (x * x * x))))


def _s5_kernel(u_ref, x0_ref, a_ref, bh_ref, c_ref, d_ref, gw_ref, gb_ref,
               o_ref, xf_ref, st_s, bur_s, bui_s, xr_s, xi_s, *, tile, last_row):
    @pl.when(pl.program_id(1) == 0)
    def _():
        st_s[...] = x0_ref[...]

    u = u_ref[...]
    bu = jnp.dot(u.astype(BF16), bh_ref[...], preferred_element_type=F32)
    bur_s[...] = bu[:, 0:S5_N]
    bui_s[...] = bu[:, S5_N:2 * S5_N]
    ar = a_ref[:, 0:S5_N]
    ai = a_ref[:, S5_N:2 * S5_N]

    def step(t, carry):
        xr, xi = carry
        row = pl.ds(t, 1)
        nr = ar * xr - ai * xi + bur_s[row, :]
        ni = ar * xi + ai * xr + bui_s[row, :]
        xr_s[row, :] = nr
        xi_s[row, :] = ni
        return nr, ni

    xr, xi = lax.fori_loop(0, tile, step, (st_s[:, 0:S5_N], st_s[:, S5_N:2 * S5_N]), unroll=8)
    st_s[:, 0:S5_N] = xr
    st_s[:, S5_N:2 * S5_N] = xi
    y = (jnp.dot(xr_s[...].astype(BF16), c_ref[0:S5_N, :], preferred_element_type=F32)
         + jnp.dot(xi_s[...].astype(BF16), c_ref[S5_N:2 * S5_N, :], preferred_element_type=F32)
         + d_ref[...] * u)
    z = _gelu_tanh(y)
    gate = jnp.dot(z.astype(BF16), gw_ref[...], preferred_element_type=F32) + gb_ref[...]
    o_ref[...] = z * _sigmoid(gate)
    xf_ref[:, 0:S5_N] = xr_s[last_row:last_row + 1, :]
    xf_ref[:, S5_N:2 * S5_N] = xi_s[last_row:last_row + 1, :]


def _s5(P, x0, w, batch, seq, tile, lvalid):
    nt = seq // tile
    last_row = (lvalid - 1) % tile
    cs = [w["s5_a"], w["s5_bh"], w["s5_c"], w["s5_d"], w["s5_gw"], w["s5_gb"]]
    return pl.pallas_call(
        functools.partial(_s5_kernel, tile=tile, last_row=last_row),
        grid=(batch, nt),
        in_specs=[pl.BlockSpec((tile, 256), lambda b, j: (b * nt + j, C_S5 // 256)),
                  pl.BlockSpec((None, 1, 2 * S5_N), lambda b, j: (b, 0, 0))] + [_const_spec(a) for a in cs],
        out_specs=[pl.BlockSpec((tile, 256), lambda b, j: (b * nt + j, 0)),
                   pl.BlockSpec((None, 1, 2 * S5_N), lambda b, j: (b, 0, 0))],
        out_shape=[jax.ShapeDtypeStruct((batch * seq, 256), F32),
                   jax.ShapeDtypeStruct((batch, 1, 2 * S5_N), F32)],
        scratch_shapes=[pltpu.VMEM((1, 2 * S5_N), F32)] + [pltpu.VMEM((tile, S5_N), F32)] * 4,
        compiler_params=_cparams(2), name="s5",
    )(P, x0, *cs)


def _s5_step_kernel(u_ref, x0_ref, a_ref, bh_ref, c_ref, d_ref, gw_ref, gb_ref, o_ref, xf_ref):
    u = u_ref[...]
    bu = jnp.dot(u.astype(BF16), bh_ref[...], preferred_element_type=F32)
    ar, ai = a_ref[:, 0:S5_N], a_ref[:, S5_N:2 * S5_N]
    xr0, xi0 = x0_ref[:, 0:S5_N], x0_ref[:, S5_N:2 * S5_N]
    xr = ar * xr0 - ai * xi0 + bu[:, 0:S5_N]
    xi = ar * xi0 + ai * xr0 + bu[:, S5_N:2 * S5_N]
    y = (jnp.dot(xr.astype(BF16), c_ref[0:S5_N, :], preferred_element_type=F32)
         + jnp.dot(xi.astype(BF16), c_ref[S5_N:2 * S5_N, :], preferred_element_type=F32)
         + d_ref[...] * u)
    z = _gelu_tanh(y)
    gate = jnp.dot(z.astype(BF16), gw_ref[...], preferred_element_type=F32) + gb_ref[...]
    o_ref[...] = z * _sigmoid(gate)
    xf_ref[:, 0:S5_N] = xr
    xf_ref[:, S5_N:2 * S5_N] = xi


def _s5_step(P1, x0, w):
    n = P1.shape[0]
    cs = [w["s5_a"], w["s5_bh"], w["s5_c"], w["s5_d"], w["s5_gw"], w["s5_gb"]]
    return pl.pallas_call(
        _s5_step_kernel,
        grid=(1,),
        in_specs=[pl.BlockSpec((n, 256), lambda i: (0, C_S5 // 256)),
                  pl.BlockSpec((n, 2 * S5_N), lambda i: (0, 0))] + [_const_spec(a) for a in cs],
        out_specs=[pl.BlockSpec((n, 256), lambda i: (0, 0)), pl.BlockSpec((n, 2 * S5_N), lambda i: (0, 0))],
        out_shape=[jax.ShapeDtypeStruct((n, 256), F32), jax.ShapeDtypeStruct((n, 2 * S5_N), F32)],
        compiler_params=_cparams(1), name="s5_step",
    )(P1, x0, *cs)


RW_CH = 16
RW_SC = 4 * RW_CH


def _rwkv_chunked(sf_ref, r_s, k_s, v_s, al_s, be_s, lw_s, y_s, sbd_s, hm, tri, onesb, strict, incl, bd,
                  nb, tile):
    nh = RW_H

    def blk(x):
        return jnp.concatenate([x[RW_CH * c:RW_CH * (c + 1)] * hm[h:h + 1]
                                for c in range(4) for h in range(nh)], axis=0)

    def rep(x):
        return jnp.concatenate([x[RW_CH * c:RW_CH * (c + 1)] for c in range(4) for _ in range(nh)], axis=0)

    def stack_heads(x):
        return jnp.concatenate([x[RW_CH * c:RW_CH * (c + 1), RW_N * h:RW_N * (h + 1)]
                                for c in range(4) for h in range(nh)], axis=0)

    def mm(a, b):
        return jnp.dot(a, b, preferred_element_type=F32)

    for b in range(nb):
        sbd_s[b] = sf_ref[b]

    def superchunk(sc, carry):
        r0 = pl.multiple_of(sc * RW_SC, RW_SC)
        rows = pl.ds(r0, RW_SC)

        def chain(b):
            lw = lw_s[b, rows, :]
            cum = _split3_rhs(tri, lw)
            tot = _split3_rhs(onesb, lw)
            rr, kx, vv = r_s[b, rows, :], k_s[b, rows, :], v_s[b, rows, :]
            al, be = al_s[b, rows, :], be_s[b, rows, :]
            pinv = jnp.exp(-cum)
            pend = jnp.exp(tot - cum)
            ab = al * jnp.exp(cum - lw)
            rb = rr * jnp.exp(cum)
            bt, kt, bp, kp = be * pinv, kx * pinv, be * pend, kx * pend
            ablk = blk(ab)
            lhs = jnp.concatenate([ablk, blk(rb)], axis=0).astype(BF16)
            rhs = jnp.concatenate([rep(bt), rep(kt)], axis=0).astype(BF16)
            g = lax.dot_general(lhs, rhs, _LANES, preferred_element_type=F32)
            yield
            mb = g[0:256, 0:256] * strict
            mk = g[0:256, 256:512] * strict
            myb = (g[256:512, 0:256] * incl).astype(BF16)
            myk = (g[256:512, 256:512] * incl).astype(BF16)
            vst = stack_heads(vv).astype(BF16)
            w0 = mm(mk.astype(BF16), vst)
            y0 = mm(myk, vst)
            u, mp = mb, mb
            for _ in range(3):
                mpb = mp.astype(BF16)
                mp = mm(mpb, mpb)
                yield
                u = u + mp + mm(u.astype(BF16), mp.astype(BF16))
                yield
            ub = u.astype(BF16)
            ab1 = (ablk + mm(ub, ablk.astype(BF16))).astype(BF16)
            rblk = lhs[256:512]
            z0 = w0 + mm(ub, w0.astype(BF16))
            bpk = jnp.concatenate([blk(bp), blk(kp)], axis=1).astype(BF16)
            yield
            S = sbd_s[b]
            nr = nh * RW_CH
            for c in range(4):
                d0 = nr * c
                lc = jnp.concatenate([ab1[d0:d0 + nr], rblk[d0:d0 + nr]], axis=0)
                s_hi = S.astype(BF16)
                s_lo = (S - s_hi.astype(F32)).astype(BF16)
                x = (lax.dot_general(lc, s_hi, _LANES, preferred_element_type=F32)
                     + lax.dot_general(lc, s_lo, _LANES, preferred_element_type=F32))
                yield
                zst = x[0:nr] + z0[d0:d0 + nr]
                zb = zst.astype(BF16)
                yst = x[nr:2 * nr] + y0[d0:d0 + nr] + mm(myb[d0:d0 + nr, d0:d0 + nr], zb)
                y_s[b, pl.ds(r0 + RW_CH * c, RW_CH), :] = jnp.concatenate(
                    [yst[RW_CH * h:RW_CH * (h + 1)] for h in range(nh)], axis=1)
                upd = (lax.dot_general(zb, bpk[d0:d0 + nr, 0:256], _ROWS, preferred_element_type=F32)
                       + lax.dot_general(vst[d0:d0 + nr], bpk[d0:d0 + nr, 256:512], _ROWS,
                                         preferred_element_type=F32))
                S = S * jnp.exp(tot[RW_CH * c:RW_CH * c + 1]) + upd
                yield
            sbd_s[b] = S

        chains = [chain(b) for b in range(nb)]
        while chains:
            chains = [ch for ch in chains if next(ch, True) is None]
        return carry

    lax.fori_loop(0, tile // RW_SC, superchunk, 0)
    for b in range(nb):
        sf_ref[b] = sbd_s[b]


def _rwkv_kernel(r_ref, k_ref, v_ref, lo_ref, s0_ref, prev_ref, mu_ref, w0_ref, a0_ref,
                 w2h_ref, w2l_ref, a2h_ref, a2l_ref, g2h_ref, g2l_ref, kkp_ref, ka_ref, rk_ref, ng_ref,
                 wones_ref, wavg_ref, idt_ref, hm_ref, tri_ref, ones_ref, strict_ref, incl_ref, bd_ref,
                 o_ref, sf_ref, r_s, k_s, v_s, kk_s, ka_s, w_s, y_s, g_s, bo_s, prev_s, sbd_s,
                 *, nb, tile, nsteps, chunked):
    @pl.when(pl.program_id(1) == 0)
    def _():
        sf_ref[...] = s0_ref[...]
        prev_s[...] = prev_ref[...]

    wones = wones_ref[...]
    wavg = wavg_ref[...]
    idt = idt_ref[...]
    row0_256 = lax.broadcasted_iota(I32, (tile, 256), 0) == 0
    row0_128 = lax.broadcasted_iota(I32, (tile, 128), 0) == 0

    def shift_mix(p, prev_row, mu, row0):
        sh = jnp.where(row0, prev_row, pltpu.roll(p, 1, 0))
        return p + (sh - p) * mu

    for b in range(nb):
        pr, pk, pv, plo = r_ref[b], k_ref[b], v_ref[b], lo_ref[b]
        r = shift_mix(pr, prev_s[b, :, 0:256], mu_ref[:, 0:256], row0_256)
        k = shift_mix(pk, prev_s[b, :, 256:512], mu_ref[:, 256:512], row0_256)
        v = shift_mix(pv, prev_s[b, :, 512:768], mu_ref[:, 512:768], row0_256)
        lo = shift_mix(plo, prev_s[b, :, 768:896], mu_ref[:, 768:896], row0_128)
        prev_s[b, :, 0:256] = pr[tile - 1:tile, :]
        prev_s[b, :, 256:512] = pk[tile - 1:tile, :]
        prev_s[b, :, 512:768] = pv[tile - 1:tile, :]
        prev_s[b, :, 768:896] = plo[tile - 1:tile, :]
        wl = w0_ref[...] + _dot3(jnp.tanh(lo), w2h_ref[...], w2l_ref[...])
        w = -_softplus(-wl) - 0.5
        a = _sigmoid(a0_ref[...] + _split_dot(lo, a2h_ref[...], 2))
        g = jnp.dot(_sigmoid(lo).astype(BF16), g2h_ref[...], preferred_element_type=F32)
        kk = k * kkp_ref[...]
        kk = kk * lax.rsqrt(_split_dot(kk * kk, wones, 2) + EPS)
        k2 = k * (1.0 + (a - 1.0) * ka_ref[...])
        bonus = _split_dot(r * k2 * rk_ref[...], wones, 2) * v
        r_s[b] = r
        k_s[b] = k2
        v_s[b] = v
        kk_s[b] = -kk
        ka_s[b] = kk * a
        w_s[b] = -jnp.exp(w)
        if nsteps < tile:
            y_s[b] = jnp.zeros((tile, 256), F32)
        g_s[b] = g
        bo_s[b] = bonus

    def step(t, carry):
        for b in range(nb):
            row = pl.ds(t, 1)
            S = sf_ref[b]
            sa = _split_dot(S * kk_s[b, row, :], wones, 2)
            vcol = _split_dot(idt * v_s[b, row, :], wones, 2)
            Sn = S * jnp.exp(w_s[b, row, :]) + sa * ka_s[b, row, :] + vcol * k_s[b, row, :]
            yb = jnp.dot((Sn * r_s[b, row, :]).astype(BF16), wones, preferred_element_type=F32)
            y_s[b, row, :] = jnp.sum(yb * idt, axis=0, keepdims=True)
            sf_ref[b] = Sn
        return carry

    if chunked:
        _rwkv_chunked(sf_ref, r_s, k_s, v_s, kk_s, ka_s, w_s, y_s, sbd_s, hm_ref[...], tri_ref[...],
                      ones_ref[...], strict_ref[...], incl_ref[...], bd_ref[...], nb, tile)
    else:
        lax.fori_loop(0, nsteps, step, 0)
    for b in range(nb):
        y = y_s[b]
        mu = _split_dot(y, wavg, 2)
        yc = y - mu
        var = _split_dot(yc * yc, wavg, 2)
        o_ref[b] = (yc * lax.rsqrt(var + RW_GN_EPS) * ng_ref[...] + bo_s[b]) * g_s[b]


def _rwkv(P3, s0, prev, w, consts, nb, tile, lvalid):
    batch, seq, _ = P3.shape
    nt = seq // tile
    nsteps = tile if lvalid >= seq else lvalid
    cs = [w["rw_mu"], w["rw_w0"], w["rw_a0"], w["rw_w2h"], w["rw_w2l"], w["rw_a2h"], w["rw_a2l"],
          w["rw_g2h"], w["rw_g2l"], w["rw_kk"], w["rw_ka"], w["rw_rk"], w["rw_ng"],
          consts["wones"], consts["wavg"], consts["idt"], consts["hm8"], consts["rw_tri"], consts["rw_ones"],
          consts["rw_strict"], consts["rw_incl"], consts["bd256"]]
    chunked = nsteps == tile and tile % RW_SC == 0

    def col(off, wd):
        return pl.BlockSpec((nb, tile, wd), lambda g, j: (g, j, off // wd))

    big = lambda: pltpu.VMEM((nb, tile, 256), F32)
    return pl.pallas_call(
        functools.partial(_rwkv_kernel, nb=nb, tile=tile, nsteps=nsteps, chunked=chunked),
        grid=(batch // nb, nt),
        in_specs=[col(C_RW, 256), col(C_RW + 256, 256), col(C_RW + 512, 256), col(C_RWLO, 128),
                  pl.BlockSpec((nb, RW_N, 256), lambda g, j: (g, 0, 0)),
                  pl.BlockSpec((nb, 1, RW_COLS), lambda g, j: (g, 0, 0))] + [_const_spec(a) for a in cs],
        out_specs=[pl.BlockSpec((nb, tile, 256), lambda g, j: (g, j, 0)),
                   pl.BlockSpec((nb, RW_N, 256), lambda g, j: (g, 0, 0))],
        out_shape=[jax.ShapeDtypeStruct((batch, seq, 256), F32),
                   jax.ShapeDtypeStruct((batch, RW_N, 256), F32)],
        scratch_shapes=[big(), big(), big(), big(), big(), big(), big(), big(), big(),
                        pltpu.VMEM((nb, 1, RW_COLS), F32),
                        pltpu.VMEM((nb, RW_N, 256) if chunked else (1, 8, 128), F32)],
        compiler_params=_cparams(2), name="rwkv",
    )(P3, P3, P3, P3, s0, prev, *cs)


def _merge_kernel(x_ref, sc_ref, sh_ref, gt_ref, g_ref, og_ref, oa_ref, os_ref, or_ref,
                  wg_ref, wbr_ref, wo_ref, o_ref):
    x = x_ref[...]
    h = _norm_mod(x, g_ref[...], sc_ref[...], sh_ref[...]).astype(BF16)
    merged = None
    for b, oref in enumerate((og_ref, oa_ref, os_ref, or_ref)):
        gate = _sigmoid(jnp.dot(h, wg_ref[:, b * D_MODEL:(b + 1) * D_MODEL], preferred_element_type=F32))
        proj = jnp.dot(oref[...].astype(BF16), wbr_ref[b], preferred_element_type=F32)
        merged = gate * proj if merged is None else merged + gate * proj
    y = jnp.dot(merged.astype(BF16), wo_ref[...], preferred_element_type=F32)
    o_ref[...] = x + gt_ref[...] * y


def _merge(x, mod, g, outs, w, tm, seq):
    rows = x.shape[0]
    row256 = pl.BlockSpec((tm, 256), lambda i: (i, 0))
    return pl.pallas_call(
        _merge_kernel,
        grid=(rows // tm,),
        in_specs=[pl.BlockSpec((tm, D_MODEL), lambda i: (i, 0)),
                  _mod_spec(mod, 1, tm, seq), _mod_spec(mod, 0, tm, seq), _mod_spec(mod, 2, tm, seq),
                  _const_spec(g), row256, row256, row256, row256,
                  _const_spec(w["w_gates"]), _const_spec(w["w_br"]), _const_spec(w["w_o"])],
        out_specs=pl.BlockSpec((tm, D_MODEL), lambda i: (i, 0)),
        out_shape=jax.ShapeDtypeStruct((rows, D_MODEL), F32),
        compiler_params=_cparams(1), name="merge",
    )(x, mod, mod, mod, g, *outs, w["w_gates"], w["w_br"], w["w_o"])


def _ffn_kernel(x_ref, sc_ref, sh_ref, gt_ref, g_ref, w1_ref, w2_ref, o_ref):
    x = x_ref[...]
    h = _norm_mod(x, g_ref[...], sc_ref[...], sh_ref[...]).astype(BF16)
    acc = None
    for c in range(D_FF // D_MODEL):
        cs = slice(c * D_MODEL, (c + 1) * D_MODEL)
        u = jnp.maximum(jnp.dot(h, w1_ref[:, cs], preferred_element_type=F32), 0.0)
        d = jnp.dot((u * u).astype(BF16), w2_ref[cs, :], preferred_element_type=F32)
        acc = d if acc is None else acc + d
    o_ref[...] = x + gt_ref[...] * acc


def _ffn(x, mod, g, w, tm, seq):
    rows = x.shape[0]
    return pl.pallas_call(
        _ffn_kernel,
        grid=(rows // tm,),
        in_specs=[pl.BlockSpec((tm, D_MODEL), lambda i: (i, 0)),
                  _mod_spec(mod, 4, tm, seq), _mod_spec(mod, 3, tm, seq), _mod_spec(mod, 5, tm, seq),
                  _const_spec(g), _const_spec(w["w_ff1"]), _const_spec(w["w_ff2"])],
        out_specs=pl.BlockSpec((tm, D_MODEL), lambda i: (i, 0)),
        out_shape=jax.ShapeDtypeStruct((rows, D_MODEL), F32),
        compiler_params=_cparams(1), name="ffn",
    )(x, mod, mod, mod, g, w["w_ff1"], w["w_ff2"])


def _ds_scores_kernel(pt_ref, q8_ref, w8_ref, kcur_ref, *refs, npages):
    pages, o_ref = refs[:npages], refs[npages]
    q8 = q8_ref[...]
    q8b = q8.astype(BF16)
    w8 = w8_ref[...] * (IDX_D ** -0.5)
    for p in range(npages):
        s = jnp.dot(q8b, pages[p][...].astype(BF16), preferred_element_type=F32)
        o_ref[p:p + 1, :] = jnp.sum(w8 * jnp.maximum(s, 0.0), axis=0, keepdims=True)
    s_cur = jnp.sum(q8 * kcur_ref[...], axis=-1, keepdims=True)
    i_cur = jnp.sum(w8 * jnp.maximum(s_cur, 0.0), axis=0, keepdims=True)
    lane = lax.broadcasted_iota(I32, (8, 128), 1)
    rowi = lax.broadcasted_iota(I32, (8, 128), 0)
    o_ref[npages:npages + 8, :] = jnp.where((lane == 0) & (rowi == 0), i_cur, -jnp.inf)


def _ds_scores(page_table, q8, w8, kcur, cache_idx, layer):
    n, npages = page_table.shape
    page_specs = [pl.BlockSpec((None, None, IDX_D, PAGE_SIZE), lambda i, pt, p=p: (layer, pt[i, p], 0, 0))
                  for p in range(npages)]
    return pl.pallas_call(
        functools.partial(_ds_scores_kernel, npages=npages),
        grid_spec=pltpu.PrefetchScalarGridSpec(
            num_scalar_prefetch=1, grid=(n,),
            in_specs=[pl.BlockSpec((None, IDX_H, IDX_D), lambda i, pt: (i, 0, 0)),
                      pl.BlockSpec((None, IDX_H, 1), lambda i, pt: (i, 0, 0)),
                      pl.BlockSpec((None, 1, IDX_D), lambda i, pt: (i, 0, 0))] + page_specs,
            out_specs=pl.BlockSpec((None, npages + 8, 128), lambda i, pt: (i, 0, 0))),
        out_shape=jax.ShapeDtypeStruct((n, npages + 8, 128), F32),
        compiler_params=_cparams(1), name="ds_scores",
    )(page_table, q8, w8, kcur, *([cache_idx] * npages))


def _ds_select_kernel(s_ref, o_ref, key_ref, *, topk, nvalid):
    rows, cols = s_ref.shape
    col = lax.broadcasted_iota(I32, (rows, cols), 1)
    valid = col < nvalid
    key_ref[...] = _score_keys(s_ref[...], valid)
    o_ref[...] = jnp.where(valid, _topk_select(key_ref, topk, col), 0.0)


def _ds_select(scores, topk, nvalid):
    rows, cols = scores.shape
    return pl.pallas_call(
        functools.partial(_ds_select_kernel, topk=topk, nvalid=nvalid),
        grid=(1,),
        in_specs=[pl.BlockSpec((rows, cols), lambda i: (0, 0))],
        out_specs=pl.BlockSpec((rows, cols), lambda i: (0, 0)),
        out_shape=jax.ShapeDtypeStruct((rows, cols), F32),
        scratch_shapes=[pltpu.VMEM((rows, cols), I32)],
        compiler_params=_cparams(1), name="ds_select",
    )(scores)


def _ds_attn_kernel(pt_ref, q_ref, kcur_ref, vcur_ref, m_ref, hm_ref, *refs, npages):
    kp, vp, o_ref = refs[:npages], refs[npages:2 * npages], refs[2 * npages]
    hm = hm_ref[...]
    qf = q_ref[...] * hm
    qb = qf.astype(BF16)
    sc = HD ** -0.5
    lg_cur = jnp.sum(qf * kcur_ref[...], axis=-1, keepdims=True) * sc
    cur_sel = m_ref[npages:npages + 1, 0:1] > 0.0
    mx = lg_cur
    lgs = []
    for p in range(npages):
        lg = jnp.dot(qb, kp[p][...].astype(BF16), preferred_element_type=F32) * sc
        lg = jnp.where(m_ref[p:p + 1, :] > 0.0, lg, -jnp.inf)
        lgs.append(lg)
        mx = jnp.maximum(mx, jnp.max(lg, axis=-1, keepdims=True))
    pc = jnp.where(cur_sel, jnp.exp(lg_cur - mx), 0.0)
    l = pc
    acc = pc * vcur_ref[...]
    for p in range(npages):
        pe = jnp.exp(lgs[p] - mx)
        l = l + jnp.sum(pe, axis=-1, keepdims=True)
        acc = acc + lax.dot_general(pe.astype(BF16), vp[p][...].astype(BF16), _LANES,
                                    preferred_element_type=F32)
    o_ref[...] = jnp.sum((acc / l) * hm, axis=0, keepdims=True)


def _ds_attn(page_table, q, kcur, vcur, mask, cache_k, cache_v, layer, headmask):
    n, npages = page_table.shape
    pspec = lambda p: pl.BlockSpec((None, None, 256, PAGE_SIZE), lambda i, pt, p=p: (layer, pt[i, p], 0, 0))
    row = pl.BlockSpec((None, 1, 256), lambda i, pt: (i, 0, 0))
    return pl.pallas_call(
        functools.partial(_ds_attn_kernel, npages=npages),
        grid_spec=pltpu.PrefetchScalarGridSpec(
            num_scalar_prefetch=1, grid=(n,),
            in_specs=[row, row, row,
                      pl.BlockSpec((None, npages + 8, 128), lambda i, pt: (i, 0, 0)),
                      pl.BlockSpec((8, 256), lambda i, pt: (0, 0))]
                     + [pspec(p) for p in range(npages)] + [pspec(p) for p in range(npages)],
            out_specs=row),
        out_shape=jax.ShapeDtypeStruct((n, 1, 256), F32),
        compiler_params=_cparams(1), name="ds_attn",
    )(page_table, q, kcur, vcur, mask, headmask, *([cache_k] * npages), *([cache_v] * npages))


def _constants(gla_keys):
    lane256 = np.arange(256)
    head = lane256 // 64
    wones = (head[:, None] == head[None, :]).astype(np.float32)
    idt = (np.arange(64)[:, None] == (lane256 % 64)[None, :]).astype(np.float32)
    e2 = ((np.arange(128) // 32)[:, None] == head[None, :]).astype(np.float32)
    bd = (head[:, None] == (np.arange(128) // 32)[None, :]).astype(np.float32)
    hm8 = (np.arange(8)[:, None] == head[None, :]).astype(np.float32)
    hm128 = (np.arange(8)[:, None] == (np.arange(128) // 32)[None, :]).astype(np.float32)

    def chunk_masks(tile, chunk, heads):
        t = np.arange(tile)
        same = (t[:, None] // chunk) == (t[None, :] // chunk)
        rows = np.arange(heads * tile)
        amask = ((rows // (heads * chunk))[:, None] == (t // chunk)[None, :]) & \
                ((t % chunk)[None, :] <= (rows % chunk)[:, None])
        return dict(tri=jnp.asarray(same & (t[None, :] <= t[:, None]), BF16), ones=jnp.asarray(same, BF16),
                    amask=jnp.asarray(amask, F32))

    gla = {key: chunk_masks(key[0], key[1], GLA_H) for key in gla_keys}
    rwm = chunk_masks(RW_SC, RW_CH, RW_H)
    i256 = np.arange(256)
    same16 = (i256[:, None] // RW_CH) == (i256[None, :] // RW_CH)
    strict = same16 & ((i256 % RW_CH)[None, :] < (i256 % RW_CH)[:, None])
    incl = same16 & ((i256 % RW_CH)[None, :] <= (i256 % RW_CH)[:, None])

    def left(width, group, half):
        return jnp.asarray(((np.arange(width) % group) < half).astype(np.float32)[None, :])

    return dict(wones=jnp.asarray(wones, BF16), wavg=jnp.asarray(wones / 64.0, BF16), idt=jnp.asarray(idt),
                e2=jnp.asarray(e2, BF16), bd=jnp.asarray(bd), hm8=jnp.asarray(hm8), hm128=jnp.asarray(hm128),
                gla=gla, rw_tri=rwm["tri"], rw_ones=rwm["ones"], rw_strict=jnp.asarray(strict, F32),
                rw_incl=jnp.asarray(incl, F32), bd256=jnp.asarray(wones, F32),
                left_q=left(256, HD, ROT // 2), left_i=left(256, IDX_D, IDX_ROT // 2),
                left_kw=left(128, 128, IDX_ROT // 2))


def _rope_tables(pos, periodic):
    pos = pos.astype(F32)[:, None]

    def build(width, group, rot, extra=None):
        half = rot // 2
        freq = ROPE_THETA ** (-jnp.arange(half, dtype=F32) * (2.0 / rot))
        ang = pos * freq
        cos, sin = jnp.cos(ang), jnp.sin(ang)
        n = pos.shape[0]
        ones = jnp.ones((n, group - rot), F32)
        zeros = jnp.zeros((n, group - rot), F32)
        cg = jnp.concatenate([cos, cos, ones], axis=1)
        sg = jnp.concatenate([-sin, sin, zeros], axis=1)
        reps = width // group
        c, s = jnp.tile(cg, (1, reps)), jnp.tile(sg, (1, reps))
        if extra is not None:
            c, s = extra(c, s)
        return c, s

    cq, sq = build(256, HD, ROT)
    ci, si = build(256, IDX_D, IDX_ROT)

    def kw_extra(c, s):
        lane = jnp.arange(128)
        scale = jnp.where((lane >= IDX_D) & (lane < IDX_D + IDX_H), IDX_H ** -0.5, 1.0)
        keep = (lane < IDX_D)
        return jnp.where(keep, c, scale[None, :]), jnp.where(keep, s, 0.0)

    ckw, skw = build(128, IDX_D, IDX_ROT, kw_extra)
    return dict(cq=cq, sq=sq, ci=ci, si=si, ckw=ckw, skw=skw, periodic=periodic)


def _blockdiag(blocks):
    g, r, c = blocks.shape
    eye = jnp.eye(g, dtype=blocks.dtype)
    return jnp.einsum('grc,gh->grhc', blocks, eye).reshape(g * r, g * c)


def _layer_weights(l, p):
    w_in = p["w_in"][l]
    z = lambda n: jnp.zeros((D_MODEL, n), F32)
    w_mix = jnp.concatenate([
        w_in[:, 256:512], w_in[:, 528:784], w_in[:, 0:256],
        w_in[:, 784:1808], w_in[:, 1848:2104], w_in[:, 2104:3000],
        w_in[:, 512:528], z(112), w_in[:, 1808:1848], z(88)], axis=1).astype(BF16)
    w = dict(w_mix=w_mix, w_gates=w_in[:, 3000:7096].astype(BF16),
             w_br=p["w_br"][l].astype(BF16), w_o=p["w_o"][l].astype(BF16),
             w_ff1=p["w_ff1"][l].astype(BF16), w_ff2=p["w_ff2"][l].astype(BF16),
             norm1_g=p["norm1_g"][l][None, :], norm2_g=p["norm2_g"][l][None, :])
    a2 = jnp.zeros((128, 128), F32).at[0:GLA_RANK].set(p["gla_a2"][l])
    w["gla_a2h"], w["gla_a2l"] = _hilo(a2)
    w["gla_ab"] = p["gla_ab"][l][None, :]
    w["gla_ng"] = jnp.tile(p["gla_ng"][l], GLA_H)[None, :]
    w["att_qg"] = jnp.tile(p["att_qg"][l], ATT_H)[None, :]
    w["att_kg"] = jnp.tile(p["att_kg"][l], ATT_H)[None, :]
    dt = jnp.exp(p["s5_log_dt"][l])[:, None]
    lr = jnp.minimum(p["s5_a_re"][l], -1e-4)
    li = p["s5_a_im"][l]
    mag = jnp.exp(lr * dt)
    abr, abi = mag * jnp.cos(li * dt), mag * jnp.sin(li * dt)
    den = lr * lr + li * li
    fr = ((abr - 1.0) * lr + abi * li) / den
    fi = (abi * lr - (abr - 1.0) * li) / den
    b_re, b_im = p["s5_b_re"][l], p["s5_b_im"][l]
    bbr = fr[..., None] * b_re - fi[..., None] * b_im
    bbi = fr[..., None] * b_im + fi[..., None] * b_re
    bmat = jnp.concatenate([_blockdiag(bbr.transpose(0, 2, 1)), _blockdiag(bbi.transpose(0, 2, 1))], axis=1)
    w["s5_bh"] = bmat.astype(BF16)
    w["s5_a"] = jnp.concatenate([abr.reshape(1, -1), abi.reshape(1, -1)], axis=1)
    w["s5_c"] = jnp.concatenate([_blockdiag(p["s5_c_re"][l].transpose(0, 2, 1)),
                                 -_blockdiag(p["s5_c_im"][l].transpose(0, 2, 1))], axis=0).astype(BF16)
    w["s5_d"] = p["s5_d"][l].reshape(1, -1)
    w["s5_gw"] = p["s5_glu_w"][l].astype(BF16)
    w["s5_gb"] = p["s5_glu_b"][l][None, :]
    w["rw_mu"] = p["rw_mu"][l][None, :]
    w["rw_w0"] = p["rw_w0"][l][None, :]
    w["rw_a0"] = p["rw_a0"][l][None, :]
    lo = jnp.zeros((128, 256), F32)
    w["rw_w2h"], w["rw_w2l"] = _hilo(lo.at[0:RW_WR].set(p["rw_w2"][l]))
    w["rw_a2h"], w["rw_a2l"] = _hilo(lo.at[RW_WR:RW_WR + RW_AR].set(p["rw_a2"][l]))
    w["rw_g2h"], w["rw_g2l"] = _hilo(lo.at[RW_WR + RW_AR:128].set(p["rw_g2"][l]))
    for nm in ("rw_kk", "rw_ka", "rw_rk", "rw_ng"):
        w[nm] = p[nm][l][None, :]
    return w


def _mix_and_ffn(x, mod, w, consts, P, o_att, st_gla0, st_s50, st_rw0, prev, batch, seq, lvalid, tm, tiles):
    o_gla, st_gla = _gla(P, st_gla0, w, consts, batch, seq, tiles["gla"], tiles["chunk"], lvalid)
    if lvalid == 1:
        o1, st1 = _s5_step(P.reshape(batch, seq, NP_COLS)[:, 0], st_s50[:, 0], w)
        o_s5 = jnp.pad(o1[:, None], ((0, 0), (0, seq - 1), (0, 0))).reshape(batch * seq, 256)
        st_s5 = st1[:, None]
    else:
        o_s5, st_s5 = _s5(P, st_s50, w, batch, seq, tiles["s5"], lvalid)
    o_rw, st_rw = _rwkv(P.reshape(batch, seq, NP_COLS), st_rw0, prev, w, consts, tiles["nb"], tiles["rw"], lvalid)
    x = _merge(x, mod, w["norm1_g"], (o_gla, o_att, o_s5, o_rw.reshape(batch * seq, 256)), w, tm, seq)
    x = _ffn(x, mod, w["norm2_g"], w, tm, seq)
    return x, st_gla, st_s5, st_rw


def _gla_state_out(st):
    n = st.shape[0]
    s = st.reshape(n, GLA_H, GLA_DV, GLA_H, GLA_DK)
    s = jnp.stack([s[:, h, :, h, :] for h in range(GLA_H)], axis=1)
    return s.transpose(0, 1, 3, 2)


def _gla_state_in(s):
    eye = jnp.eye(GLA_H, dtype=s.dtype)
    n = s.shape[0]
    return jnp.einsum('nhkv,hg->nhvgk', s, eye).reshape(n, GLA_H * GLA_DV, GLA_H * GLA_DK)


def _rw_state_out(st):
    n = st.shape[0]
    return st.reshape(n, RW_N, RW_H, RW_N).transpose(0, 2, 1, 3)


def _rw_state_in(s):
    n = s.shape[0]
    return s.transpose(0, 2, 1, 3).reshape(n, RW_N, RW_H * RW_N)


def _forward(x_prompt, x_sample, c_prompt, c_sample, cache_k, cache_v, cache_idx, state_gla,
             state_s5_re, state_s5_im, state_rwkv, state_shift, page_table, p):
    B, S, _ = x_prompt.shape
    N = x_sample.shape[0]
    depth = p["w_in"].shape[0]
    past = page_table.shape[1] * PAGE_SIZE
    tm_p = min(512, S)
    tm_s = min(128, N * SAMPLE_PAD)
    tiles_p = dict(gla=min(256, S), chunk=16, s5=min(256, S), rw=min(128, S), nb=min(4, B))
    tiles_s = dict(gla=min(128, N * SAMPLE_PAD), chunk=SAMPLE_PAD, s5=SAMPLE_PAD, rw=SAMPLE_PAD, nb=8)
    tq = min(128, S)
    consts = _constants({(t["gla"], t["chunk"]) for t in (tiles_p, tiles_s)})
    mod_all = _modulation(jnp.concatenate([c_prompt, c_sample], axis=0), p["ada_w"], p["ada_b"])
    tabs_p = _rope_tables(jnp.arange(S), True)
    tabs_s = _rope_tables(jnp.full((N * SAMPLE_PAD,), past), False)
    ck = cache_k.transpose(0, 1, 3, 4, 2).reshape(*cache_k.shape[:2], ATT_H * HD, PAGE_SIZE)
    cv = cache_v.transpose(0, 1, 3, 4, 2).reshape(*cache_v.shape[:2], ATT_H * HD, PAGE_SIZE)
    ci = cache_idx.transpose(0, 1, 3, 2)

    xp = x_prompt.reshape(B * S, D_MODEL)
    xs = jnp.pad(x_sample, ((0, 0), (0, SAMPLE_PAD - 1), (0, 0))).reshape(N * SAMPLE_PAD, D_MODEL)
    outs_p, outs_s = [], []
    for l in range(depth):
        w = _layer_weights(l, p)
        mod = mod_all[l, :B].reshape(B, 1, 6 * D_MODEL)
        P = _inproj(xp, mod, w["norm1_g"], w["w_mix"], tm_p, S)
        qn, kn, qir, kwr, qh, kh, vh, qih, kib = _dsa_prep(P, tabs_p, consts, w["att_qg"], w["att_kg"], tm_p, S)
        o_att = _dsa_attn(qh, kh, vh, qih, kwr, kib, B, S, tq)
        xp, st_gla, st_s5, st_rw = _mix_and_ffn(
            xp, mod, w, consts, P, o_att,
            jnp.zeros((B, 256, 128), F32), jnp.zeros((B, 1, 2 * S5_N), F32),
            jnp.zeros((B, RW_N, 256), F32), jnp.zeros((B, 1, RW_COLS), F32), B, S, S, tm_p, tiles_p)
        P3 = P.reshape(B, S, NP_COLS)
        outs_p.append((kn.reshape(B, S, ATT_H, HD), P3[:, :, C_DV:C_DV + 256].reshape(B, S, ATT_H, HD),
                       kwr.reshape(B, S, 128)[:, :, :IDX_D], _gla_state_out(st_gla),
                       st_s5[:, 0, :S5_N].reshape(B, S5_G, S5_P), st_s5[:, 0, S5_N:].reshape(B, S5_G, S5_P),
                       _rw_state_out(st_rw), P3[:, S - 1, C_RW:C_RW + RW_COLS]))
        mod = jnp.repeat(mod_all[l, B:], SAMPLE_PAD, axis=0)
        P = _inproj(xs, mod, w["norm1_g"], w["w_mix"], tm_s, SAMPLE_PAD)
        qn, kn, qir, kwr = _dsa_prep(P, tabs_s, consts, w["att_qg"], w["att_kg"], tm_s, SAMPLE_PAD)[:4]
        first = lambda a: a.reshape(N, SAMPLE_PAD, a.shape[-1])[:, 0]
        qn1, kn1, qir1, kwr1, P1 = first(qn), first(kn), first(qir), first(kwr), first(P)
        v1 = P1[:, C_DV:C_DV + 256]
        scores = _ds_scores(page_table, qir1.reshape(N, IDX_H, IDX_D),
                            kwr1[:, IDX_D:IDX_D + IDX_H].reshape(N, IDX_H, 1),
                            kwr1[:, :IDX_D].reshape(N, 1, IDX_D), ci, l)
        ncols = scores.shape[1] * 128
        sel = _ds_select(scores.reshape(N, ncols), min(TOPK_MAX, (past + 1) // 4), past + 1)
        o1 = _ds_attn(page_table, qn1.reshape(N, 1, 256), kn1.reshape(N, 1, 256), v1.reshape(N, 1, 256),
                      sel.reshape(N, ncols // 128, 128), ck, cv, l, consts["hm8"])
        o_att = jnp.pad(o1, ((0, 0), (0, SAMPLE_PAD - 1), (0, 0))).reshape(N * SAMPLE_PAD, 256)
        x0 = jnp.concatenate([state_s5_re[l].reshape(N, 1, S5_N), state_s5_im[l].reshape(N, 1, S5_N)], axis=2)
        xs, st_gla, st_s5, st_rw = _mix_and_ffn(
            xs, mod, w, consts, P, o_att, _gla_state_in(state_gla[l]), x0, _rw_state_in(state_rwkv[l]),
            state_shift[l].reshape(N, 1, RW_COLS), N, SAMPLE_PAD, 1, tm_s, tiles_s)
        outs_s.append((kn1.reshape(N, 1, ATT_H, HD), v1.reshape(N, 1, ATT_H, HD), kwr1[:, None, :IDX_D],
                       _gla_state_out(st_gla), st_s5[:, 0, :S5_N].reshape(N, S5_G, S5_P),
                       st_s5[:, 0, S5_N:].reshape(N, S5_G, S5_P), _rw_state_out(st_rw),
                       P1[:, C_RW:C_RW + RW_COLS]))
    yp = xp.reshape(B, S, D_MODEL)
    ys = xs.reshape(N, SAMPLE_PAD, D_MODEL)[:, 0:1]
    stack = lambda lst, i: jnp.stack([s[i] for s in lst])
    return (yp, ys) + tuple(stack(outs_p, i) for i in range(8)) + tuple(stack(outs_s, i) for i in range(8))


def kernel(x_prompt, x_sample, c_prompt, c_sample, cache_k, cache_v, cache_idx, state_gla, state_s5_re, state_s5_im, state_rwkv, state_shift, page_table, ada_w, ada_b, norm1_g, norm2_g, w_in, gla_a2, gla_ab, gla_ng, att_qg, att_kg, s5_a_re, s5_a_im, s5_log_dt, s5_b_re, s5_b_im, s5_c_re, s5_c_im, s5_d, s5_glu_w, s5_glu_b, rw_mu, rw_w0, rw_w2, rw_a0, rw_a2, rw_g2, rw_kk, rw_ka, rw_rk, rw_ng, w_br, w_o, w_ff1, w_ff2):
    p = dict(ada_w=ada_w, ada_b=ada_b, norm1_g=norm1_g, norm2_g=norm2_g, w_in=w_in, gla_a2=gla_a2,
             gla_ab=gla_ab, gla_ng=gla_ng, att_qg=att_qg, att_kg=att_kg, s5_a_re=s5_a_re, s5_a_im=s5_a_im,
             s5_log_dt=s5_log_dt, s5_b_re=s5_b_re, s5_b_im=s5_b_im, s5_c_re=s5_c_re, s5_c_im=s5_c_im,
             s5_d=s5_d, s5_glu_w=s5_glu_w, s5_glu_b=s5_glu_b, rw_mu=rw_mu, rw_w0=rw_w0, rw_w2=rw_w2,
             rw_a0=rw_a0, rw_a2=rw_a2, rw_g2=rw_g2, rw_kk=rw_kk, rw_ka=rw_ka, rw_rk=rw_rk, rw_ng=rw_ng,
             w_br=w_br, w_o=w_o, w_ff1=w_ff1, w_ff2=w_ff2)
    return _forward(x_prompt, x_sample, c_prompt, c_sample, cache_k, cache_v, cache_idx, state_gla,
                    state_s5_re, state_s5_im, state_rwkv, state_shift, page_table, p)
```

```python
import functools
import math

import numpy as np
import jax
import jax.numpy as jnp
from jax import lax
from jax.experimental import pallas as pl
from jax.experimental.pallas import tpu as pltpu

F32 = jnp.float32
BF16 = jnp.bfloat16
I32 = jnp.int32

D_MODEL = 1024
BR_W = 256
GLA_H, GLA_DK, GLA_DV, GLA_RANK, GLA_TAU = 4, 32, 64, 16, 16.0
ATT_H, HD, ROT = 4, 64, 16
IDX_H, IDX_D, IDX_ROT = 8, 32, 8
TOPK_MAX = 256
ROPE_THETA = 500000.0
S5_G, S5_P, S5_CH = 16, 64, 16
S5_N = S5_G * S5_P
RW_H, RW_N, RW_WR, RW_AR, RW_GR = 4, 64, 32, 32, 64
RW_COLS = 896
D_FF = 4096
EPS = 1e-6
RW_GN_EPS = 64e-5
PAGE_SIZE = 128
INT_MIN = -(2 ** 31)
SAMPLE_PAD = 8

C_GV, C_GR, C_GQK = 0, 256, 512
C_DQ, C_DK, C_DV, C_DQI = 768, 1024, 1280, 1536
C_S5 = 1792
C_RW = 2048
C_RWLO = 2816
C_GA = 2944
C_DKW = 3072
NP_COLS = 3200
VMEM_LIMIT = 56 * 1024 * 1024


def _cparams(n_axes):
    return pltpu.CompilerParams(dimension_semantics=("arbitrary",) * n_axes,
                                vmem_limit_bytes=VMEM_LIMIT)


def _split_dot(x, w, terms):
    acc = None
    r = x
    for i in range(terms):
        hi = r.astype(BF16)
        d = jnp.dot(hi, w, preferred_element_type=F32)
        acc = d if acc is None else acc + d
        if i + 1 < terms:
            r = r - hi.astype(F32)
    return acc


def _dot3(x, w_hi, w_lo):
    x_hi = x.astype(BF16)
    x_lo = (x - x_hi.astype(F32)).astype(BF16)
    return (jnp.dot(x_hi, w_hi, preferred_element_type=F32)
            + jnp.dot(x_hi, w_lo, preferred_element_type=F32)
            + jnp.dot(x_lo, w_hi, preferred_element_type=F32))


def _hilo(w):
    hi = w.astype(BF16)
    return hi, (w - hi.astype(F32)).astype(BF16)


def _sigmoid(x):
    return 1.0 / (1.0 + jnp.exp(-x))


def _softplus(x):
    return jnp.maximum(x, 0.0) + jnp.log1p(jnp.exp(-jnp.abs(x)))


def _norm_mod(x, g, sc, sh):
    ms = jnp.mean(x * x, axis=-1, keepdims=True)
    return (x * lax.rsqrt(ms + EPS) * g) * (1.0 + sc) + sh


def _mod_spec(mod, j, tm, seq):
    if mod.ndim == 3:
        return pl.BlockSpec((None, 1, D_MODEL), lambda i: ((i * tm) // seq, 0, j))
    return pl.BlockSpec((tm, D_MODEL), lambda i: (i, j))


def _const_spec(a):
    nd = a.ndim
    return pl.BlockSpec(a.shape, lambda *_: (0,) * nd)


def _mod_kernel(c_ref, w_ref, b_ref, o_ref):
    o_ref[...] = jnp.dot(c_ref[...], w_ref[...].astype(BF16), preferred_element_type=F32) + b_ref[...]


def _modulation(c_all, ada_w, ada_b):
    depth = ada_w.shape[0]
    rows = c_all.shape[0]
    tn = 1536
    return pl.pallas_call(
        _mod_kernel,
        grid=(depth, 6 * D_MODEL // tn),
        in_specs=[pl.BlockSpec((rows, D_MODEL), lambda l, j: (0, 0)),
                  pl.BlockSpec((None, D_MODEL, tn), lambda l, j: (l, 0, j)),
                  pl.BlockSpec((None, 1, tn), lambda l, j: (l, 0, j))],
        out_specs=pl.BlockSpec((None, rows, tn), lambda l, j: (l, 0, j)),
        out_shape=jax.ShapeDtypeStruct((depth, rows, 6 * D_MODEL), F32),
        compiler_params=_cparams(2), name="modulation",
    )(c_all.astype(BF16), ada_w, ada_b.reshape(depth, 1, 6 * D_MODEL))


def _inproj_kernel(x_ref, sc_ref, sh_ref, g_ref, w_ref, o_ref):
    h = _norm_mod(x_ref[...], g_ref[...], sc_ref[...], sh_ref[...])
    o_ref[...] = jnp.dot(h.astype(BF16), w_ref[...], preferred_element_type=F32)


def _inproj(x, mod, g, w_mix, tm, seq):
    rows = x.shape[0]
    return pl.pallas_call(
        _inproj_kernel,
        grid=(rows // tm,),
        in_specs=[pl.BlockSpec((tm, D_MODEL), lambda i: (i, 0)),
                  _mod_spec(mod, 1, tm, seq), _mod_spec(mod, 0, tm, seq),
                  _const_spec(g), _const_spec(w_mix)],
        out_specs=pl.BlockSpec((tm, NP_COLS), lambda i: (i, 0)),
        out_shape=jax.ShapeDtypeStruct((rows, NP_COLS), F32),
        compiler_params=_cparams(1), name="inproj",
    )(x, mod, mod, g, w_mix)


def _rope_apply(x, cos, sn, left, shift):
    n = x.shape[-1]
    rot = jnp.where(left > 0.0, pltpu.roll(x, n - shift, 1), pltpu.roll(x, shift, 1))
    return x * cos + rot * sn


def _dsa_prep_kernel(q_ref, k_ref, v_ref, qi_ref, kw_ref, cq_ref, sq_ref, ci_ref, si_ref, ckw_ref, skw_ref,
                     lq_ref, li_ref, lkw_ref, qg_ref, kg_ref, wavg_ref,
                     qn_ref, kn_ref, qir_ref, kwr_ref, qh_ref, kh_ref, vh_ref, qih_ref, kib_ref):
    wavg = wavg_ref[...]

    def headnorm(x, g):
        ms = _split_dot(x * x, wavg, 3)
        return x * lax.rsqrt(ms + EPS) * g

    cq, sq, lq = cq_ref[...], sq_ref[...], lq_ref[...]
    qn = _rope_apply(headnorm(q_ref[...], qg_ref[...]), cq, sq, lq, ROT // 2)
    kn = _rope_apply(headnorm(k_ref[...], kg_ref[...]), cq, sq, lq, ROT // 2)
    qir = _rope_apply(qi_ref[...], ci_ref[...], si_ref[...], li_ref[...], IDX_ROT // 2)
    kwr = _rope_apply(kw_ref[...], ckw_ref[...], skw_ref[...], lkw_ref[...], IDX_ROT // 2)
    qn_ref[...] = qn
    kn_ref[...] = kn
    qir_ref[...] = qir
    kwr_ref[...] = kwr
    v = v_ref[...]
    for h in range(ATT_H):
        hs = slice(h * HD, (h + 1) * HD)
        qh_ref[h] = qn[:, hs].astype(BF16)
        kh_ref[h] = kn[:, hs].astype(BF16)
        vh_ref[h] = v[:, hs].astype(BF16)
    for h in range(IDX_H):
        qih_ref[h] = qir[:, h * IDX_D:(h + 1) * IDX_D].astype(BF16)
    kib_ref[...] = kwr.astype(BF16)


def _dsa_prep(P, tabs, consts, qg, kg, tm, seq):
    rows = P.shape[0]
    nt = seq // tm if tabs["periodic"] else None

    def tab_spec(w):
        if tabs["periodic"]:
            return pl.BlockSpec((tm, w), lambda i: (i % nt, 0))
        return pl.BlockSpec((tm, w), lambda i: (i, 0))

    def col(off, w):
        return pl.BlockSpec((tm, w), lambda i: (i, off // w))

    out256 = jax.ShapeDtypeStruct((rows, 256), F32)
    heads = lambda n, d: (pl.BlockSpec((n, tm, d), lambda i: (0, i, 0)), jax.ShapeDtypeStruct((n, rows, d), BF16))
    hq, hi = heads(ATT_H, HD), heads(IDX_H, IDX_D)
    return pl.pallas_call(
        _dsa_prep_kernel,
        grid=(rows // tm,),
        in_specs=[col(C_DQ, 256), col(C_DK, 256), col(C_DV, 256), col(C_DQI, 256), col(C_DKW, 128),
                  tab_spec(256), tab_spec(256), tab_spec(256), tab_spec(256), tab_spec(128), tab_spec(128),
                  _const_spec(consts["left_q"]), _const_spec(consts["left_i"]), _const_spec(consts["left_kw"]),
                  _const_spec(qg), _const_spec(kg), _const_spec(consts["wavg"])],
        out_specs=[pl.BlockSpec((tm, 256), lambda i: (i, 0))] * 3 + [pl.BlockSpec((tm, 128), lambda i: (i, 0))]
                  + [hq[0], hq[0], hq[0], hi[0], pl.BlockSpec((tm, 128), lambda i: (i, 0))],
        out_shape=[out256, out256, out256, jax.ShapeDtypeStruct((rows, 128), F32),
                   hq[1], hq[1], hq[1], hi[1], jax.ShapeDtypeStruct((rows, 128), BF16)],
        compiler_params=_cparams(1), name="dsa_prep",
    )(P, P, P, P, P, tabs["cq"], tabs["sq"], tabs["ci"], tabs["si"], tabs["ckw"], tabs["skw"],
      consts["left_q"], consts["left_i"], consts["left_kw"], qg, kg, consts["wavg"])


def _score_keys(scores, valid):
    s = jnp.where(scores == 0.0, 0.0, scores)
    bits = pltpu.bitcast(s, I32)
    key = bits ^ (jnp.right_shift(bits, 31) & 0x7FFFFFFF)
    return jnp.where(valid, key, INT_MIN)


def _topk_select(key_ref, k, col, side=None, n_side=0):
    rows, cols = key_ref.shape
    kf = float(k)
    nbits = max(1, int(math.ceil(math.log2(cols))))

    ng = 4 if rows % 32 == 0 else 1
    rg = rows // ng

    def count_ge(g, c):
        return jnp.sum(jnp.where(key_ref[g * rg:(g + 1) * rg, :] >= c, 1.0, 0.0), axis=-1, keepdims=True)

    bases = tuple(jnp.where(count_ge(g, jnp.zeros((rg, 1), I32)) >= kf, 0, INT_MIN).astype(I32)
                  for g in range(ng))

    def bit_step(i, bases):
        bit = lax.shift_left(jnp.int32(1), 30 - i)
        return tuple(jnp.where(count_ge(g, b | bit) >= kf, b | bit, b) for g, b in enumerate(bases))

    if side is None:
        bases = lax.fori_loop(0, 31, bit_step, bases, unroll=4)
    else:
        per = -(-31 // n_side)

        def outer(j, bases):
            side(j)
            for t in range(per):
                s = j * per + t
                bit = jnp.where(s <= 30, lax.shift_left(jnp.int32(1), jnp.maximum(30 - s, 0)), 0)
                bases = tuple(jnp.where(count_ge(g, b | bit) >= kf, b | bit, b) for g, b in enumerate(bases))
            return bases

        bases = lax.fori_loop(0, n_side, outer, bases)
    thr = bases[0] if ng == 1 else jnp.concatenate(bases, axis=0)
    key = key_ref[...]
    need = kf - jnp.sum(jnp.where(key > thr, 1.0, 0.0), axis=-1, keepdims=True)
    excess = jnp.sum(jnp.where(key == thr, 1.0, 0.0), axis=-1, keepdims=True) > need

    def pos_step(i, pos):
        cand = pos + lax.shift_left(jnp.int32(1), nbits - 1 - i)
        hit = jnp.where(key_ref[...] == thr, jnp.where(col < cand, 1.0, 0.0), 0.0)
        return jnp.where(jnp.sum(hit, axis=-1, keepdims=True) < need, cand, pos)

    n_iter = jnp.where(jnp.max(jnp.where(excess, 1, 0)) > 0, nbits, 0)
    pos = lax.fori_loop(0, n_iter, pos_step, jnp.zeros((rows, 1), I32))
    pos = jnp.where(excess, pos, cols)
    return jnp.where(key > thr, 1.0, jnp.where(key == thr, jnp.where(col <= pos, 1.0, 0.0), 0.0))


def _dsa_attn_kernel(qh_ref, qih_ref, kwq_ref, kh_ref, vh_ref, kib_ref, o_ref, key_ref, lg_ref,
                     *, tq, topk, nvar):
    it = pl.program_id(1)
    t0 = it * tq
    seq = kh_ref.shape[1]
    step = seq // nvar
    per = step // tq

    def body(klen):
        ki = kib_ref[0:klen, 0:IDX_D]
        kwq = kwq_ref[...]
        scores = jnp.zeros((tq, klen), F32)
        for h in range(IDX_H):
            s = lax.dot_general(qih_ref[h], ki, _LANES, preferred_element_type=F32)
            w = kwq[:, IDX_D + h:IDX_D + h + 1] * (IDX_D ** -0.5)
            scores = scores + w * jnp.maximum(s, 0.0)
        col = lax.broadcasted_iota(I32, (tq, klen), 1)
        row = t0 + lax.broadcasted_iota(I32, (tq, klen), 0)
        causal = col <= row
        keys = key_ref.at[:, pl.ds(0, klen)]
        keys[...] = _score_keys(scores, causal)

        def logits(h):
            lg_ref[h, :, 0:klen] = lax.dot_general(qh_ref[h], kh_ref[h, 0:klen, :], _LANES,
                                                   preferred_element_type=F32) * (HD ** -0.5)

        sel = jnp.where(causal, _topk_select(keys, topk, col, logits, ATT_H), 0.0) > 0.0

        def head(h):
            lg = jnp.where(sel, lg_ref[h, :, 0:klen], -jnp.inf)
            m = jnp.max(lg, axis=-1, keepdims=True)
            yield
            p = jnp.exp(lg - m)
            l = jnp.sum(p, axis=-1, keepdims=True)
            o = jnp.dot(p.astype(BF16), vh_ref[h, 0:klen, :], preferred_element_type=F32)
            yield
            o_ref[:, h * HD:(h + 1) * HD] = o / l

        chains = [head(h) for h in range(ATT_H)]
        while chains:
            chains = [ch for ch in chains if next(ch, True) is None]

    for var in range(nvar):
        pl.when(it // per == var)(functools.partial(body, (var + 1) * step))


def _dsa_attn(qh, kh, vh, qih, kwr, kib, batch, seq, tq):
    topk = min(TOPK_MAX, seq // 4)
    nq = seq // tq
    nvar = min(4, nq)
    while seq // nvar < topk:
        nvar //= 2
    return pl.pallas_call(
        functools.partial(_dsa_attn_kernel, tq=tq, topk=topk, nvar=nvar),
        grid=(batch, nq),
        in_specs=[pl.BlockSpec((ATT_H, tq, HD), lambda b, i: (0, b * nq + i, 0)),
                  pl.BlockSpec((IDX_H, tq, IDX_D), lambda b, i: (0, b * nq + i, 0)),
                  pl.BlockSpec((tq, 128), lambda b, i: (b * nq + i, 0)),
                  pl.BlockSpec((ATT_H, seq, HD), lambda b, i: (0, b, 0)),
                  pl.BlockSpec((ATT_H, seq, HD), lambda b, i: (0, b, 0)),
                  pl.BlockSpec((seq, 128), lambda b, i: (b, 0))],
        out_specs=pl.BlockSpec((tq, 256), lambda b, i: (b * nq + i, 0)),
        out_shape=jax.ShapeDtypeStruct((batch * seq, 256), F32),
        scratch_shapes=[pltpu.VMEM((tq, seq), I32), pltpu.VMEM((ATT_H, tq, seq), F32)],
        compiler_params=_cparams(2), name="dsa_attn",
    )(qh, qih, kwr, kh, vh, kib)


def _split3_rhs(m, x):
    hi = x.astype(BF16)
    r = x - hi.astype(F32)
    mid = r.astype(BF16)
    lo = (r - mid.astype(F32)).astype(BF16)
    return (jnp.dot(m, hi, preferred_element_type=F32) + jnp.dot(m, mid, preferred_element_type=F32)
            + jnp.dot(m, lo, preferred_element_type=F32))


_LANES = (((1,), (1,)), ((), ()))
_ROWS = (((0,), (0,)), ((), ()))


def _gla_kernel(v_ref, r_ref, qk_ref, a_ref, st0_ref, a2h_ref, a2l_ref, ab_ref, ng_ref,
                tri_ref, ones_ref, amask_ref, hm128_ref, hm256_ref, bd_ref, wavg_ref,
                o_ref, st_ref, *, tile, chunk, lvalid, independent):
    if not independent:
        @pl.when(pl.program_id(1) == 0)
        def _():
            st_ref[...] = st0_ref[...]

    nch = tile // chunk
    z = _dot3(a_ref[...], a2h_ref[...], a2l_ref[...]) + ab_ref[...]
    la = (jnp.minimum(z, 0.0) - jnp.log1p(jnp.exp(-jnp.abs(z)))) * (1.0 / GLA_TAU)
    q = qk_ref[:, 0:128] * (GLA_DK ** -0.5)
    k = qk_ref[:, 128:256]
    v = v_ref[...]
    if independent and lvalid < chunk:
        keep = (lax.broadcasted_iota(I32, (tile, 128), 0) % chunk) < lvalid
        la = jnp.where(keep, la, 0.0)
        k = jnp.where(keep, k, 0.0)
        v = jnp.where((lax.broadcasted_iota(I32, (tile, 256), 0) % chunk) < lvalid, v, 0.0)
    b = _split3_rhs(tri_ref[...], la)
    tot = _split3_rhs(ones_ref[...], la)
    qe = q * jnp.exp(b)
    kinv = (k * jnp.exp(-b)).astype(BF16)
    kd = (k * jnp.exp(tot - b)).astype(BF16)
    vb = v.astype(BF16)
    hm128 = hm128_ref[...]
    hm256 = hm256_ref[...]
    qblk = jnp.concatenate([qe[c * chunk:(c + 1) * chunk] * hm128[h:h + 1]
                            for c in range(nch) for h in range(GLA_H)], axis=0).astype(BF16)
    att = lax.dot_general(qblk, kinv, _LANES, preferred_element_type=F32) * amask_ref[...]
    intra = jnp.dot(att.astype(BF16), vb, preferred_element_type=F32)
    qeb = qe.astype(BF16)
    st = None if independent else st_ref[...]
    for c in range(nch):
        rows = slice(c * chunk, (c + 1) * chunk)
        if independent:
            st = st0_ref[c]
        o = lax.dot_general(qeb[rows], st.astype(BF16), _LANES, preferred_element_type=F32)
        for h in range(GLA_H):
            r0 = (c * GLA_H + h) * chunk
            o = o + intra[r0:r0 + chunk] * hm256[h:h + 1]
        o_ref[rows, :] = o
        upd = lax.dot_general(vb[rows], kd[rows], _ROWS, preferred_element_type=F32)
        st = st * jnp.exp(tot[c * chunk:c * chunk + 1]) + upd * bd_ref[...]
        if independent:
            st_ref[c] = st
    if not independent:
        st_ref[...] = st
    o = o_ref[...]
    ms = _split_dot(o * o, wavg_ref[...], 2)
    r = r_ref[...]
    o_ref[...] = o * lax.rsqrt(ms + EPS) * ng_ref[...] * (r * _sigmoid(r))


def _gla(P, st0, w, consts, batch, seq, tg, chunk, lvalid):
    independent = lvalid < seq
    if independent:
        assert seq == chunk
        nseq = tg // chunk
        nt, grid = 1, (batch // nseq, 1)
        st_spec = pl.BlockSpec((nseq, 256, 128), lambda b, j: (b, 0, 0))
    else:
        nt, grid = seq // tg, (batch, seq // tg)
        st_spec = pl.BlockSpec((None, 256, 128), lambda b, j: (b, 0, 0))

    def col(off, wd):
        return pl.BlockSpec((tg, wd), lambda b, j: (b * nt + j, off // wd))

    gm = consts["gla"][(tg, chunk)]
    cs = [w["gla_a2h"], w["gla_a2l"], w["gla_ab"], w["gla_ng"],
          gm["tri"], gm["ones"], gm["amask"], consts["hm128"], consts["hm8"], consts["bd"], consts["wavg"]]
    return pl.pallas_call(
        functools.partial(_gla_kernel, tile=tg, chunk=chunk, lvalid=lvalid, independent=independent),
        grid=grid,
        in_specs=[col(C_GV, 256), col(C_GR, 256), col(C_GQK, 256), col(C_GA, 128), st_spec]
                 + [_const_spec(a) for a in cs],
        out_specs=[pl.BlockSpec((tg, 256), lambda b, j: (b * nt + j, 0)), st_spec],
        out_shape=[jax.ShapeDtypeStruct((batch * seq, 256), F32),
                   jax.ShapeDtypeStruct((batch, 256, 128), F32)],
        compiler_params=_cparams(2), name="gla",
    )(P, P, P, P, st0, *cs)


def _gelu_tanh(x):
    return 0.5 * x * (1.0 + jnp.tanh(math.sqrt(2.0 / math.pi) * (x + 0.044715 * (x * x * x))))


def _s5_kernel(u_ref, x0_ref, a_ref, bh_ref, c_ref, d_ref, gw_ref, gb_ref,
               o_ref, xf_ref, st_s, bur_s, bui_s, xr_s, xi_s, *, tile, last_row):
    @pl.when(pl.program_id(1) == 0)
    def _():
        st_s[...] = x0_ref[...]

    u = u_ref[...]
    bu = jnp.dot(u.astype(BF16), bh_ref[...], preferred_element_type=F32)
    bur_s[...] = bu[:, 0:S5_N]
    bui_s[...] = bu[:, S5_N:2 * S5_N]
    ar = a_ref[:, 0:S5_N]
    ai = a_ref[:, S5_N:2 * S5_N]

    def step(t, carry):
        xr, xi = carry
        row = pl.ds(t, 1)
        nr = ar * xr - ai * xi + bur_s[row, :]
        ni = ar * xi + ai * xr + bui_s[row, :]
        xr_s[row, :] = nr
        xi_s[row, :] = ni
        return nr, ni

    xr, xi = lax.fori_loop(0, tile, step, (st_s[:, 0:S5_N], st_s[:, S5_N:2 * S5_N]), unroll=8)
    st_s[:, 0:S5_N] = xr
    st_s[:, S5_N:2 * S5_N] = xi
    y = (jnp.dot(xr_s[...].astype(BF16), c_ref[0:S5_N, :], preferred_element_type=F32)
         + jnp.dot(xi_s[...].astype(BF16), c_ref[S5_N:2 * S5_N, :], preferred_element_type=F32)
         + d_ref[...] * u)
    z = _gelu_tanh(y)
    gate = jnp.dot(z.astype(BF16), gw_ref[...], preferred_element_type=F32) + gb_ref[...]
    o_ref[...] = z * _sigmoid(gate)
    xf_ref[:, 0:S5_N] = xr_s[last_row:last_row + 1, :]
    xf_ref[:, S5_N:2 * S5_N] = xi_s[last_row:last_row + 1, :]


def _s5(P, x0, w, batch, seq, tile, lvalid):
    nt = seq // tile
    last_row = (lvalid - 1) % tile
    cs = [w["s5_a"], w["s5_bh"], w["s5_c"], w["s5_d"], w["s5_gw"], w["s5_gb"]]
    return pl.pallas_call(
        functools.partial(_s5_kernel, tile=tile, last_row=last_row),
        grid=(batch, nt),
        in_specs=[pl.BlockSpec((tile, 256), lambda b, j: (b * nt + j, C_S5 // 256)),
                  pl.BlockSpec((None, 1, 2 * S5_N), lambda b, j: (b, 0, 0))] + [_const_spec(a) for a in cs],
        out_specs=[pl.BlockSpec((tile, 256), lambda b, j: (b * nt + j, 0)),
                   pl.BlockSpec((None, 1, 2 * S5_N), lambda b, j: (b, 0, 0))],
        out_shape=[jax.ShapeDtypeStruct((batch * seq, 256), F32),
                   jax.ShapeDtypeStruct((batch, 1, 2 * S5_N), F32)],
        scratch_shapes=[pltpu.VMEM((1, 2 * S5_N), F32)] + [pltpu.VMEM((tile, S5_N), F32)] * 4,
        compiler_params=_cparams(2), name="s5",
    )(P, x0, *cs)


def _s5_step_kernel(u_ref, x0_ref, a_ref, bh_ref, c_ref, d_ref, gw_ref, gb_ref, o_ref, xf_ref):
    u = u_ref[...]
    bu = jnp.dot(u.astype(BF16), bh_ref[...], preferred_element_type=F32)
    ar, ai = a_ref[:, 0:S5_N], a_ref[:, S5_N:2 * S5_N]
    xr0, xi0 = x0_ref[:, 0:S5_N], x0_ref[:, S5_N:2 * S5_N]
    xr = ar * xr0 - ai * xi0 + bu[:, 0:S5_N]
    xi = ar * xi0 + ai * xr0 + bu[:, S5_N:2 * S5_N]
    y = (jnp.dot(xr.astype(BF16), c_ref[0:S5_N, :], preferred_element_type=F32)
         + jnp.dot(xi.astype(BF16), c_ref[S5_N:2 * S5_N, :], preferred_element_type=F32)
         + d_ref[...] * u)
    z = _gelu_tanh(y)
    gate = jnp.dot(z.astype(BF16), gw_ref[...], preferred_element_type=F32) + gb_ref[...]
    o_ref[...] = z * _sigmoid(gate)
    xf_ref[:, 0:S5_N] = xr
    xf_ref[:, S5_N:2 * S5_N] = xi


def _s5_step(P1, x0, w):
    n = P1.shape[0]
    cs = [w["s5_a"], w["s5_bh"], w["s5_c"], w["s5_d"], w["s5_gw"], w["s5_gb"]]
    return pl.pallas_call(
        _s5_step_kernel,
        grid=(1,),
        in_specs=[pl.BlockSpec((n, 256), lambda i: (0, C_S5 // 256)),
                  pl.BlockSpec((n, 2 * S5_N), lambda i: (0, 0))] + [_const_spec(a) for a in cs],
        out_specs=[pl.BlockSpec((n, 256), lambda i: (0, 0)), pl.BlockSpec((n, 2 * S5_N), lambda i: (0, 0))],
        out_shape=[jax.ShapeDtypeStruct((n, 256), F32), jax.ShapeDtypeStruct((n, 2 * S5_N), F32)],
        compiler_params=_cparams(1), name="s5_step",
    )(P1, x0, *cs)


RW_CH = 16
RW_SC = 4 * RW_CH


def _rwkv_chunked(sf_ref, r_s, k_s, v_s, al_s, be_s, lw_s, y_s, sbd_s, hm, tri, onesb, strict, incl, bd,
                  nb, tile):
    nh = RW_H

    def blk(x):
        return jnp.concatenate([x[RW_CH * c:RW_CH * (c + 1)] * hm[h:h + 1]
                                for c in range(4) for h in range(nh)], axis=0)

    def rep(x):
        return jnp.concatenate([x[RW_CH * c:RW_CH * (c + 1)] for c in range(4) for _ in range(nh)], axis=0)

    def stack_heads(x):
        return jnp.concatenate([x[RW_CH * c:RW_CH * (c + 1), RW_N * h:RW_N * (h + 1)]
                                for c in range(4) for h in range(nh)], axis=0)

    def mm(a, b):
        return jnp.dot(a, b, preferred_element_type=F32)

    for b in range(nb):
        sbd_s[b] = sf_ref[b]

    def superchunk(sc, carry):
        r0 = pl.multiple_of(sc * RW_SC, RW_SC)
        rows = pl.ds(r0, RW_SC)

        def chain(b):
            lw = lw_s[b, rows, :]
            cum = _split3_rhs(tri, lw)
            tot = _split3_rhs(onesb, lw)
            rr, kx, vv = r_s[b, rows, :], k_s[b, rows, :], v_s[b, rows, :]
            al, be = al_s[b, rows, :], be_s[b, rows, :]
            pinv = jnp.exp(-cum)
            pend = jnp.exp(tot - cum)
            ab = al * jnp.exp(cum - lw)
            rb = rr * jnp.exp(cum)
            bt, kt, bp, kp = be * pinv, kx * pinv, be * pend, kx * pend
            ablk = blk(ab)
            lhs = jnp.concatenate([ablk, blk(rb)], axis=0).astype(BF16)
            rhs = jnp.concatenate([rep(bt), rep(kt)], axis=0).astype(BF16)
            g = lax.dot_general(lhs, rhs, _LANES, preferred_element_type=F32)
            yield
            mb = g[0:256, 0:256] * strict
            mk = g[0:256, 256:512] * strict
            myb = (g[256:512, 0:256] * incl).astype(BF16)
            myk = (g[256:512, 256:512] * incl).astype(BF16)
            vst = stack_heads(vv).astype(BF16)
            w0 = mm(mk.astype(BF16), vst)
            y0 = mm(myk, vst)
            u, mp = mb, mb
            for _ in range(3):
                mpb = mp.astype(BF16)
                mp = mm(mpb, mpb)
                yield
                u = u + mp + mm(u.astype(BF16), mp.astype(BF16))
                yield
            ub = u.astype(BF16)
            ab1 = (ablk + mm(ub, ablk.astype(BF16))).astype(BF16)
            rblk = lhs[256:512]
            z0 = w0 + mm(ub, w0.astype(BF16))
            bpk = jnp.concatenate([blk(bp), blk(kp)], axis=1).astype(BF16)
            yield
            S = sbd_s[b]
            nr = nh * RW_CH
            for c in range(4):
                d0 = nr * c
                lc = jnp.concatenate([ab1[d0:d0 + nr], rblk[d0:d0 + nr]], axis=0)
                s_hi = S.astype(BF16)
                s_lo = (S - s_hi.astype(F32)).astype(BF16)
                x = (lax.dot_general(lc, s_hi, _LANES, preferred_element_type=F32)
                     + lax.dot_general(lc, s_lo, _LANES, preferred_element_type=F32))
                yield
                zst = x[0:nr] + z0[d0:d0 + nr]
                zb = zst.astype(BF16)
                yst = x[nr:2 * nr] + y0[d0:d0 + nr] + mm(myb[d0:d0 + nr, d0:d0 + nr], zb)
                y_s[b, pl.ds(r0 + RW_CH * c, RW_CH), :] = jnp.concatenate(
                    [yst[RW_CH * h:RW_CH * (h + 1)] for h in range(nh)], axis=1)
                upd = (lax.dot_general(zb, bpk[d0:d0 + nr, 0:256], _ROWS, preferred_element_type=F32)
                       + lax.dot_general(vst[d0:d0 + nr], bpk[d0:d0 + nr, 256:512], _ROWS,
                                         preferred_element_type=F32))
                S = S * jnp.exp(tot[RW_CH * c:RW_CH * c + 1]) + upd
                yield
            sbd_s[b] = S

        chains = [chain(b) for b in range(nb)]
        while chains:
            chains = [ch for ch in chains if next(ch, True) is None]
        return carry

    lax.fori_loop(0, tile // RW_SC, superchunk, 0)
    for b in range(nb):
        sf_ref[b] = sbd_s[b]


def _rwkv_kernel(r_ref, k_ref, v_ref, lo_ref, s0_ref, prev_ref, mu_ref, w0_ref, a0_ref,
                 w2h_ref, w2l_ref, a2h_ref, a2l_ref, g2h_ref, g2l_ref, kkp_ref, ka_ref, rk_ref, ng_ref,
                 wones_ref, wavg_ref, idt_ref, hm_ref, tri_ref, ones_ref, strict_ref, incl_ref, bd_ref,
                 o_ref, sf_ref, r_s, k_s, v_s, kk_s, ka_s, w_s, y_s, g_s, bo_s, prev_s, sbd_s,
                 *, nb, tile, nsteps, chunked):
    @pl.when(pl.program_id(1) == 0)
    def _():
        sf_ref[...] = s0_ref[...]
        prev_s[...] = prev_ref[...]

    wones = wones_ref[...]
    wavg = wavg_ref[...]
    idt = idt_ref[...]
    row0_256 = lax.broadcasted_iota(I32, (tile, 256), 0) == 0
    row0_128 = lax.broadcasted_iota(I32, (tile, 128), 0) == 0

    def shift_mix(p, prev_row, mu, row0):
        sh = jnp.where(row0, prev_row, pltpu.roll(p, 1, 0))
        return p + (sh - p) * mu

    for b in range(nb):
        pr, pk, pv, plo = r_ref[b], k_ref[b], v_ref[b], lo_ref[b]
        r = shift_mix(pr, prev_s[b, :, 0:256], mu_ref[:, 0:256], row0_256)
        k = shift_mix(pk, prev_s[b, :, 256:512], mu_ref[:, 256:512], row0_256)
        v = shift_mix(pv, prev_s[b, :, 512:768], mu_ref[:, 512:768], row0_256)
        lo = shift_mix(plo, prev_s[b, :, 768:896], mu_ref[:, 768:896], row0_128)
        prev_s[b, :, 0:256] = pr[tile - 1:tile, :]
        prev_s[b, :, 256:512] = pk[tile - 1:tile, :]
        prev_s[b, :, 512:768] = pv[tile - 1:tile, :]
        prev_s[b, :, 768:896] = plo[tile - 1:tile, :]
        wl = w0_ref[...] + _dot3(jnp.tanh(lo), w2h_ref[...], w2l_ref[...])
        w = -_softplus(-wl) - 0.5
        a = _sigmoid(a0_ref[...] + _split_dot(lo, a2h_ref[...], 2))
        g = jnp.dot(_sigmoid(lo).astype(BF16), g2h_ref[...], preferred_element_type=F32)
        kk = k * kkp_ref[...]
        kk = kk * lax.rsqrt(_split_dot(kk * kk, wones, 2) + EPS)
        k2 = k * (1.0 + (a - 1.0) * ka_ref[...])
        bonus = _split_dot(r * k2 * rk_ref[...], wones, 2) * v
        r_s[b] = r
        k_s[b] = k2
        v_s[b] = v
        kk_s[b] = -kk
        ka_s[b] = kk * a
        w_s[b] = -jnp.exp(w)
        if nsteps < tile:
            y_s[b] = jnp.zeros((tile, 256), F32)
        g_s[b] = g
        bo_s[b] = bonus

    def step(t, carry):
        for b in range(nb):
            row = pl.ds(t, 1)
            S = sf_ref[b]
            sa = _split_dot(S * kk_s[b, row, :], wones, 2)
            vcol = _split_dot(idt * v_s[b, row, :], wones, 2)
            Sn = S * jnp.exp(w_s[b, row, :]) + sa * ka_s[b, row, :] + vcol * k_s[b, row, :]
            yb = jnp.dot((Sn * r_s[b, row, :]).astype(BF16), wones, preferred_element_type=F32)
            y_s[b, row, :] = jnp.sum(yb * idt, axis=0, keepdims=True)
            sf_ref[b] = Sn
        return carry

    if chunked:
        _rwkv_chunked(sf_ref, r_s, k_s, v_s, kk_s, ka_s, w_s, y_s, sbd_s, hm_ref[...], tri_ref[...],
                      ones_ref[...], strict_ref[...], incl_ref[...], bd_ref[...], nb, tile)
    else:
        lax.fori_loop(0, nsteps, step, 0)
    for b in range(nb):
        y = y_s[b]
        mu = _split_dot(y, wavg, 2)
        yc = y - mu
        var = _split_dot(yc * yc, wavg, 2)
        o_ref[b] = (yc * lax.rsqrt(var + RW_GN_EPS) * ng_ref[...] + bo_s[b]) * g_s[b]


def _rwkv(P3, s0, prev, w, consts, nb, tile, lvalid):
    batch, seq, _ = P3.shape
    nt = seq // tile
    nsteps = tile if lvalid >= seq else lvalid
    cs = [w["rw_mu"], w["rw_w0"], w["rw_a0"], w["rw_w2h"], w["rw_w2l"], w["rw_a2h"], w["rw_a2l"],
          w["rw_g2h"], w["rw_g2l"], w["rw_kk"], w["rw_ka"], w["rw_rk"], w["rw_ng"],
          consts["wones"], consts["wavg"], consts["idt"], consts["hm8"], consts["rw_tri"], consts["rw_ones"],
          consts["rw_strict"], consts["rw_incl"], consts["bd256"]]
    chunked = nsteps == tile and tile % RW_SC == 0

    def col(off, wd):
        return pl.BlockSpec((nb, tile, wd), lambda g, j: (g, j, off // wd))

    big = lambda: pltpu.VMEM((nb, tile, 256), F32)
    return pl.pallas_call(
        functools.partial(_rwkv_kernel, nb=nb, tile=tile, nsteps=nsteps, chunked=chunked),
        grid=(batch // nb, nt),
        in_specs=[col(C_RW, 256), col(C_RW + 256, 256), col(C_RW + 512, 256), col(C_RWLO, 128),
                  pl.BlockSpec((nb, RW_N, 256), lambda g, j: (g, 0, 0)),
                  pl.BlockSpec((nb, 1, RW_COLS), lambda g, j: (g, 0, 0))] + [_const_spec(a) for a in cs],
        out_specs=[pl.BlockSpec((nb, tile, 256), lambda g, j: (g, j, 0)),
                   pl.BlockSpec((nb, RW_N, 256), lambda g, j: (g, 0, 0))],
        out_shape=[jax.ShapeDtypeStruct((batch, seq, 256), F32),
                   jax.ShapeDtypeStruct((batch, RW_N, 256), F32)],
        scratch_shapes=[big(), big(), big(), big(), big(), big(), big(), big(), big(),
                        pltpu.VMEM((nb, 1, RW_COLS), F32),
                        pltpu.VMEM((nb, RW_N, 256) if chunked else (1, 8, 128), F32)],
        compiler_params=_cparams(2), name="rwkv",
    )(P3, P3, P3, P3, s0, prev, *cs)


def _merge_kernel(x_ref, sc_ref, sh_ref, gt_ref, g_ref, og_ref, oa_ref, os_ref, or_ref,
                  wg_ref, wbr_ref, wo_ref, o_ref):
    x = x_ref[...]
    h = _norm_mod(x, g_ref[...], sc_ref[...], sh_ref[...]).astype(BF16)
    merged = None
    for b, oref in enumerate((og_ref, oa_ref, os_ref, or_ref)):
        gate = _sigmoid(jnp.dot(h, wg_ref[:, b * D_MODEL:(b + 1) * D_MODEL], preferred_element_type=F32))
        proj = jnp.dot(oref[...].astype(BF16), wbr_ref[b], preferred_element_type=F32)
        merged = gate * proj if merged is None else merged + gate * proj
    y = jnp.dot(merged.astype(BF16), wo_ref[...], preferred_element_type=F32)
    o_ref[...] = x + gt_ref[...] * y


def _merge(x, mod, g, outs, w, tm, seq):
    rows = x.shape[0]
    row256 = pl.BlockSpec((tm, 256), lambda i: (i, 0))
    return pl.pallas_call(
        _merge_kernel,
        grid=(rows // tm,),
        in_specs=[pl.BlockSpec((tm, D_MODEL), lambda i: (i, 0)),
                  _mod_spec(mod, 1, tm, seq), _mod_spec(mod, 0, tm, seq), _mod_spec(mod, 2, tm, seq),
                  _const_spec(g), row256, row256, row256, row256,
                  _const_spec(w["w_gates"]), _const_spec(w["w_br"]), _const_spec(w["w_o"])],
        out_specs=pl.BlockSpec((tm, D_MODEL), lambda i: (i, 0)),
        out_shape=jax.ShapeDtypeStruct((rows, D_MODEL), F32),
        compiler_params=_cparams(1), name="merge",
    )(x, mod, mod, mod, g, *outs, w["w_gates"], w["w_br"], w["w_o"])


def _ffn_kernel(x_ref, sc_ref, sh_ref, gt_ref, g_ref, w1_ref, w2_ref, o_ref):
    x = x_ref[...]
    h = _norm_mod(x, g_ref[...], sc_ref[...], sh_ref[...]).astype(BF16)
    acc = None
    for c in range(D_FF // D_MODEL):
        cs = slice(c * D_MODEL, (c + 1) * D_MODEL)
        u = jnp.maximum(jnp.dot(h, w1_ref[:, cs], preferred_element_type=F32), 0.0)
        d = jnp.dot((u * u).astype(BF16), w2_ref[cs, :], preferred_element_type=F32)
        acc = d if acc is None else acc + d
    o_ref[...] = x + gt_ref[...] * acc


def _ffn(x, mod, g, w, tm, seq):
    rows = x.shape[0]
    return pl.pallas_call(
        _ffn_kernel,
        grid=(rows // tm,),
        in_specs=[pl.BlockSpec((tm, D_MODEL), lambda i: (i, 0)),
                  _mod_spec(mod, 4, tm, seq), _mod_spec(mod, 3, tm, seq), _mod_spec(mod, 5, tm, seq),
                  _const_spec(g), _const_spec(w["w_ff1"]), _const_spec(w["w_ff2"])],
        out_specs=pl.BlockSpec((tm, D_MODEL), lambda i: (i, 0)),
        out_shape=jax.ShapeDtypeStruct((rows, D_MODEL), F32),
        compiler_params=_cparams(1), name="ffn",
    )(x, mod, mod, mod, g, w["w_ff1"], w["w_ff2"])


def _ds_scores_kernel(pt_ref, q8_ref, w8_ref, kcur_ref, *refs, npages):
    pages, o_ref = refs[:npages], refs[npages]
    q8 = q8_ref[...]
    q8b = q8.astype(BF16)
    w8 = w8_ref[...] * (IDX_D ** -0.5)
    for p in range(npages):
        s = jnp.dot(q8b, pages[p][...].astype(BF16), preferred_element_type=F32)
        o_ref[p:p + 1, :] = jnp.sum(w8 * jnp.maximum(s, 0.0), axis=0, keepdims=True)
    s_cur = jnp.sum(q8 * kcur_ref[...], axis=-1, keepdims=True)
    i_cur = jnp.sum(w8 * jnp.maximum(s_cur, 0.0), axis=0, keepdims=True)
    lane = lax.broadcasted_iota(I32, (8, 128), 1)
    rowi = lax.broadcasted_iota(I32, (8, 128), 0)
    o_ref[npages:npages + 8, :] = jnp.where((lane == 0) & (rowi == 0), i_cur, -jnp.inf)


def _ds_scores(page_table, q8, w8, kcur, cache_idx, layer):
    n, npages = page_table.shape
    page_specs = [pl.BlockSpec((None, None, IDX_D, PAGE_SIZE), lambda i, pt, p=p: (layer, pt[i, p], 0, 0))
                  for p in range(npages)]
    return pl.pallas_call(
        functools.partial(_ds_scores_kernel, npages=npages),
        grid_spec=pltpu.PrefetchScalarGridSpec(
            num_scalar_prefetch=1, grid=(n,),
            in_specs=[pl.BlockSpec((None, IDX_H, IDX_D), lambda i, pt: (i, 0, 0)),
                      pl.BlockSpec((None, IDX_H, 1), lambda i, pt: (i, 0, 0)),
                      pl.BlockSpec((None, 1, IDX_D), lambda i, pt: (i, 0, 0))] + page_specs,
            out_specs=pl.BlockSpec((None, npages + 8, 128), lambda i, pt: (i, 0, 0))),
        out_shape=jax.ShapeDtypeStruct((n, npages + 8, 128), F32),
        compiler_params=_cparams(1), name="ds_scores",
    )(page_table, q8, w8, kcur, *([cache_idx] * npages))


def _ds_select_kernel(s_ref, o_ref, key_ref, *, topk, nvalid):
    rows, cols = s_ref.shape
    col = lax.broadcasted_iota(I32, (rows, cols), 1)
    valid = col < nvalid
    key_ref[...] = _score_keys(s_ref[...], valid)
    o_ref[...] = jnp.where(valid, _topk_select(key_ref, topk, col), 0.0)


def _ds_select(scores, topk, nvalid):
    rows, cols = scores.shape
    return pl.pallas_call(
        functools.partial(_ds_select_kernel, topk=topk, nvalid=nvalid),
        grid=(1,),
        in_specs=[pl.BlockSpec((rows, cols), lambda i: (0, 0))],
        out_specs=pl.BlockSpec((rows, cols), lambda i: (0, 0)),
        out_shape=jax.ShapeDtypeStruct((rows, cols), F32),
        scratch_shapes=[pltpu.VMEM((rows, cols), I32)],
        compiler_params=_cparams(1), name="ds_select",
    )(scores)


def _ds_attn_kernel(pt_ref, q_ref, kcur_ref, vcur_ref, m_ref, hm_ref, *refs, npages):
    kp, vp, o_ref = refs[:npages], refs[npages:2 * npages], refs[2 * npages]
    hm = hm_ref[...]
    qf = q_ref[...] * hm
    qb = qf.astype(BF16)
    sc = HD ** -0.5
    lg_cur = jnp.sum(qf * kcur_ref[...], axis=-1, keepdims=True) * sc
    cur_sel = m_ref[npages:npages + 1, 0:1] > 0.0
    mx = lg_cur
    lgs = []
    for p in range(npages):
        lg = jnp.dot(qb, kp[p][...].astype(BF16), preferred_element_type=F32) * sc
        lg = jnp.where(m_ref[p:p + 1, :] > 0.0, lg, -jnp.inf)
        lgs.append(lg)
        mx = jnp.maximum(mx, jnp.max(lg, axis=-1, keepdims=True))
    pc = jnp.where(cur_sel, jnp.exp(lg_cur - mx), 0.0)
    l = pc
    acc = pc * vcur_ref[...]
    for p in range(npages):
        pe = jnp.exp(lgs[p] - mx)
        l = l + jnp.sum(pe, axis=-1, keepdims=True)
        acc = acc + lax.dot_general(pe.astype(BF16), vp[p][...].astype(BF16), _LANES,
                                    preferred_element_type=F32)
    o_ref[...] = jnp.sum((acc / l) * hm, axis=0, keepdims=True)


def _ds_attn(page_table, q, kcur, vcur, mask, cache_k, cache_v, layer, headmask):
    n, npages = page_table.shape
    pspec = lambda p: pl.BlockSpec((None, None, 256, PAGE_SIZE), lambda i, pt, p=p: (layer, pt[i, p], 0, 0))
    row = pl.BlockSpec((None, 1, 256), lambda i, pt: (i, 0, 0))
    return pl.pallas_call(
        functools.partial(_ds_attn_kernel, npages=npages),
        grid_spec=pltpu.PrefetchScalarGridSpec(
            num_scalar_prefetch=1, grid=(n,),
            in_specs=[row, row, row,
                      pl.BlockSpec((None, npages + 8, 128), lambda i, pt: (i, 0, 0)),
                      pl.BlockSpec((8, 256), lambda i, pt: (0, 0))]
                     + [pspec(p) for p in range(npages)] + [pspec(p) for p in range(npages)],
            out_specs=row),
        out_shape=jax.ShapeDtypeStruct((n, 1, 256), F32),
        compiler_params=_cparams(1), name="ds_attn",
    )(page_table, q, kcur, vcur, mask, headmask, *([cache_k] * npages), *([cache_v] * npages))


def _constants(gla_keys):
    lane256 = np.arange(256)
    head = lane256 // 64
    wones = (head[:, None] == head[None, :]).astype(np.float32)
    idt = (np.arange(64)[:, None] == (lane256 % 64)[None, :]).astype(np.float32)
    e2 = ((np.arange(128) // 32)[:, None] == head[None, :]).astype(np.float32)
    bd = (head[:, None] == (np.arange(128) // 32)[None, :]).astype(np.float32)
    hm8 = (np.arange(8)[:, None] == head[None, :]).astype(np.float32)
    hm128 = (np.arange(8)[:, None] == (np.arange(128) // 32)[None, :]).astype(np.float32)

    def chunk_masks(tile, chunk, heads):
        t = np.arange(tile)
        same = (t[:, None] // chunk) == (t[None, :] // chunk)
        rows = np.arange(heads * tile)
        amask = ((rows // (heads * chunk))[:, None] == (t // chunk)[None, :]) & \
                ((t % chunk)[None, :] <= (rows % chunk)[:, None])
        return dict(tri=jnp.asarray(same & (t[None, :] <= t[:, None]), BF16), ones=jnp.asarray(same, BF16),
                    amask=jnp.asarray(amask, F32))

    gla = {key: chunk_masks(key[0], key[1], GLA_H) for key in gla_keys}
    rwm = chunk_masks(RW_SC, RW_CH, RW_H)
    i256 = np.arange(256)
    same16 = (i256[:, None] // RW_CH) == (i256[None, :] // RW_CH)
    strict = same16 & ((i256 % RW_CH)[None, :] < (i256 % RW_CH)[:, None])
    incl = same16 & ((i256 % RW_CH)[None, :] <= (i256 % RW_CH)[:, None])

    def left(width, group, half):
        return jnp.asarray(((np.arange(width) % group) < half).astype(np.float32)[None, :])

    return dict(wones=jnp.asarray(wones, BF16), wavg=jnp.asarray(wones / 64.0, BF16), idt=jnp.asarray(idt),
                e2=jnp.asarray(e2, BF16), bd=jnp.asarray(bd), hm8=jnp.asarray(hm8), hm128=jnp.asarray(hm128),
                gla=gla, rw_tri=rwm["tri"], rw_ones=rwm["ones"], rw_strict=jnp.asarray(strict, F32),
                rw_incl=jnp.asarray(incl, F32), bd256=jnp.asarray(wones, F32),
                left_q=left(256, HD, ROT // 2), left_i=left(256, IDX_D, IDX_ROT // 2),
                left_kw=left(128, 128, IDX_ROT // 2))


def _rope_tables(pos, periodic):
    pos = pos.astype(F32)[:, None]

    def build(width, group, rot, extra=None):
        half = rot // 2
        freq = ROPE_THETA ** (-jnp.arange(half, dtype=F32) * (2.0 / rot))
        ang = pos * freq
        cos, sin = jnp.cos(ang), jnp.sin(ang)
        n = pos.shape[0]
        ones = jnp.ones((n, group - rot), F32)
        zeros = jnp.zeros((n, group - rot), F32)
        cg = jnp.concatenate([cos, cos, ones], axis=1)
        sg = jnp.concatenate([-sin, sin, zeros], axis=1)
        reps = width // group
        c, s = jnp.tile(cg, (1, reps)), jnp.tile(sg, (1, reps))
        if extra is not None:
            c, s = extra(c, s)
        return c, s

    cq, sq = build(256, HD, ROT)
    ci, si = build(256, IDX_D, IDX_ROT)

    def kw_extra(c, s):
        lane = jnp.arange(128)
        scale = jnp.where((lane >= IDX_D) & (lane < IDX_D + IDX_H), IDX_H ** -0.5, 1.0)
        keep = (lane < IDX_D)
        return jnp.where(keep, c, scale[None, :]), jnp.where(keep, s, 0.0)

    ckw, skw = build(128, IDX_D, IDX_ROT, kw_extra)
    return dict(cq=cq, sq=sq, ci=ci, si=si, ckw=ckw, skw=skw, periodic=periodic)


def _blockdiag(blocks):
    g, r, c = blocks.shape
    eye = jnp.eye(g, dtype=blocks.dtype)
    return jnp.einsum('grc,gh->grhc', blocks, eye).reshape(g * r, g * c)


def _layer_weights(l, p):
    w_in = p["w_in"][l]
    z = lambda n: jnp.zeros((D_MODEL, n), F32)
    w_mix = jnp.concatenate([
        w_in[:, 256:512], w_in[:, 528:784], w_in[:, 0:256],
        w_in[:, 784:1808], w_in[:, 1848:2104], w_in[:, 2104:3000],
        w_in[:, 512:528], z(112), w_in[:, 1808:1848], z(88)], axis=1).astype(BF16)
    w = dict(w_mix=w_mix, w_gates=w_in[:, 3000:7096].astype(BF16),
             w_br=p["w_br"][l].astype(BF16), w_o=p["w_o"][l].astype(BF16),
             w_ff1=p["w_ff1"][l].astype(BF16), w_ff2=p["w_ff2"][l].astype(BF16),
             norm1_g=p["norm1_g"][l][None, :], norm2_g=p["norm2_g"][l][None, :])
    a2 = jnp.zeros((128, 128), F32).at[0:GLA_RANK].set(p["gla_a2"][l])
    w["gla_a2h"], w["gla_a2l"] = _hilo(a2)
    w["gla_ab"] = p["gla_ab"][l][None, :]
    w["gla_ng"] = jnp.tile(p["gla_ng"][l], GLA_H)[None, :]
    w["att_qg"] = jnp.tile(p["att_qg"][l], ATT_H)[None, :]
    w["att_kg"] = jnp.tile(p["att_kg"][l], ATT_H)[None, :]
    dt = jnp.exp(p["s5_log_dt"][l])[:, None]
    lr = jnp.minimum(p["s5_a_re"][l], -1e-4)
    li = p["s5_a_im"][l]
    mag = jnp.exp(lr * dt)
    abr, abi = mag * jnp.cos(li * dt), mag * jnp.sin(li * dt)
    den = lr * lr + li * li
    fr = ((abr - 1.0) * lr + abi * li) / den
    fi = (abi * lr - (abr - 1.0) * li) / den
    b_re, b_im = p["s5_b_re"][l], p["s5_b_im"][l]
    bbr = fr[..., None] * b_re - fi[..., None] * b_im
    bbi = fr[..., None] * b_im + fi[..., None] * b_re
    bmat = jnp.concatenate([_blockdiag(bbr.transpose(0, 2, 1)), _blockdiag(bbi.transpose(0, 2, 1))], axis=1)
    w["s5_bh"] = bmat.astype(BF16)
    w["s5_a"] = jnp.concatenate([abr.reshape(1, -1), abi.reshape(1, -1)], axis=1)
    w["s5_c"] = jnp.concatenate([_blockdiag(p["s5_c_re"][l].transpose(0, 2, 1)),
                                 -_blockdiag(p["s5_c_im"][l].transpose(0, 2, 1))], axis=0).astype(BF16)
    w["s5_d"] = p["s5_d"][l].reshape(1, -1)
    w["s5_gw"] = p["s5_glu_w"][l].astype(BF16)
    w["s5_gb"] = p["s5_glu_b"][l][None, :]
    w["rw_mu"] = p["rw_mu"][l][None, :]
    w["rw_w0"] = p["rw_w0"][l][None, :]
    w["rw_a0"] = p["rw_a0"][l][None, :]
    lo = jnp.zeros((128, 256), F32)
    w["rw_w2h"], w["rw_w2l"] = _hilo(lo.at[0:RW_WR].set(p["rw_w2"][l]))
    w["rw_a2h"], w["rw_a2l"] = _hilo(lo.at[RW_WR:RW_WR + RW_AR].set(p["rw_a2"][l]))
    w["rw_g2h"], w["rw_g2l"] = _hilo(lo.at[RW_WR + RW_AR:128].set(p["rw_g2"][l]))
    for nm in ("rw_kk", "rw_ka", "rw_rk", "rw_ng"):
        w[nm] = p[nm][l][None, :]
    return w


def _mix_and_ffn(x, mod, w, consts, P, o_att, st_gla0, st_s50, st_rw0, prev, batch, seq, lvalid, tm, tiles):
    o_gla, st_gla = _gla(P, st_gla0, w, consts, batch, seq, tiles["gla"], tiles["chunk"], lvalid)
    if lvalid == 1:
        o1, st1 = _s5_step(P.reshape(batch, seq, NP_COLS)[:, 0], st_s50[:, 0], w)
        o_s5 = jnp.pad(o1[:, None], ((0, 0), (0, seq - 1), (0, 0))).reshape(batch * seq, 256)
        st_s5 = st1[:, None]
    else:
        o_s5, st_s5 = _s5(P, st_s50, w, batch, seq, tiles["s5"], lvalid)
    o_rw, st_rw = _rwkv(P.reshape(batch, seq, NP_COLS), st_rw0, prev, w, consts, tiles["nb"], tiles["rw"], lvalid)
    x = _merge(x, mod, w["norm1_g"], (o_gla, o_att, o_s5, o_rw.reshape(batch * seq, 256)), w, tm, seq)
    x = _ffn(x, mod, w["norm2_g"], w, tm, seq)
    return x, st_gla, st_s5, st_rw


def _gla_state_out(st):
    n = st.shape[0]
    s = st.reshape(n, GLA_H, GLA_DV, GLA_H, GLA_DK)
    s = jnp.stack([s[:, h, :, h, :] for h in range(GLA_H)], axis=1)
    return s.transpose(0, 1, 3, 2)


def _gla_state_in(s):
    eye = jnp.eye(GLA_H, dtype=s.dtype)
    n = s.shape[0]
    return jnp.einsum('nhkv,hg->nhvgk', s, eye).reshape(n, GLA_H * GLA_DV, GLA_H * GLA_DK)


def _rw_state_out(st):
    n = st.shape[0]
    return st.reshape(n, RW_N, RW_H, RW_N).transpose(0, 2, 1, 3)


def _rw_state_in(s):
    n = s.shape[0]
    return s.transpose(0, 2, 1, 3).reshape(n, RW_N, RW_H * RW_N)


def _forward(x_prompt, x_sample, c_prompt, c_sample, cache_k, cache_v, cache_idx, state_gla,
             state_s5_re, state_s5_im, state_rwkv, state_shift, page_table, p):
    B, S, _ = x_prompt.shape
    N = x_sample.shape[0]
    depth = p["w_in"].shape[0]
    past = page_table.shape[1] * PAGE_SIZE
    tm_p = min(512, S)
    tm_s = min(128, N * SAMPLE_PAD)
    tiles_p = dict(gla=min(256, S), chunk=16, s5=min(256, S), rw=min(128, S), nb=min(4, B))
    tiles_s = dict(gla=min(128, N * SAMPLE_PAD), chunk=SAMPLE_PAD, s5=SAMPLE_PAD, rw=SAMPLE_PAD, nb=8)
    tq = min(128, S)
    consts = _constants({(t["gla"], t["chunk"]) for t in (tiles_p, tiles_s)})
    mod_all = _modulation(jnp.concatenate([c_prompt, c_sample], axis=0), p["ada_w"], p["ada_b"])
    tabs_p = _rope_tables(jnp.arange(S), True)
    tabs_s = _rope_tables(jnp.full((N * SAMPLE_PAD,), past), False)
    ck = cache_k.transpose(0, 1, 3, 4, 2).reshape(*cache_k.shape[:2], ATT_H * HD, PAGE_SIZE)
    cv = cache_v.transpose(0, 1, 3, 4, 2).reshape(*cache_v.shape[:2], ATT_H * HD, PAGE_SIZE)
    ci = cache_idx.transpose(0, 1, 3, 2)

    xp = x_prompt.reshape(B * S, D_MODEL)
    xs = jnp.pad(x_sample, ((0, 0), (0, SAMPLE_PAD - 1), (0, 0))).reshape(N * SAMPLE_PAD, D_MODEL)
    outs_p, outs_s = [], []
    layer_params = {k: v for k, v in p.items() if k not in ("ada_w", "ada_b")}
    w_all = jax.vmap(lambda one: _layer_weights(0, {k: v[None] for k, v in one.items()}))(layer_params)
    gla_in = jax.vmap(_gla_state_in)(state_gla)
    rw_in = jax.vmap(_rw_state_in)(state_rwkv)
    s5_in = jnp.concatenate([state_s5_re.reshape(depth, N, 1, S5_N), state_s5_im.reshape(depth, N, 1, S5_N)],
                            axis=3)
    mod_s = jnp.repeat(mod_all[:, B:], SAMPLE_PAD, axis=1)
    zeros_p = (jnp.zeros((B, 256, 128), F32), jnp.zeros((B, 1, 2 * S5_N), F32),
               jnp.zeros((B, RW_N, 256), F32), jnp.zeros((B, 1, RW_COLS), F32))
    for l in range(depth):
        w = {k: v[l] for k, v in w_all.items()}
        mod = mod_all[l, :B].reshape(B, 1, 6 * D_MODEL)
        P = _inproj(xp, mod, w["norm1_g"], w["w_mix"], tm_p, S)
        qn, kn, qir, kwr, qh, kh, vh, qih, kib = _dsa_prep(P, tabs_p, consts, w["att_qg"], w["att_kg"], tm_p, S)
        o_att = _dsa_attn(qh, kh, vh, qih, kwr, kib, B, S, tq)
        xp, st_gla, st_s5, st_rw = _mix_and_ffn(
            xp, mod, w, consts, P, o_att, *zeros_p, B, S, S, tm_p, tiles_p)
        P3 = P.reshape(B, S, NP_COLS)
        outs_p.append((kn, P3[:, :, C_DV:C_DV + 256], kwr, st_gla, st_s5, st_rw,
                       P3[:, S - 1, C_RW:C_RW + RW_COLS]))
        mod = mod_s[l]
        P = _inproj(xs, mod, w["norm1_g"], w["w_mix"], tm_s, SAMPLE_PAD)
        qn, kn, qir, kwr = _dsa_prep(P, tabs_s, consts, w["att_qg"], w["att_kg"], tm_s, SAMPLE_PAD)[:4]
        first = lambda a: a.reshape(N, SAMPLE_PAD, a.shape[-1])[:, 0]
        qn1, kn1, qir1, kwr1, P1 = first(qn), first(kn), first(qir), first(kwr), first(P)
        v1 = P1[:, C_DV:C_DV + 256]
        scores = _ds_scores(page_table, qir1.reshape(N, IDX_H, IDX_D),
                            kwr1[:, IDX_D:IDX_D + IDX_H].reshape(N, IDX_H, 1),
                            kwr1[:, :IDX_D].reshape(N, 1, IDX_D), ci, l)
        ncols = scores.shape[1] * 128
        sel = _ds_select(scores.reshape(N, ncols), min(TOPK_MAX, (past + 1) // 4), past + 1)
        o1 = _ds_attn(page_table, qn1.reshape(N, 1, 256), kn1.reshape(N, 1, 256), v1.reshape(N, 1, 256),
                      sel.reshape(N, ncols // 128, 128), ck, cv, l, consts["hm8"])
        o_att = jnp.pad(o1, ((0, 0), (0, SAMPLE_PAD - 1), (0, 0))).reshape(N * SAMPLE_PAD, 256)
        xs, st_gla, st_s5, st_rw = _mix_and_ffn(
            xs, mod, w, consts, P, o_att, gla_in[l], s5_in[l], rw_in[l],
            state_shift[l].reshape(N, 1, RW_COLS), N, SAMPLE_PAD, 1, tm_s, tiles_s)
        outs_s.append((kn1, v1, kwr1, st_gla, st_s5, st_rw, P1[:, C_RW:C_RW + RW_COLS]))
    yp = xp.reshape(B, S, D_MODEL)
    ys = xs.reshape(N, SAMPLE_PAD, D_MODEL)[:, 0:1]

    def assemble(outs, nb, seq):
        kn, v, kw, st_gla, st_s5, st_rw, shift = (jnp.stack([o[i] for o in outs]) for i in range(7))
        s5 = st_s5[:, :, 0]
        return (kn.reshape(depth, nb, seq, ATT_H, HD), v.reshape(depth, nb, seq, ATT_H, HD),
                kw.reshape(depth, nb, seq, 128)[..., :IDX_D], jax.vmap(_gla_state_out)(st_gla),
                s5[..., :S5_N].reshape(depth, nb, S5_G, S5_P), s5[..., S5_N:].reshape(depth, nb, S5_G, S5_P),
                jax.vmap(_rw_state_out)(st_rw), shift)

    return (yp, ys) + assemble(outs_p, B, S) + assemble(outs_s, N, 1)


def kernel(x_prompt, x_sample, c_prompt, c_sample, cache_k, cache_v, cache_idx, state_gla, state_s5_re, state_s5_im, state_rwkv, state_shift, page_table, ada_w, ada_b, norm1_g, norm2_g, w_in, gla_a2, gla_ab, gla_ng, att_qg, att_kg, s5_a_re, s5_a_im, s5_log_dt, s5_b_re, s5_b_im, s5_c_re, s5_c_im, s5_d, s5_glu_w, s5_glu_b, rw_mu, rw_w0, rw_w2, rw_a0, rw_a2, rw_g2, rw_kk, rw_ka, rw_rk, rw_ng, w_br, w_o, w_ff1, w_ff2):
    p = dict(ada_w=ada_w, ada_b=ada_b, norm1_g=norm1_g, norm2_g=norm2_g, w_in=w_in, gla_a2=gla_a2,
             gla_ab=gla_ab, gla_ng=gla_ng, att_qg=att_qg, att_kg=att_kg, s5_a_re=s5_a_re, s5_a_im=s5_a_im,
             s5_log_dt=s5_log_dt, s5_b_re=s5_b_re, s5_b_im=s5_b_im, s5_c_re=s5_c_re, s5_c_im=s5_c_im,
             s5_d=s5_d, s5_glu_w=s5_glu_w, s5_glu_b=s5_glu_b, rw_mu=rw_mu, rw_w0=rw_w0, rw_w2=rw_w2,
             rw_a0=rw_a0, rw_a2=rw_a2, rw_g2=rw_g2, rw_kk=rw_kk, rw_ka=rw_ka, rw_rk=rw_rk, rw_ng=rw_ng,
             w_br=w_br, w_o=w_o, w_ff1=w_ff1, w_ff2=w_ff2)
    return _forward(x_prompt, x_sample, c_prompt, c_sample, cache_k, cache_v, cache_idx, state_gla,
                    state_s5_re, state_s5_im, state_rwkv, state_shift, page_table, p)
```

```python
import functools
import math

import numpy as np
import jax
import jax.numpy as jnp
from jax import lax
from jax.experimental import pallas as pl
from jax.experimental.pallas import tpu as pltpu

F32 = jnp.float32
BF16 = jnp.bfloat16
I32 = jnp.int32

D_MODEL = 1024
BR_W = 256
GLA_H, GLA_DK, GLA_DV, GLA_RANK, GLA_TAU = 4, 32, 64, 16, 16.0
ATT_H, HD, ROT = 4, 64, 16
IDX_H, IDX_D, IDX_ROT = 8, 32, 8
TOPK_MAX = 256
ROPE_THETA = 500000.0
S5_G, S5_P, S5_CH = 16, 64, 16
S5_N = S5_G * S5_P
RW_H, RW_N, RW_WR, RW_AR, RW_GR = 4, 64, 32, 32, 64
RW_COLS = 896
D_FF = 4096
EPS = 1e-6
RW_GN_EPS = 64e-5
PAGE_SIZE = 128
INT_MIN = -(2 ** 31)
SAMPLE_PAD = 8

C_GV, C_GR, C_GQK = 0, 256, 512
C_DQ, C_DK, C_DV, C_DQI = 768, 1024, 1280, 1536
C_S5 = 1792
C_RW = 2048
C_RWLO = 2816
C_GA = 2944
C_DKW = 3072
NP_COLS = 3200
VMEM_LIMIT = 56 * 1024 * 1024


def _cparams(n_axes):
    return pltpu.CompilerParams(dimension_semantics=("arbitrary",) * n_axes,
                                vmem_limit_bytes=VMEM_LIMIT)


def _split_dot(x, w, terms):
    acc = None
    r = x
    for i in range(terms):
        hi = r.astype(BF16)
        d = jnp.dot(hi, w, preferred_element_type=F32)
        acc = d if acc is None else acc + d
        if i + 1 < terms:
            r = r - hi.astype(F32)
    return acc


def _dot3(x, w_hi, w_lo):
    x_hi = x.astype(BF16)
    x_lo = (x - x_hi.astype(F32)).astype(BF16)
    return (jnp.dot(x_hi, w_hi, preferred_element_type=F32)
            + jnp.dot(x_hi, w_lo, preferred_element_type=F32)
            + jnp.dot(x_lo, w_hi, preferred_element_type=F32))


def _hilo(w):
    hi = w.astype(BF16)
    return hi, (w - hi.astype(F32)).astype(BF16)


def _sigmoid(x):
    return 1.0 / (1.0 + jnp.exp(-x))


def _softplus(x):
    return jnp.maximum(x, 0.0) + jnp.log1p(jnp.exp(-jnp.abs(x)))


def _norm_mod(x, g, sc, sh):
    ms = jnp.mean(x * x, axis=-1, keepdims=True)
    return (x * lax.rsqrt(ms + EPS) * g) * (1.0 + sc) + sh


def _mod_spec(mod, j, tm, seq):
    if mod.ndim == 3:
        return pl.BlockSpec((None, 1, D_MODEL), lambda i: ((i * tm) // seq, 0, j))
    return pl.BlockSpec((tm, D_MODEL), lambda i: (i, j))


def _const_spec(a):
    nd = a.ndim
    return pl.BlockSpec(a.shape, lambda *_: (0,) * nd)


def _mod_kernel(c_ref, w_ref, b_ref, o_ref):
    o_ref[...] = jnp.dot(c_ref[...], w_ref[...].astype(BF16), preferred_element_type=F32) + b_ref[...]


def _modulation(c_all, ada_w, ada_b):
    depth = ada_w.shape[0]
    rows = c_all.shape[0]
    tn = 1536
    return pl.pallas_call(
        _mod_kernel,
        grid=(depth, 6 * D_MODEL // tn),
        in_specs=[pl.BlockSpec((rows, D_MODEL), lambda l, j: (0, 0)),
                  pl.BlockSpec((None, D_MODEL, tn), lambda l, j: (l, 0, j)),
                  pl.BlockSpec((None, 1, tn), lambda l, j: (l, 0, j))],
        out_specs=pl.BlockSpec((None, rows, tn), lambda l, j: (l, 0, j)),
        out_shape=jax.ShapeDtypeStruct((depth, rows, 6 * D_MODEL), F32),
        compiler_params=_cparams(2), name="modulation",
    )(c_all.astype(BF16), ada_w, ada_b.reshape(depth, 1, 6 * D_MODEL))


def _inproj_kernel(x_ref, sc_ref, sh_ref, g_ref, w_ref, o_ref):
    h = _norm_mod(x_ref[...], g_ref[...], sc_ref[...], sh_ref[...])
    o_ref[...] = jnp.dot(h.astype(BF16), w_ref[...], preferred_element_type=F32)


def _inproj(x, mod, g, w_mix, tm, seq):
    rows = x.shape[0]
    return pl.pallas_call(
        _inproj_kernel,
        grid=(rows // tm,),
        in_specs=[pl.BlockSpec((tm, D_MODEL), lambda i: (i, 0)),
                  _mod_spec(mod, 1, tm, seq), _mod_spec(mod, 0, tm, seq),
                  _const_spec(g), _const_spec(w_mix)],
        out_specs=pl.BlockSpec((tm, NP_COLS), lambda i: (i, 0)),
        out_shape=jax.ShapeDtypeStruct((rows, NP_COLS), F32),
        compiler_params=_cparams(1), name="inproj",
    )(x, mod, mod, g, w_mix)


def _rope_apply(x, cos, sn, left, shift):
    n = x.shape[-1]
    rot = jnp.where(left > 0.0, pltpu.roll(x, n - shift, 1), pltpu.roll(x, shift, 1))
    return x * cos + rot * sn


def _dsa_prep_kernel(q_ref, k_ref, v_ref, qi_ref, kw_ref, cq_ref, sq_ref, ci_ref, si_ref, ckw_ref, skw_ref,
                     lq_ref, li_ref, lkw_ref, qg_ref, kg_ref, wavg_ref,
                     qn_ref, kn_ref, qir_ref, kwr_ref, qh_ref, kh_ref, vh_ref, qih_ref, kib_ref):
    wavg = wavg_ref[...]

    def headnorm(x, g):
        ms = _split_dot(x * x, wavg, 3)
        return x * lax.rsqrt(ms + EPS) * g

    cq, sq, lq = cq_ref[...], sq_ref[...], lq_ref[...]
    qn = _rope_apply(headnorm(q_ref[...], qg_ref[...]), cq, sq, lq, ROT // 2)
    kn = _rope_apply(headnorm(k_ref[...], kg_ref[...]), cq, sq, lq, ROT // 2)
    qir = _rope_apply(qi_ref[...], ci_ref[...], si_ref[...], li_ref[...], IDX_ROT // 2)
    kwr = _rope_apply(kw_ref[...], ckw_ref[...], skw_ref[...], lkw_ref[...], IDX_ROT // 2)
    qn_ref[...] = qn
    kn_ref[...] = kn
    qir_ref[...] = qir
    kwr_ref[...] = kwr
    v = v_ref[...]
    for h in range(ATT_H):
        hs = slice(h * HD, (h + 1) * HD)
        qh_ref[h] = qn[:, hs].astype(BF16)
        kh_ref[h] = kn[:, hs].astype(BF16)
        vh_ref[h] = v[:, hs].astype(BF16)
    for h in range(IDX_H):
        qih_ref[h] = qir[:, h * IDX_D:(h + 1) * IDX_D].astype(BF16)
    kib_ref[...] = kwr.astype(BF16)


def _dsa_prep(P, tabs, consts, qg, kg, tm, seq):
    rows = P.shape[0]
    nt = seq // tm if tabs["periodic"] else None

    def tab_spec(w):
        if tabs["periodic"]:
            return pl.BlockSpec((tm, w), lambda i: (i % nt, 0))
        return pl.BlockSpec((tm, w), lambda i: (i, 0))

    def col(off, w):
        return pl.BlockSpec((tm, w), lambda i: (i, off // w))

    out256 = jax.ShapeDtypeStruct((rows, 256), F32)
    heads = lambda n, d: (pl.BlockSpec((n, tm, d), lambda i: (0, i, 0)), jax.ShapeDtypeStruct((n, rows, d), BF16))
    hq, hi = heads(ATT_H, HD), heads(IDX_H, IDX_D)
    return pl.pallas_call(
        _dsa_prep_kernel,
        grid=(rows // tm,),
        in_specs=[col(C_DQ, 256), col(C_DK, 256), col(C_DV, 256), col(C_DQI, 256), col(C_DKW, 128),
                  tab_spec(256), tab_spec(256), tab_spec(256), tab_spec(256), tab_spec(128), tab_spec(128),
                  _const_spec(consts["left_q"]), _const_spec(consts["left_i"]), _const_spec(consts["left_kw"]),
                  _const_spec(qg), _const_spec(kg), _const_spec(consts["wavg"])],
        out_specs=[pl.BlockSpec((tm, 256), lambda i: (i, 0))] * 3 + [pl.BlockSpec((tm, 128), lambda i: (i, 0))]
                  + [hq[0], hq[0], hq[0], hi[0], pl.BlockSpec((tm, 128), lambda i: (i, 0))],
        out_shape=[out256, out256, out256, jax.ShapeDtypeStruct((rows, 128), F32),
                   hq[1], hq[1], hq[1], hi[1], jax.ShapeDtypeStruct((rows, 128), BF16)],
        compiler_params=_cparams(1), name="dsa_prep",
    )(P, P, P, P, P, tabs["cq"], tabs["sq"], tabs["ci"], tabs["si"], tabs["ckw"], tabs["skw"],
      consts["left_q"], consts["left_i"], consts["left_kw"], qg, kg, consts["wavg"])


def _score_keys(scores, valid):
    s = jnp.where(scores == 0.0, 0.0, scores)
    bits = pltpu.bitcast(s, I32)
    key = bits ^ (jnp.right_shift(bits, 31) & 0x7FFFFFFF)
    return jnp.where(valid, key, INT_MIN)


def _topk_select(key_ref, k, col, side=None, n_side=0):
    rows, cols = key_ref.shape
    kf = float(k)
    nbits = max(1, int(math.ceil(math.log2(cols))))

    ng = 4 if rows % 32 == 0 else 1
    rg = rows // ng

    def count_ge(g, c):
        return jnp.sum(jnp.where(key_ref[g * rg:(g + 1) * rg, :] >= c, 1.0, 0.0), axis=-1, keepdims=True)

    bases = tuple(jnp.where(count_ge(g, jnp.zeros((rg, 1), I32)) >= kf, 0, INT_MIN).astype(I32)
                  for g in range(ng))

    def bit_step(i, bases):
        bit = lax.shift_left(jnp.int32(1), 30 - i)
        return tuple(jnp.where(count_ge(g, b | bit) >= kf, b | bit, b) for g, b in enumerate(bases))

    if side is None:
        bases = lax.fori_loop(0, 31, bit_step, bases, unroll=4)
    else:
        per = -(-31 // n_side)

        def outer(j, bases):
            side(j)
            for t in range(per):
                s = j * per + t
                bit = jnp.where(s <= 30, lax.shift_left(jnp.int32(1), jnp.maximum(30 - s, 0)), 0)
                bases = tuple(jnp.where(count_ge(g, b | bit) >= kf, b | bit, b) for g, b in enumerate(bases))
            return bases

        bases = lax.fori_loop(0, n_side, outer, bases)
    thr = bases[0] if ng == 1 else jnp.concatenate(bases, axis=0)
    key = key_ref[...]
    need = kf - jnp.sum(jnp.where(key > thr, 1.0, 0.0), axis=-1, keepdims=True)
    excess = jnp.sum(jnp.where(key == thr, 1.0, 0.0), axis=-1, keepdims=True) > need

    def pos_step(i, pos):
        cand = pos + lax.shift_left(jnp.int32(1), nbits - 1 - i)
        hit = jnp.where(key_ref[...] == thr, jnp.where(col < cand, 1.0, 0.0), 0.0)
        return jnp.where(jnp.sum(hit, axis=-1, keepdims=True) < need, cand, pos)

    n_iter = jnp.where(jnp.max(jnp.where(excess, 1, 0)) > 0, nbits, 0)
    pos = lax.fori_loop(0, n_iter, pos_step, jnp.zeros((rows, 1), I32))
    pos = jnp.where(excess, pos, cols)
    return jnp.where(key > thr, 1.0, jnp.where(key == thr, jnp.where(col <= pos, 1.0, 0.0), 0.0))


def _dsa_attn_kernel(qh_ref, qih_ref, kwq_ref, kh_ref, vh_ref, kib_ref, o_ref, key_ref, lg_ref,
                     *, tq, topk, nvar):
    it = pl.program_id(1)
    t0 = it * tq
    seq = kh_ref.shape[1]
    step = seq // nvar
    per = step // tq

    def body(klen):
        ki = kib_ref[0:klen, 0:IDX_D]
        kwq = kwq_ref[...]
        scores = jnp.zeros((tq, klen), F32)
        for h in range(IDX_H):
            s = lax.dot_general(qih_ref[h], ki, _LANES, preferred_element_type=F32)
            w = kwq[:, IDX_D + h:IDX_D + h + 1] * (IDX_D ** -0.5)
            scores = scores + w * jnp.maximum(s, 0.0)
        col = lax.broadcasted_iota(I32, (tq, klen), 1)
        row = t0 + lax.broadcasted_iota(I32, (tq, klen), 0)
        causal = col <= row
        keys = key_ref.at[:, pl.ds(0, klen)]
        keys[...] = _score_keys(scores, causal)

        def logits(h):
            lg_ref[h, :, 0:klen] = lax.dot_general(qh_ref[h], kh_ref[h, 0:klen, :], _LANES,
                                                   preferred_element_type=F32) * (HD ** -0.5)

        sel = jnp.where(causal, _topk_select(keys, topk, col, logits, ATT_H), 0.0) > 0.0

        def head(h):
            lg = jnp.where(sel, lg_ref[h, :, 0:klen], -jnp.inf)
            m = jnp.max(lg, axis=-1, keepdims=True)
            yield
            p = jnp.exp(lg - m)
            l = jnp.sum(p, axis=-1, keepdims=True)
            o = jnp.dot(p.astype(BF16), vh_ref[h, 0:klen, :], preferred_element_type=F32)
            yield
            o_ref[:, h * HD:(h + 1) * HD] = o / l

        chains = [head(h) for h in range(ATT_H)]
        while chains:
            chains = [ch for ch in chains if next(ch, True) is None]

    for var in range(nvar):
        pl.when(it // per == var)(functools.partial(body, (var + 1) * step))


def _dsa_attn(qh, kh, vh, qih, kwr, kib, batch, seq, tq):
    topk = min(TOPK_MAX, seq // 4)
    nq = seq // tq
    nvar = min(8, nq)
    while seq // nvar < topk:
        nvar //= 2
    return pl.pallas_call(
        functools.partial(_dsa_attn_kernel, tq=tq, topk=topk, nvar=nvar),
        grid=(batch, nq),
        in_specs=[pl.BlockSpec((ATT_H, tq, HD), lambda b, i: (0, b * nq + i, 0)),
                  pl.BlockSpec((IDX_H, tq, IDX_D), lambda b, i: (0, b * nq + i, 0)),
                  pl.BlockSpec((tq, 128), lambda b, i: (b * nq + i, 0)),
                  pl.BlockSpec((ATT_H, seq, HD), lambda b, i: (0, b, 0)),
                  pl.BlockSpec((ATT_H, seq, HD), lambda b, i: (0, b, 0)),
                  pl.BlockSpec((seq, 128), lambda b, i: (b, 0))],
        out_specs=pl.BlockSpec((tq, 256), lambda b, i: (b * nq + i, 0)),
        out_shape=jax.ShapeDtypeStruct((batch * seq, 256), F32),
        scratch_shapes=[pltpu.VMEM((tq, seq), I32), pltpu.VMEM((ATT_H, tq, seq), F32)],
        compiler_params=_cparams(2), name="dsa_attn",
    )(qh, qih, kwr, kh, vh, kib)


def _split3_rhs(m, x):
    hi = x.astype(BF16)
    r = x - hi.astype(F32)
    mid = r.astype(BF16)
    lo = (r - mid.astype(F32)).astype(BF16)
    return (jnp.dot(m, hi, preferred_element_type=F32) + jnp.dot(m, mid, preferred_element_type=F32)
            + jnp.dot(m, lo, preferred_element_type=F32))


_LANES = (((1,), (1,)), ((), ()))
_ROWS = (((0,), (0,)), ((), ()))


def _gla_kernel(v_ref, r_ref, qk_ref, a_ref, st0_ref, a2h_ref, a2l_ref, ab_ref, ng_ref,
                tri_ref, ones_ref, amask_ref, hm128_ref, hm256_ref, bd_ref, wavg_ref,
                o_ref, st_ref, *, tile, chunk, lvalid, independent):
    if not independent:
        @pl.when(pl.program_id(1) == 0)
        def _():
            st_ref[...] = st0_ref[...]

    nch = tile // chunk
    z = _dot3(a_ref[...], a2h_ref[...], a2l_ref[...]) + ab_ref[...]
    la = (jnp.minimum(z, 0.0) - jnp.log1p(jnp.exp(-jnp.abs(z)))) * (1.0 / GLA_TAU)
    q = qk_ref[:, 0:128] * (GLA_DK ** -0.5)
    k = qk_ref[:, 128:256]
    v = v_ref[...]
    if independent and lvalid < chunk:
        keep = (lax.broadcasted_iota(I32, (tile, 128), 0) % chunk) < lvalid
        la = jnp.where(keep, la, 0.0)
        k = jnp.where(keep, k, 0.0)
        v = jnp.where((lax.broadcasted_iota(I32, (tile, 256), 0) % chunk) < lvalid, v, 0.0)
    b = _split3_rhs(tri_ref[...], la)
    tot = _split3_rhs(ones_ref[...], la)
    qe = q * jnp.exp(b)
    kinv = (k * jnp.exp(-b)).astype(BF16)
    kd = (k * jnp.exp(tot - b)).astype(BF16)
    vb = v.astype(BF16)
    hm128 = hm128_ref[...]
    hm256 = hm256_ref[...]
    qblk = jnp.concatenate([qe[c * chunk:(c + 1) * chunk] * hm128[h:h + 1]
                            for c in range(nch) for h in range(GLA_H)], axis=0).astype(BF16)
    att = lax.dot_general(qblk, kinv, _LANES, preferred_element_type=F32) * amask_ref[...]
    intra = jnp.dot(att.astype(BF16), vb, preferred_element_type=F32)
    qeb = qe.astype(BF16)
    st = None if independent else st_ref[...]
    for c in range(nch):
        rows = slice(c * chunk, (c + 1) * chunk)
        if independent:
            st = st0_ref[c]
        o = lax.dot_general(qeb[rows], st.astype(BF16), _LANES, preferred_element_type=F32)
        for h in range(GLA_H):
            r0 = (c * GLA_H + h) * chunk
            o = o + intra[r0:r0 + chunk] * hm256[h:h + 1]
        o_ref[rows, :] = o
        upd = lax.dot_general(vb[rows], kd[rows], _ROWS, preferred_element_type=F32)
        st = st * jnp.exp(tot[c * chunk:c * chunk + 1]) + upd * bd_ref[...]
        if independent:
            st_ref[c] = st
    if not independent:
        st_ref[...] = st
    o = o_ref[...]
    ms = _split_dot(o * o, wavg_ref[...], 2)
    r = r_ref[...]
    o_ref[...] = o * lax.rsqrt(ms + EPS) * ng_ref[...] * (r * _sigmoid(r))


def _gla(P, st0, w, consts, batch, seq, tg, chunk, lvalid):
    independent = lvalid < seq
    if independent:
        assert seq == chunk
        nseq = tg // chunk
        nt, grid = 1, (batch // nseq, 1)
        st_spec = pl.BlockSpec((nseq, 256, 128), lambda b, j: (b, 0, 0))
    else:
        nt, grid = seq // tg, (batch, seq // tg)
        st_spec = pl.BlockSpec((None, 256, 128), lambda b, j: (b, 0, 0))

    def col(off, wd):
        return pl.BlockSpec((tg, wd), lambda b, j: (b * nt + j, off // wd))

    gm = consts["gla"][(tg, chunk)]
    cs = [w["gla_a2h"], w["gla_a2l"], w["gla_ab"], w["gla_ng"],
          gm["tri"], gm["ones"], gm["amask"], consts["hm128"], consts["hm8"], consts["bd"], consts["wavg"]]
    return pl.pallas_call(
        functools.partial(_gla_kernel, tile=tg, chunk=chunk, lvalid=lvalid, independent=independent),
        grid=grid,
        in_specs=[col(C_GV, 256), col(C_GR, 256), col(C_GQK, 256), col(C_GA, 128), st_spec]
                 + [_const_spec(a) for a in cs],
        out_specs=[pl.BlockSpec((tg, 256), lambda b, j: (b * nt + j, 0)), st_spec],
        out_shape=[jax.ShapeDtypeStruct((batch * seq, 256), F32),
                   jax.ShapeDtypeStruct((batch, 256, 128), F32)],
        compiler_params=_cparams(2), name="gla",
    )(P, P, P, P, st0, *cs)


def _gelu_tanh(x):
    return 0.5 * x * (1.0 + jnp.tanh(math.sqrt(2.0 / math.pi) * (x + 0.044715 * (x * x * x))))


def _s5_kernel(u_ref, x0_ref, a_ref, bh_ref, c_ref, d_ref, gw_ref, gb_ref,
               o_ref, xf_ref, st_s, bur_s, bui_s, xr_s, xi_s, *, tile, last_row):
    @pl.when(pl.program_id(1) == 0)
    def _():
        st_s[...] = x0_ref[...]

    u = u_ref[...]
    bu = jnp.dot(u.astype(BF16), bh_ref[...], preferred_element_type=F32)
    bur_s[...] = bu[:, 0:S5_N]
    bui_s[...] = bu[:, S5_N:2 * S5_N]
    ar = a_ref[:, 0:S5_N]
    ai = a_ref[:, S5_N:2 * S5_N]

    def step(t, carry):
        xr, xi = carry
        row = pl.ds(t, 1)
        nr = ar * xr - ai * xi + bur_s[row, :]
        ni = ar * xi + ai * xr + bui_s[row, :]
        xr_s[row, :] = nr
        xi_s[row, :] = ni
        return nr, ni

    xr, xi = lax.fori_loop(0, tile, step, (st_s[:, 0:S5_N], st_s[:, S5_N:2 * S5_N]), unroll=8)
    st_s[:, 0:S5_N] = xr
    st_s[:, S5_N:2 * S5_N] = xi
    y = (jnp.dot(xr_s[...].astype(BF16), c_ref[0:S5_N, :], preferred_element_type=F32)
         + jnp.dot(xi_s[...].astype(BF16), c_ref[S5_N:2 * S5_N, :], preferred_element_type=F32)
         + d_ref[...] * u)
    z = _gelu_tanh(y)
    gate = jnp.dot(z.astype(BF16), gw_ref[...], preferred_element_type=F32) + gb_ref[...]
    o_ref[...] = z * _sigmoid(gate)
    xf_ref[:, 0:S5_N] = xr_s[last_row:last_row + 1, :]
    xf_ref[:, S5_N:2 * S5_N] = xi_s[last_row:last_row + 1, :]


def _s5(P, x0, w, batch, seq, tile, lvalid):
    nt = seq // tile
    last_row = (lvalid - 1) % tile
    cs = [w["s5_a"], w["s5_bh"], w["s5_c"], w["s5_d"], w["s5_gw"], w["s5_gb"]]
    return pl.pallas_call(
        functools.partial(_s5_kernel, tile=tile, last_row=last_row),
        grid=(batch, nt),
        in_specs=[pl.BlockSpec((tile, 256), lambda b, j: (b * nt + j, C_S5 // 256)),
                  pl.BlockSpec((None, 1, 2 * S5_N), lambda b, j: (b, 0, 0))] + [_const_spec(a) for a in cs],
        out_specs=[pl.BlockSpec((tile, 256), lambda b, j: (b * nt + j, 0)),
                   pl.BlockSpec((None, 1, 2 * S5_N), lambda b, j: (b, 0, 0))],
        out_shape=[jax.ShapeDtypeStruct((batch * seq, 256), F32),
                   jax.ShapeDtypeStruct((batch, 1, 2 * S5_N), F32)],
        scratch_shapes=[pltpu.VMEM((1, 2 * S5_N), F32)] + [pltpu.VMEM((tile, S5_N), F32)] * 4,
        compiler_params=_cparams(2), name="s5",
    )(P, x0, *cs)


def _s5_step_kernel(u_ref, x0_ref, a_ref, bh_ref, c_ref, d_ref, gw_ref, gb_ref, o_ref, xf_ref):
    u = u_ref[...]
    bu = jnp.dot(u.astype(BF16), bh_ref[...], preferred_element_type=F32)
    ar, ai = a_ref[:, 0:S5_N], a_ref[:, S5_N:2 * S5_N]
    xr0, xi0 = x0_ref[:, 0:S5_N], x0_ref[:, S5_N:2 * S5_N]
    xr = ar * xr0 - ai * xi0 + bu[:, 0:S5_N]
    xi = ar * xi0 + ai * xr0 + bu[:, S5_N:2 * S5_N]
    y = (jnp.dot(xr.astype(BF16), c_ref[0:S5_N, :], preferred_element_type=F32)
         + jnp.dot(xi.astype(BF16), c_ref[S5_N:2 * S5_N, :], preferred_element_type=F32)
         + d_ref[...] * u)
    z = _gelu_tanh(y)
    gate = jnp.dot(z.astype(BF16), gw_ref[...], preferred_element_type=F32) + gb_ref[...]
    o_ref[...] = z * _sigmoid(gate)
    xf_ref[:, 0:S5_N] = xr
    xf_ref[:, S5_N:2 * S5_N] = xi


def _s5_step(P1, x0, w):
    n = P1.shape[0]
    cs = [w["s5_a"], w["s5_bh"], w["s5_c"], w["s5_d"], w["s5_gw"], w["s5_gb"]]
    return pl.pallas_call(
        _s5_step_kernel,
        grid=(1,),
        in_specs=[pl.BlockSpec((n, 256), lambda i: (0, C_S5 // 256)),
                  pl.BlockSpec((n, 2 * S5_N), lambda i: (0, 0))] + [_const_spec(a) for a in cs],
        out_specs=[pl.BlockSpec((n, 256), lambda i: (0, 0)), pl.BlockSpec((n, 2 * S5_N), lambda i: (0, 0))],
        out_shape=[jax.ShapeDtypeStruct((n, 256), F32), jax.ShapeDtypeStruct((n, 2 * S5_N), F32)],
        compiler_params=_cparams(1), name="s5_step",
    )(P1, x0, *cs)


RW_CH = 16
RW_SC = 4 * RW_CH


def _rwkv_chunked(sf_ref, r_s, k_s, v_s, al_s, be_s, lw_s, y_s, sbd_s, hm, tri, onesb, strict, incl, bd,
                  nb, tile):
    nh = RW_H

    def blk(x):
        return jnp.concatenate([x[RW_CH * c:RW_CH * (c + 1)] * hm[h:h + 1]
                                for c in range(4) for h in range(nh)], axis=0)

    def rep(x):
        return jnp.concatenate([x[RW_CH * c:RW_CH * (c + 1)] for c in range(4) for _ in range(nh)], axis=0)

    def stack_heads(x):
        return jnp.concatenate([x[RW_CH * c:RW_CH * (c + 1), RW_N * h:RW_N * (h + 1)]
                                for c in range(4) for h in range(nh)], axis=0)

    def mm(a, b):
        return jnp.dot(a, b, preferred_element_type=F32)

    for b in range(nb):
        sbd_s[b] = sf_ref[b]

    def superchunk(sc, carry):
        r0 = pl.multiple_of(sc * RW_SC, RW_SC)
        rows = pl.ds(r0, RW_SC)

        def chain(b):
            lw = lw_s[b, rows, :]
            cum = _split3_rhs(tri, lw)
            tot = _split3_rhs(onesb, lw)
            rr, kx, vv = r_s[b, rows, :], k_s[b, rows, :], v_s[b, rows, :]
            al, be = al_s[b, rows, :], be_s[b, rows, :]
            pinv = jnp.exp(-cum)
            pend = jnp.exp(tot - cum)
            ab = al * jnp.exp(cum - lw)
            rb = rr * jnp.exp(cum)
            bt, kt, bp, kp = be * pinv, kx * pinv, be * pend, kx * pend
            ablk = blk(ab)
            lhs = jnp.concatenate([ablk, blk(rb)], axis=0).astype(BF16)
            rhs = jnp.concatenate([rep(bt), rep(kt)], axis=0).astype(BF16)
            g = lax.dot_general(lhs, rhs, _LANES, preferred_element_type=F32)
            yield
            mb = g[0:256, 0:256] * strict
            mk = g[0:256, 256:512] * strict
            myb = (g[256:512, 0:256] * incl).astype(BF16)
            myk = (g[256:512, 256:512] * incl).astype(BF16)
            vst = stack_heads(vv).astype(BF16)
            w0 = mm(mk.astype(BF16), vst)
            y0 = mm(myk, vst)
            u, mp = mb, mb
            for _ in range(3):
                mpb = mp.astype(BF16)
                mp = mm(mpb, mpb)
                yield
                u = u + mp + mm(u.astype(BF16), mp.astype(BF16))
                yield
            ub = u.astype(BF16)
            ab1 = (ablk + mm(ub, ablk.astype(BF16))).astype(BF16)
            rblk = lhs[256:512]
            z0 = w0 + mm(ub, w0.astype(BF16))
            bpk = jnp.concatenate([blk(bp), blk(kp)], axis=1).astype(BF16)
            yield
            S = sbd_s[b]
            nr = nh * RW_CH
            for c in range(4):
                d0 = nr * c
                lc = jnp.concatenate([ab1[d0:d0 + nr], rblk[d0:d0 + nr]], axis=0)
                s_hi = S.astype(BF16)
                s_lo = (S - s_hi.astype(F32)).astype(BF16)
                x = (lax.dot_general(lc, s_hi, _LANES, preferred_element_type=F32)
                     + lax.dot_general(lc, s_lo, _LANES, preferred_element_type=F32))
                yield
                zst = x[0:nr] + z0[d0:d0 + nr]
                zb = zst.astype(BF16)
                yst = x[nr:2 * nr] + y0[d0:d0 + nr] + mm(myb[d0:d0 + nr, d0:d0 + nr], zb)
                y_s[b, pl.ds(r0 + RW_CH * c, RW_CH), :] = jnp.concatenate(
                    [yst[RW_CH * h:RW_CH * (h + 1)] for h in range(nh)], axis=1)
                upd = (lax.dot_general(zb, bpk[d0:d0 + nr, 0:256], _ROWS, preferred_element_type=F32)
                       + lax.dot_general(vst[d0:d0 + nr], bpk[d0:d0 + nr, 256:512], _ROWS,
                                         preferred_element_type=F32))
                S = S * jnp.exp(tot[RW_CH * c:RW_CH * c + 1]) + upd
                yield
            sbd_s[b] = S

        chains = [chain(b) for b in range(nb)]
        while chains:
            chains = [ch for ch in chains if next(ch, True) is None]
        return carry

    lax.fori_loop(0, tile // RW_SC, superchunk, 0)
    for b in range(nb):
        sf_ref[b] = sbd_s[b]


def _rwkv_kernel(r_ref, k_ref, v_ref, lo_ref, s0_ref, prev_ref, mu_ref, w0_ref, a0_ref,
                 w2h_ref, w2l_ref, a2h_ref, a2l_ref, g2h_ref, g2l_ref, kkp_ref, ka_ref, rk_ref, ng_ref,
                 wones_ref, wavg_ref, idt_ref, hm_ref, tri_ref, ones_ref, strict_ref, incl_ref, bd_ref,
                 o_ref, sf_ref, r_s, k_s, v_s, kk_s, ka_s, w_s, y_s, g_s, bo_s, prev_s, sbd_s,
                 *, nb, tile, nsteps, chunked):
    @pl.when(pl.program_id(1) == 0)
    def _():
        sf_ref[...] = s0_ref[...]
        prev_s[...] = prev_ref[...]

    wones = wones_ref[...]
    wavg = wavg_ref[...]
    idt = idt_ref[...]
    row0_256 = lax.broadcasted_iota(I32, (tile, 256), 0) == 0
    row0_128 = lax.broadcasted_iota(I32, (tile, 128), 0) == 0

    def shift_mix(p, prev_row, mu, row0):
        sh = jnp.where(row0, prev_row, pltpu.roll(p, 1, 0))
        return p + (sh - p) * mu

    for b in range(nb):
        pr, pk, pv, plo = r_ref[b], k_ref[b], v_ref[b], lo_ref[b]
        r = shift_mix(pr, prev_s[b, :, 0:256], mu_ref[:, 0:256], row0_256)
        k = shift_mix(pk, prev_s[b, :, 256:512], mu_ref[:, 256:512], row0_256)
        v = shift_mix(pv, prev_s[b, :, 512:768], mu_ref[:, 512:768], row0_256)
        lo = shift_mix(plo, prev_s[b, :, 768:896], mu_ref[:, 768:896], row0_128)
        prev_s[b, :, 0:256] = pr[tile - 1:tile, :]
        prev_s[b, :, 256:512] = pk[tile - 1:tile, :]
        prev_s[b, :, 512:768] = pv[tile - 1:tile, :]
        prev_s[b, :, 768:896] = plo[tile - 1:tile, :]
        wl = w0_ref[...] + _dot3(jnp.tanh(lo), w2h_ref[...], w2l_ref[...])
        w = -_softplus(-wl) - 0.5
        a = _sigmoid(a0_ref[...] + _split_dot(lo, a2h_ref[...], 2))
        g = jnp.dot(_sigmoid(lo).astype(BF16), g2h_ref[...], preferred_element_type=F32)
        kk = k * kkp_ref[...]
        kk = kk * lax.rsqrt(_split_dot(kk * kk, wones, 2) + EPS)
        k2 = k * (1.0 + (a - 1.0) * ka_ref[...])
        bonus = _split_dot(r * k2 * rk_ref[...], wones, 2) * v
        r_s[b] = r
        k_s[b] = k2
        v_s[b] = v
        kk_s[b] = -kk
        ka_s[b] = kk * a
        w_s[b] = -jnp.exp(w)
        if nsteps < tile:
            y_s[b] = jnp.zeros((tile, 256), F32)
        g_s[b] = g
        bo_s[b] = bonus

    def step(t, carry):
        for b in range(nb):
            row = pl.ds(t, 1)
            S = sf_ref[b]
            sa = _split_dot(S * kk_s[b, row, :], wones, 2)
            vcol = _split_dot(idt * v_s[b, row, :], wones, 2)
            Sn = S * jnp.exp(w_s[b, row, :]) + sa * ka_s[b, row, :] + vcol * k_s[b, row, :]
            yb = jnp.dot((Sn * r_s[b, row, :]).astype(BF16), wones, preferred_element_type=F32)
            y_s[b, row, :] = jnp.sum(yb * idt, axis=0, keepdims=True)
            sf_ref[b] = Sn
        return carry

    if chunked:
        _rwkv_chunked(sf_ref, r_s, k_s, v_s, kk_s, ka_s, w_s, y_s, sbd_s, hm_ref[...], tri_ref[...],
                      ones_ref[...], strict_ref[...], incl_ref[...], bd_ref[...], nb, tile)
    else:
        lax.fori_loop(0, nsteps, step, 0)
    for b in range(nb):
        y = y_s[b]
        mu = _split_dot(y, wavg, 2)
        yc = y - mu
        var = _split_dot(yc * yc, wavg, 2)
        o_ref[b] = (yc * lax.rsqrt(var + RW_GN_EPS) * ng_ref[...] + bo_s[b]) * g_s[b]


def _rwkv(P3, s0, prev, w, consts, nb, tile, lvalid):
    batch, seq, _ = P3.shape
    nt = seq // tile
    nsteps = tile if lvalid >= seq else lvalid
    cs = [w["rw_mu"], w["rw_w0"], w["rw_a0"], w["rw_w2h"], w["rw_w2l"], w["rw_a2h"], w["rw_a2l"],
          w["rw_g2h"], w["rw_g2l"], w["rw_kk"], w["rw_ka"], w["rw_rk"], w["rw_ng"],
          consts["wones"], consts["wavg"], consts["idt"], consts["hm8"], consts["rw_tri"], consts["rw_ones"],
          consts["rw_strict"], consts["rw_incl"], consts["bd256"]]
    chunked = nsteps == tile and tile % RW_SC == 0

    def col(off, wd):
        return pl.BlockSpec((nb, tile, wd), lambda g, j: (g, j, off // wd))

    big = lambda: pltpu.VMEM((nb, tile, 256), F32)
    return pl.pallas_call(
        functools.partial(_rwkv_kernel, nb=nb, tile=tile, nsteps=nsteps, chunked=chunked),
        grid=(batch // nb, nt),
        in_specs=[col(C_RW, 256), col(C_RW + 256, 256), col(C_RW + 512, 256), col(C_RWLO, 128),
                  pl.BlockSpec((nb, RW_N, 256), lambda g, j: (g, 0, 0)),
                  pl.BlockSpec((nb, 1, RW_COLS), lambda g, j: (g, 0, 0))] + [_const_spec(a) for a in cs],
        out_specs=[pl.BlockSpec((nb, tile, 256), lambda g, j: (g, j, 0)),
                   pl.BlockSpec((nb, RW_N, 256), lambda g, j: (g, 0, 0))],
        out_shape=[jax.ShapeDtypeStruct((batch, seq, 256), F32),
                   jax.ShapeDtypeStruct((batch, RW_N, 256), F32)],
        scratch_shapes=[big(), big(), big(), big(), big(), big(), big(), big(), big(),
                        pltpu.VMEM((nb, 1, RW_COLS), F32),
                        pltpu.VMEM((nb, RW_N, 256) if chunked else (1, 8, 128), F32)],
        compiler_params=_cparams(2), name="rwkv",
    )(P3, P3, P3, P3, s0, prev, *cs)


def _merge_kernel(x_ref, sc_ref, sh_ref, gt_ref, g_ref, og_ref, oa_ref, os_ref, or_ref,
                  wg_ref, wbr_ref, wo_ref, o_ref):
    x = x_ref[...]
    h = _norm_mod(x, g_ref[...], sc_ref[...], sh_ref[...]).astype(BF16)
    merged = None
    for b, oref in enumerate((og_ref, oa_ref, os_ref, or_ref)):
        gate = _sigmoid(jnp.dot(h, wg_ref[:, b * D_MODEL:(b + 1) * D_MODEL], preferred_element_type=F32))
        proj = jnp.dot(oref[...].astype(BF16), wbr_ref[b], preferred_element_type=F32)
        merged = gate * proj if merged is None else merged + gate * proj
    y = jnp.dot(merged.astype(BF16), wo_ref[...], preferred_element_type=F32)
    o_ref[...] = x + gt_ref[...] * y


def _merge(x, mod, g, outs, w, tm, seq):
    rows = x.shape[0]
    row256 = pl.BlockSpec((tm, 256), lambda i: (i, 0))
    return pl.pallas_call(
        _merge_kernel,
        grid=(rows // tm,),
        in_specs=[pl.BlockSpec((tm, D_MODEL), lambda i: (i, 0)),
                  _mod_spec(mod, 1, tm, seq), _mod_spec(mod, 0, tm, seq), _mod_spec(mod, 2, tm, seq),
                  _const_spec(g), row256, row256, row256, row256,
                  _const_spec(w["w_gates"]), _const_spec(w["w_br"]), _const_spec(w["w_o"])],
        out_specs=pl.BlockSpec((tm, D_MODEL), lambda i: (i, 0)),
        out_shape=jax.ShapeDtypeStruct((rows, D_MODEL), F32),
        compiler_params=_cparams(1), name="merge",
    )(x, mod, mod, mod, g, *outs, w["w_gates"], w["w_br"], w["w_o"])


def _ffn_kernel(x_ref, sc_ref, sh_ref, gt_ref, g_ref, w1_ref, w2_ref, o_ref):
    x = x_ref[...]
    h = _norm_mod(x, g_ref[...], sc_ref[...], sh_ref[...]).astype(BF16)
    acc = None
    for c in range(D_FF // D_MODEL):
        cs = slice(c * D_MODEL, (c + 1) * D_MODEL)
        u = jnp.maximum(jnp.dot(h, w1_ref[:, cs], preferred_element_type=F32), 0.0)
        d = jnp.dot((u * u).astype(BF16), w2_ref[cs, :], preferred_element_type=F32)
        acc = d if acc is None else acc + d
    o_ref[...] = x + gt_ref[...] * acc


def _ffn(x, mod, g, w, tm, seq):
    rows = x.shape[0]
    return pl.pallas_call(
        _ffn_kernel,
        grid=(rows // tm,),
        in_specs=[pl.BlockSpec((tm, D_MODEL), lambda i: (i, 0)),
                  _mod_spec(mod, 4, tm, seq), _mod_spec(mod, 3, tm, seq), _mod_spec(mod, 5, tm, seq),
                  _const_spec(g), _const_spec(w["w_ff1"]), _const_spec(w["w_ff2"])],
        out_specs=pl.BlockSpec((tm, D_MODEL), lambda i: (i, 0)),
        out_shape=jax.ShapeDtypeStruct((rows, D_MODEL), F32),
        compiler_params=_cparams(1), name="ffn",
    )(x, mod, mod, mod, g, w["w_ff1"], w["w_ff2"])


def _ds_scores_kernel(pt_ref, q8_ref, w8_ref, kcur_ref, *refs, npages):
    pages, o_ref = refs[:npages], refs[npages]
    q8 = q8_ref[...]
    q8b = q8.astype(BF16)
    w8 = w8_ref[...] * (IDX_D ** -0.5)
    for p in range(npages):
        s = jnp.dot(q8b, pages[p][...].astype(BF16), preferred_element_type=F32)
        o_ref[p:p + 1, :] = jnp.sum(w8 * jnp.maximum(s, 0.0), axis=0, keepdims=True)
    s_cur = jnp.sum(q8 * kcur_ref[...], axis=-1, keepdims=True)
    i_cur = jnp.sum(w8 * jnp.maximum(s_cur, 0.0), axis=0, keepdims=True)
    lane = lax.broadcasted_iota(I32, (8, 128), 1)
    rowi = lax.broadcasted_iota(I32, (8, 128), 0)
    o_ref[npages:npages + 8, :] = jnp.where((lane == 0) & (rowi == 0), i_cur, -jnp.inf)


def _ds_scores(page_table, q8, w8, kcur, cache_idx, layer):
    n, npages = page_table.shape
    page_specs = [pl.BlockSpec((None, None, IDX_D, PAGE_SIZE), lambda i, pt, p=p: (layer, pt[i, p], 0, 0))
                  for p in range(npages)]
    return pl.pallas_call(
        functools.partial(_ds_scores_kernel, npages=npages),
        grid_spec=pltpu.PrefetchScalarGridSpec(
            num_scalar_prefetch=1, grid=(n,),
            in_specs=[pl.BlockSpec((None, IDX_H, IDX_D), lambda i, pt: (i, 0, 0)),
                      pl.BlockSpec((None, IDX_H, 1), lambda i, pt: (i, 0, 0)),
                      pl.BlockSpec((None, 1, IDX_D), lambda i, pt: (i, 0, 0))] + page_specs,
            out_specs=pl.BlockSpec((None, npages + 8, 128), lambda i, pt: (i, 0, 0))),
        out_shape=jax.ShapeDtypeStruct((n, npages + 8, 128), F32),
        compiler_params=_cparams(1), name="ds_scores",
    )(page_table, q8, w8, kcur, *([cache_idx] * npages))


def _ds_select_kernel(s_ref, o_ref, key_ref, *, topk, nvalid):
    rows, cols = s_ref.shape
    col = lax.broadcasted_iota(I32, (rows, cols), 1)
    valid = col < nvalid
    key_ref[...] = _score_keys(s_ref[...], valid)
    o_ref[...] = jnp.where(valid, _topk_select(key_ref, topk, col), 0.0)


def _ds_select(scores, topk, nvalid):
    rows, cols = scores.shape
    return pl.pallas_call(
        functools.partial(_ds_select_kernel, topk=topk, nvalid=nvalid),
        grid=(1,),
        in_specs=[pl.BlockSpec((rows, cols), lambda i: (0, 0))],
        out_specs=pl.BlockSpec((rows, cols), lambda i: (0, 0)),
        out_shape=jax.ShapeDtypeStruct((rows, cols), F32),
        scratch_shapes=[pltpu.VMEM((rows, cols), I32)],
        compiler_params=_cparams(1), name="ds_select",
    )(scores)


def _ds_attn_kernel(pt_ref, q_ref, kcur_ref, vcur_ref, m_ref, hm_ref, *refs, npages):
    kp, vp, o_ref = refs[:npages], refs[npages:2 * npages], refs[2 * npages]
    hm = hm_ref[...]
    qf = q_ref[...] * hm
    qb = qf.astype(BF16)
    sc = HD ** -0.5
    lg_cur = jnp.sum(qf * kcur_ref[...], axis=-1, keepdims=True) * sc
    cur_sel = m_ref[npages:npages + 1, 0:1] > 0.0
    mx = lg_cur
    lgs = []
    for p in range(npages):
        lg = jnp.dot(qb, kp[p][...].astype(BF16), preferred_element_type=F32) * sc
        lg = jnp.where(m_ref[p:p + 1, :] > 0.0, lg, -jnp.inf)
        lgs.append(lg)
        mx = jnp.maximum(mx, jnp.max(lg, axis=-1, keepdims=True))
    pc = jnp.where(cur_sel, jnp.exp(lg_cur - mx), 0.0)
    l = pc
    acc = pc * vcur_ref[...]
    for p in range(npages):
        pe = jnp.exp(lgs[p] - mx)
        l = l + jnp.sum(pe, axis=-1, keepdims=True)
        acc = acc + lax.dot_general(pe.astype(BF16), vp[p][...].astype(BF16), _LANES,
                                    preferred_element_type=F32)
    o_ref[...] = jnp.sum((acc / l) * hm, axis=0, keepdims=True)


def _ds_attn(page_table, q, kcur, vcur, mask, cache_k, cache_v, layer, headmask):
    n, npages = page_table.shape
    pspec = lambda p: pl.BlockSpec((None, None, 256, PAGE_SIZE), lambda i, pt, p=p: (layer, pt[i, p], 0, 0))
    row = pl.BlockSpec((None, 1, 256), lambda i, pt: (i, 0, 0))
    return pl.pallas_call(
        functools.partial(_ds_attn_kernel, npages=npages),
        grid_spec=pltpu.PrefetchScalarGridSpec(
            num_scalar_prefetch=1, grid=(n,),
            in_specs=[row, row, row,
                      pl.BlockSpec((None, npages + 8, 128), lambda i, pt: (i, 0, 0)),
                      pl.BlockSpec((8, 256), lambda i, pt: (0, 0))]
                     + [pspec(p) for p in range(npages)] + [pspec(p) for p in range(npages)],
            out_specs=row),
        out_shape=jax.ShapeDtypeStruct((n, 1, 256), F32),
        compiler_params=_cparams(1), name="ds_attn",
    )(page_table, q, kcur, vcur, mask, headmask, *([cache_k] * npages), *([cache_v] * npages))


def _constants(gla_keys):
    lane256 = np.arange(256)
    head = lane256 // 64
    wones = (head[:, None] == head[None, :]).astype(np.float32)
    idt = (np.arange(64)[:, None] == (lane256 % 64)[None, :]).astype(np.float32)
    e2 = ((np.arange(128) // 32)[:, None] == head[None, :]).astype(np.float32)
    bd = (head[:, None] == (np.arange(128) // 32)[None, :]).astype(np.float32)
    hm8 = (np.arange(8)[:, None] == head[None, :]).astype(np.float32)
    hm128 = (np.arange(8)[:, None] == (np.arange(128) // 32)[None, :]).astype(np.float32)

    def chunk_masks(tile, chunk, heads):
        t = np.arange(tile)
        same = (t[:, None] // chunk) == (t[None, :] // chunk)
        rows = np.arange(heads * tile)
        amask = ((rows // (heads * chunk))[:, None] == (t // chunk)[None, :]) & \
                ((t % chunk)[None, :] <= (rows % chunk)[:, None])
        return dict(tri=jnp.asarray(same & (t[None, :] <= t[:, None]), BF16), ones=jnp.asarray(same, BF16),
                    amask=jnp.asarray(amask, F32))

    gla = {key: chunk_masks(key[0], key[1], GLA_H) for key in gla_keys}
    rwm = chunk_masks(RW_SC, RW_CH, RW_H)
    i256 = np.arange(256)
    same16 = (i256[:, None] // RW_CH) == (i256[None, :] // RW_CH)
    strict = same16 & ((i256 % RW_CH)[None, :] < (i256 % RW_CH)[:, None])
    incl = same16 & ((i256 % RW_CH)[None, :] <= (i256 % RW_CH)[:, None])

    def left(width, group, half):
        return jnp.asarray(((np.arange(width) % group) < half).astype(np.float32)[None, :])

    return dict(wones=jnp.asarray(wones, BF16), wavg=jnp.asarray(wones / 64.0, BF16), idt=jnp.asarray(idt),
                e2=jnp.asarray(e2, BF16), bd=jnp.asarray(bd), hm8=jnp.asarray(hm8), hm128=jnp.asarray(hm128),
                gla=gla, rw_tri=rwm["tri"], rw_ones=rwm["ones"], rw_strict=jnp.asarray(strict, F32),
                rw_incl=jnp.asarray(incl, F32), bd256=jnp.asarray(wones, F32),
                left_q=left(256, HD, ROT // 2), left_i=left(256, IDX_D, IDX_ROT // 2),
                left_kw=left(128, 128, IDX_ROT // 2))


def _rope_tables(pos, periodic):
    pos = pos.astype(F32)[:, None]

    def build(width, group, rot, extra=None):
        half = rot // 2
        freq = ROPE_THETA ** (-jnp.arange(half, dtype=F32) * (2.0 / rot))
        ang = pos * freq
        cos, sin = jnp.cos(ang), jnp.sin(ang)
        n = pos.shape[0]
        ones = jnp.ones((n, group - rot), F32)
        zeros = jnp.zeros((n, group - rot), F32)
        cg = jnp.concatenate([cos, cos, ones], axis=1)
        sg = jnp.concatenate([-sin, sin, zeros], axis=1)
        reps = width // group
        c, s = jnp.tile(cg, (1, reps)), jnp.tile(sg, (1, reps))
        if extra is not None:
            c, s = extra(c, s)
        return c, s

    cq, sq = build(256, HD, ROT)
    ci, si = build(256, IDX_D, IDX_ROT)

    def kw_extra(c, s):
        lane = jnp.arange(128)
        scale = jnp.where((lane >= IDX_D) & (lane < IDX_D + IDX_H), IDX_H ** -0.5, 1.0)
        keep = (lane < IDX_D)
        return jnp.where(keep, c, scale[None, :]), jnp.where(keep, s, 0.0)

    ckw, skw = build(128, IDX_D, IDX_ROT, kw_extra)
    return dict(cq=cq, sq=sq, ci=ci, si=si, ckw=ckw, skw=skw, periodic=periodic)


def _blockdiag(blocks):
    g, r, c = blocks.shape
    eye = jnp.eye(g, dtype=blocks.dtype)
    return jnp.einsum('grc,gh->grhc', blocks, eye).reshape(g * r, g * c)


def _layer_weights(l, p):
    w_in = p["w_in"][l]
    z = lambda n: jnp.zeros((D_MODEL, n), F32)
    w_mix = jnp.concatenate([
        w_in[:, 256:512], w_in[:, 528:784], w_in[:, 0:256],
        w_in[:, 784:1808], w_in[:, 1848:2104], w_in[:, 2104:3000],
        w_in[:, 512:528], z(112), w_in[:, 1808:1848], z(88)], axis=1).astype(BF16)
    w = dict(w_mix=w_mix, w_gates=w_in[:, 3000:7096].astype(BF16),
             w_br=p["w_br"][l].astype(BF16), w_o=p["w_o"][l].astype(BF16),
             w_ff1=p["w_ff1"][l].astype(BF16), w_ff2=p["w_ff2"][l].astype(BF16),
             norm1_g=p["norm1_g"][l][None, :], norm2_g=p["norm2_g"][l][None, :])
    a2 = jnp.zeros((128, 128), F32).at[0:GLA_RANK].set(p["gla_a2"][l])
    w["gla_a2h"], w["gla_a2l"] = _hilo(a2)
    w["gla_ab"] = p["gla_ab"][l][None, :]
    w["gla_ng"] = jnp.tile(p["gla_ng"][l], GLA_H)[None, :]
    w["att_qg"] = jnp.tile(p["att_qg"][l], ATT_H)[None, :]
    w["att_kg"] = jnp.tile(p["att_kg"][l], ATT_H)[None, :]
    dt = jnp.exp(p["s5_log_dt"][l])[:, None]
    lr = jnp.minimum(p["s5_a_re"][l], -1e-4)
    li = p["s5_a_im"][l]
    mag = jnp.exp(lr * dt)
    abr, abi = mag * jnp.cos(li * dt), mag * jnp.sin(li * dt)
    den = lr * lr + li * li
    fr = ((abr - 1.0) * lr + abi * li) / den
    fi = (abi * lr - (abr - 1.0) * li) / den
    b_re, b_im = p["s5_b_re"][l], p["s5_b_im"][l]
    bbr = fr[..., None] * b_re - fi[..., None] * b_im
    bbi = fr[..., None] * b_im + fi[..., None] * b_re
    bmat = jnp.concatenate([_blockdiag(bbr.transpose(0, 2, 1)), _blockdiag(bbi.transpose(0, 2, 1))], axis=1)
    w["s5_bh"] = bmat.astype(BF16)
    w["s5_a"] = jnp.concatenate([abr.reshape(1, -1), abi.reshape(1, -1)], axis=1)
    w["s5_c"] = jnp.concatenate([_blockdiag(p["s5_c_re"][l].transpose(0, 2, 1)),
                                 -_blockdiag(p["s5_c_im"][l].transpose(0, 2, 1))], axis=0).astype(BF16)
    w["s5_d"] = p["s5_d"][l].reshape(1, -1)
    w["s5_gw"] = p["s5_glu_w"][l].astype(BF16)
    w["s5_gb"] = p["s5_glu_b"][l][None, :]
    w["rw_mu"] = p["rw_mu"][l][None, :]
    w["rw_w0"] = p["rw_w0"][l][None, :]
    w["rw_a0"] = p["rw_a0"][l][None, :]
    lo = jnp.zeros((128, 256), F32)
    w["rw_w2h"], w["rw_w2l"] = _hilo(lo.at[0:RW_WR].set(p["rw_w2"][l]))
    w["rw_a2h"], w["rw_a2l"] = _hilo(lo.at[RW_WR:RW_WR + RW_AR].set(p["rw_a2"][l]))
    w["rw_g2h"], w["rw_g2l"] = _hilo(lo.at[RW_WR + RW_AR:128].set(p["rw_g2"][l]))
    for nm in ("rw_kk", "rw_ka", "rw_rk", "rw_ng"):
        w[nm] = p[nm][l][None, :]
    return w


def _mix_and_ffn(x, mod, w, consts, P, o_att, st_gla0, st_s50, st_rw0, prev, batch, seq, lvalid, tm, tiles):
    o_gla, st_gla = _gla(P, st_gla0, w, consts, batch, seq, tiles["gla"], tiles["chunk"], lvalid)
    if lvalid == 1:
        o1, st1 = _s5_step(P.reshape(batch, seq, NP_COLS)[:, 0], st_s50[:, 0], w)
        o_s5 = jnp.pad(o1[:, None], ((0, 0), (0, seq - 1), (0, 0))).reshape(batch * seq, 256)
        st_s5 = st1[:, None]
    else:
        o_s5, st_s5 = _s5(P, st_s50, w, batch, seq, tiles["s5"], lvalid)
    o_rw, st_rw = _rwkv(P.reshape(batch, seq, NP_COLS), st_rw0, prev, w, consts, tiles["nb"], tiles["rw"], lvalid)
    x = _merge(x, mod, w["norm1_g"], (o_gla, o_att, o_s5, o_rw.reshape(batch * seq, 256)), w, tm, seq)
    x = _ffn(x, mod, w["norm2_g"], w, tm, seq)
    return x, st_gla, st_s5, st_rw


def _gla_state_out(st):
    n = st.shape[0]
    s = st.reshape(n, GLA_H, GLA_DV, GLA_H, GLA_DK)
    s = jnp.stack([s[:, h, :, h, :] for h in range(GLA_H)], axis=1)
    return s.transpose(0, 1, 3, 2)


def _gla_state_in(s):
    eye = jnp.eye(GLA_H, dtype=s.dtype)
    n = s.shape[0]
    return jnp.einsum('nhkv,hg->nhvgk', s, eye).reshape(n, GLA_H * GLA_DV, GLA_H * GLA_DK)


def _rw_state_out(st):
    n = st.shape[0]
    return st.reshape(n, RW_N, RW_H, RW_N).transpose(0, 2, 1, 3)


def _rw_state_in(s):
    n = s.shape[0]
    return s.transpose(0, 2, 1, 3).reshape(n, RW_N, RW_H * RW_N)


def _forward(x_prompt, x_sample, c_prompt, c_sample, cache_k, cache_v, cache_idx, state_gla,
             state_s5_re, state_s5_im, state_rwkv, state_shift, page_table, p):
    B, S, _ = x_prompt.shape
    N = x_sample.shape[0]
    depth = p["w_in"].shape[0]
    past = page_table.shape[1] * PAGE_SIZE
    tm_p = min(512, S)
    tm_s = min(128, N * SAMPLE_PAD)
    tiles_p = dict(gla=min(256, S), chunk=16, s5=min(256, S), rw=min(128, S), nb=min(4, B))
    tiles_s = dict(gla=min(128, N * SAMPLE_PAD), chunk=SAMPLE_PAD, s5=SAMPLE_PAD, rw=SAMPLE_PAD, nb=8)
    tq = min(128, S)
    consts = _constants({(t["gla"], t["chunk"]) for t in (tiles_p, tiles_s)})
    mod_all = _modulation(jnp.concatenate([c_prompt, c_sample], axis=0), p["ada_w"], p["ada_b"])
    tabs_p = _rope_tables(jnp.arange(S), True)
    tabs_s = _rope_tables(jnp.full((N * SAMPLE_PAD,), past), False)
    ck = cache_k.transpose(0, 1, 3, 4, 2).reshape(*cache_k.shape[:2], ATT_H * HD, PAGE_SIZE)
    cv = cache_v.transpose(0, 1, 3, 4, 2).reshape(*cache_v.shape[:2], ATT_H * HD, PAGE_SIZE)
    ci = cache_idx.transpose(0, 1, 3, 2)

    xp = x_prompt.reshape(B * S, D_MODEL)
    xs = jnp.pad(x_sample, ((0, 0), (0, SAMPLE_PAD - 1), (0, 0))).reshape(N * SAMPLE_PAD, D_MODEL)
    outs_p, outs_s = [], []
    for l in range(depth):
        w = _layer_weights(l, p)
        mod = mod_all[l, :B].reshape(B, 1, 6 * D_MODEL)
        P = _inproj(xp, mod, w["norm1_g"], w["w_mix"], tm_p, S)
        qn, kn, qir, kwr, qh, kh, vh, qih, kib = _dsa_prep(P, tabs_p, consts, w["att_qg"], w["att_kg"], tm_p, S)
        o_att = _dsa_attn(qh, kh, vh, qih, kwr, kib, B, S, tq)
        xp, st_gla, st_s5, st_rw = _mix_and_ffn(
            xp, mod, w, consts, P, o_att,
            jnp.zeros((B, 256, 128), F32), jnp.zeros((B, 1, 2 * S5_N), F32),
            jnp.zeros((B, RW_N, 256), F32), jnp.zeros((B, 1, RW_COLS), F32), B, S, S, tm_p, tiles_p)
        P3 = P.reshape(B, S, NP_COLS)
        outs_p.append((kn.reshape(B, S, ATT_H, HD), P3[:, :, C_DV:C_DV + 256].reshape(B, S, ATT_H, HD),
                       kwr.reshape(B, S, 128)[:, :, :IDX_D], _gla_state_out(st_gla),
                       st_s5[:, 0, :S5_N].reshape(B, S5_G, S5_P), st_s5[:, 0, S5_N:].reshape(B, S5_G, S5_P),
                       _rw_state_out(st_rw), P3[:, S - 1, C_RW:C_RW + RW_COLS]))
        mod = jnp.repeat(mod_all[l, B:], SAMPLE_PAD, axis=0)
        P = _inproj(xs, mod, w["norm1_g"], w["w_mix"], tm_s, SAMPLE_PAD)
        qn, kn, qir, kwr = _dsa_prep(P, tabs_s, consts, w["att_qg"], w["att_kg"], tm_s, SAMPLE_PAD)[:4]
        first = lambda a: a.reshape(N, SAMPLE_PAD, a.shape[-1])[:, 0]
        qn1, kn1, qir1, kwr1, P1 = first(qn), first(kn), first(qir), first(kwr), first(P)
        v1 = P1[:, C_DV:C_DV + 256]
        scores = _ds_scores(page_table, qir1.reshape(N, IDX_H, IDX_D),
                            kwr1[:, IDX_D:IDX_D + IDX_H].reshape(N, IDX_H, 1),
                            kwr1[:, :IDX_D].reshape(N, 1, IDX_D), ci, l)
        ncols = scores.shape[1] * 128
        sel = _ds_select(scores.reshape(N, ncols), min(TOPK_MAX, (past + 1) // 4), past + 1)
        o1 = _ds_attn(page_table, qn1.reshape(N, 1, 256), kn1.reshape(N, 1, 256), v1.reshape(N, 1, 256),
                      sel.reshape(N, ncols // 128, 128), ck, cv, l, consts["hm8"])
        o_att = jnp.pad(o1, ((0, 0), (0, SAMPLE_PAD - 1), (0, 0))).reshape(N * SAMPLE_PAD, 256)
        x0 = jnp.concatenate([state_s5_re[l].reshape(N, 1, S5_N), state_s5_im[l].reshape(N, 1, S5_N)], axis=2)
        xs, st_gla, st_s5, st_rw = _mix_and_ffn(
            xs, mod, w, consts, P, o_att, _gla_state_in(state_gla[l]), x0, _rw_state_in(state_rwkv[l]),
            state_shift[l].reshape(N, 1, RW_COLS), N, SAMPLE_PAD, 1, tm_s, tiles_s)
        outs_s.append((kn1.reshape(N, 1, ATT_H, HD), v1.reshape(N, 1, ATT_H, HD), kwr1[:, None, :IDX_D],
                       _gla_state_out(st_gla), st_s5[:, 0, :S5_N].reshape(N, S5_G, S5_P),
                       st_s5[:, 0, S5_N:].reshape(N, S5_G, S5_P), _rw_state_out(st_rw),
                       P1[:, C_RW:C_RW + RW_COLS]))
    yp = xp.reshape(B, S, D_MODEL)
    ys = xs.reshape(N, SAMPLE_PAD, D_MODEL)[:, 0:1]
    stack = lambda lst, i: jnp.stack([s[i] for s in lst])
    return (yp, ys) + tuple(stack(outs_p, i) for i in range(8)) + tuple(stack(outs_s, i) for i in range(8))


def kernel(x_prompt, x_sample, c_prompt, c_sample, cache_k, cache_v, cache_idx, state_gla, state_s5_re, state_s5_im, state_rwkv, state_shift, page_table, ada_w, ada_b, norm1_g, norm2_g, w_in, gla_a2, gla_ab, gla_ng, att_qg, att_kg, s5_a_re, s5_a_im, s5_log_dt, s5_b_re, s5_b_im, s5_c_re, s5_c_im, s5_d, s5_glu_w, s5_glu_b, rw_mu, rw_w0, rw_w2, rw_a0, rw_a2, rw_g2, rw_kk, rw_ka, rw_rk, rw_ng, w_br, w_o, w_ff1, w_ff2):
    p = dict(ada_w=ada_w, ada_b=ada_b, norm1_g=norm1_g, norm2_g=norm2_g, w_in=w_in, gla_a2=gla_a2,
             gla_ab=gla_ab, gla_ng=gla_ng, att_qg=att_qg, att_kg=att_kg, s5_a_re=s5_a_re, s5_a_im=s5_a_im,
             s5_log_dt=s5_log_dt, s5_b_re=s5_b_re, s5_b_im=s5_b_im, s5_c_re=s5_c_re, s5_c_im=s5_c_im,
             s5_d=s5_d, s5_glu_w=s5_glu_w, s5_glu_b=s5_glu_b, rw_mu=rw_mu, rw_w0=rw_w0, rw_w2=rw_w2,
             rw_a0=rw_a0, rw_a2=rw_a2, rw_g2=rw_g2, rw_kk=rw_kk, rw_ka=rw_ka, rw_rk=rw_rk, rw_ng=rw_ng,
             w_br=w_br, w_o=w_o, w_ff1=w_ff1, w_ff2=w_ff2)
    return _forward(x_prompt, x_sample, c_prompt, c_sample, cache_k, cache_v, cache_idx, state_gla,
                    state_s5_re, state_s5_im, state_rwkv, state_shift, page_table, p)
```

```python
import functools
import math

import numpy as np
import jax
import jax.numpy as jnp
from jax import lax
from jax.experimental import pallas as pl
from jax.experimental.pallas import tpu as pltpu

F32 = jnp.float32
BF16 = jnp.bfloat16
I32 = jnp.int32

D_MODEL = 1024
BR_W = 256
GLA_H, GLA_DK, GLA_DV, GLA_RANK, GLA_TAU = 4, 32, 64, 16, 16.0
ATT_H, HD, ROT = 4, 64, 16
IDX_H, IDX_D, IDX_ROT = 8, 32, 8
TOPK_MAX = 256
ROPE_THETA = 500000.0
S5_G, S5_P, S5_CH = 16, 64, 16
S5_N = S5_G * S5_P
RW_H, RW_N, RW_WR, RW_AR, RW_GR = 4, 64, 32, 32, 64
RW_COLS = 896
D_FF = 4096
EPS = 1e-6
RW_GN_EPS = 64e-5
PAGE_SIZE = 128
INT_MIN = -(2 ** 31)
SAMPLE_PAD = 8

C_GV, C_GR, C_GQK = 0, 256, 512
C_DQ, C_DK, C_DV, C_DQI = 768, 1024, 1280, 1536
C_S5 = 1792
C_RW = 2048
C_RWLO = 2816
C_GA = 2944
C_DKW = 3072
NP_COLS = 3200
VMEM_LIMIT = 56 * 1024 * 1024


def _cparams(n_axes):
    return pltpu.CompilerParams(dimension_semantics=("arbitrary",) * n_axes,
                                vmem_limit_bytes=VMEM_LIMIT)


def _split_dot(x, w, terms):
    acc = None
    r = x
    for i in range(terms):
        hi = r.astype(BF16)
        d = jnp.dot(hi, w, preferred_element_type=F32)
        acc = d if acc is None else acc + d
        if i + 1 < terms:
            r = r - hi.astype(F32)
    return acc


def _dot3(x, w_hi, w_lo):
    x_hi = x.astype(BF16)
    x_lo = (x - x_hi.astype(F32)).astype(BF16)
    return (jnp.dot(x_hi, w_hi, preferred_element_type=F32)
            + jnp.dot(x_hi, w_lo, preferred_element_type=F32)
            + jnp.dot(x_lo, w_hi, preferred_element_type=F32))


def _hilo(w):
    hi = w.astype(BF16)
    return hi, (w - hi.astype(F32)).astype(BF16)


def _sigmoid(x):
    return 1.0 / (1.0 + jnp.exp(-x))


def _softplus(x):
    return jnp.maximum(x, 0.0) + jnp.log1p(jnp.exp(-jnp.abs(x)))


def _norm_mod(x, g, sc, sh):
    ms = jnp.mean(x * x, axis=-1, keepdims=True)
    return (x * lax.rsqrt(ms + EPS) * g) * (1.0 + sc) + sh


def _mod_spec(mod, j, tm, seq):
    if mod.ndim == 3:
        return pl.BlockSpec((None, 1, D_MODEL), lambda i: ((i * tm) // seq, 0, j))
    return pl.BlockSpec((tm, D_MODEL), lambda i: (i, j))


def _const_spec(a):
    nd = a.ndim
    return pl.BlockSpec(a.shape, lambda *_: (0,) * nd)


def _mod_kernel(c_ref, w_ref, b_ref, o_ref):
    o_ref[...] = jnp.dot(c_ref[...], w_ref[...].astype(BF16), preferred_element_type=F32) + b_ref[...]


def _modulation(c_all, ada_w, ada_b):
    depth = ada_w.shape[0]
    rows = c_all.shape[0]
    tn = 1536
    return pl.pallas_call(
        _mod_kernel,
        grid=(depth, 6 * D_MODEL // tn),
        in_specs=[pl.BlockSpec((rows, D_MODEL), lambda l, j: (0, 0)),
                  pl.BlockSpec((None, D_MODEL, tn), lambda l, j: (l, 0, j)),
                  pl.BlockSpec((None, 1, tn), lambda l, j: (l, 0, j))],
        out_specs=pl.BlockSpec((None, rows, tn), lambda l, j: (l, 0, j)),
        out_shape=jax.ShapeDtypeStruct((depth, rows, 6 * D_MODEL), F32),
        compiler_params=_cparams(2), name="modulation",
    )(c_all.astype(BF16), ada_w, ada_b.reshape(depth, 1, 6 * D_MODEL))


def _inproj_kernel(x_ref, sc_ref, sh_ref, g_ref, w_ref, o_ref):
    h = _norm_mod(x_ref[...], g_ref[...], sc_ref[...], sh_ref[...])
    o_ref[...] = jnp.dot(h.astype(BF16), w_ref[...], preferred_element_type=F32)


def _inproj(x, mod, g, w_mix, tm, seq):
    rows = x.shape[0]
    return pl.pallas_call(
        _inproj_kernel,
        grid=(rows // tm,),
        in_specs=[pl.BlockSpec((tm, D_MODEL), lambda i: (i, 0)),
                  _mod_spec(mod, 1, tm, seq), _mod_spec(mod, 0, tm, seq),
                  _const_spec(g), _const_spec(w_mix)],
        out_specs=pl.BlockSpec((tm, NP_COLS), lambda i: (i, 0)),
        out_shape=jax.ShapeDtypeStruct((rows, NP_COLS), F32),
        compiler_params=_cparams(1), name="inproj",
    )(x, mod, mod, g, w_mix)


def _rope_apply(x, cos, sn, left, shift):
    n = x.shape[-1]
    rot = jnp.where(left > 0.0, pltpu.roll(x, n - shift, 1), pltpu.roll(x, shift, 1))
    return x * cos + rot * sn


def _dsa_prep_kernel(q_ref, k_ref, v_ref, qi_ref, kw_ref, cq_ref, sq_ref, ci_ref, si_ref, ckw_ref, skw_ref,
                     lq_ref, li_ref, lkw_ref, qg_ref, kg_ref, wavg_ref,
                     qn_ref, kn_ref, qir_ref, kwr_ref, qh_ref, kh_ref, vh_ref, qih_ref, kib_ref):
    wavg = wavg_ref[...]

    def headnorm(x, g):
        ms = _split_dot(x * x, wavg, 3)
        return x * lax.rsqrt(ms + EPS) * g

    cq, sq, lq = cq_ref[...], sq_ref[...], lq_ref[...]
    qn = _rope_apply(headnorm(q_ref[...], qg_ref[...]), cq, sq, lq, ROT // 2)
    kn = _rope_apply(headnorm(k_ref[...], kg_ref[...]), cq, sq, lq, ROT // 2)
    qir = _rope_apply(qi_ref[...], ci_ref[...], si_ref[...], li_ref[...], IDX_ROT // 2)
    kwr = _rope_apply(kw_ref[...], ckw_ref[...], skw_ref[...], lkw_ref[...], IDX_ROT // 2)
    qn_ref[...] = qn
    kn_ref[...] = kn
    qir_ref[...] = qir
    kwr_ref[...] = kwr
    v = v_ref[...]
    for h in range(ATT_H):
        hs = slice(h * HD, (h + 1) * HD)
        qh_ref[h] = qn[:, hs].astype(BF16)
        kh_ref[h] = kn[:, hs].astype(BF16)
        vh_ref[h] = v[:, hs].astype(BF16)
    for h in range(IDX_H):
        qih_ref[h] = qir[:, h * IDX_D:(h + 1) * IDX_D].astype(BF16)
    kib_ref[...] = kwr.astype(BF16)


def _dsa_prep(P, tabs, consts, qg, kg, tm, seq):
    rows = P.shape[0]
    nt = seq // tm if tabs["periodic"] else None

    def tab_spec(w):
        if tabs["periodic"]:
            return pl.BlockSpec((tm, w), lambda i: (i % nt, 0))
        return pl.BlockSpec((tm, w), lambda i: (i, 0))

    def col(off, w):
        return pl.BlockSpec((tm, w), lambda i: (i, off // w))

    out256 = jax.ShapeDtypeStruct((rows, 256), F32)
    heads = lambda n, d: (pl.BlockSpec((n, tm, d), lambda i: (0, i, 0)), jax.ShapeDtypeStruct((n, rows, d), BF16))
    hq, hi = heads(ATT_H, HD), heads(IDX_H, IDX_D)
    return pl.pallas_call(
        _dsa_prep_kernel,
        grid=(rows // tm,),
        in_specs=[col(C_DQ, 256), col(C_DK, 256), col(C_DV, 256), col(C_DQI, 256), col(C_DKW, 128),
                  tab_spec(256), tab_spec(256), tab_spec(256), tab_spec(256), tab_spec(128), tab_spec(128),
                  _const_spec(consts["left_q"]), _const_spec(consts["left_i"]), _const_spec(consts["left_kw"]),
                  _const_spec(qg), _const_spec(kg), _const_spec(consts["wavg"])],
        out_specs=[pl.BlockSpec((tm, 256), lambda i: (i, 0))] * 3 + [pl.BlockSpec((tm, 128), lambda i: (i, 0))]
                  + [hq[0], hq[0], hq[0], hi[0], pl.BlockSpec((tm, 128), lambda i: (i, 0))],
        out_shape=[out256, out256, out256, jax.ShapeDtypeStruct((rows, 128), F32),
                   hq[1], hq[1], hq[1], hi[1], jax.ShapeDtypeStruct((rows, 128), BF16)],
        compiler_params=_cparams(1), name="dsa_prep",
    )(P, P, P, P, P, tabs["cq"], tabs["sq"], tabs["ci"], tabs["si"], tabs["ckw"], tabs["skw"],
      consts["left_q"], consts["left_i"], consts["left_kw"], qg, kg, consts["wavg"])


def _score_keys(scores, valid):
    s = jnp.where(scores == 0.0, 0.0, scores)
    bits = pltpu.bitcast(s, I32)
    key = bits ^ (jnp.right_shift(bits, 31) & 0x7FFFFFFF)
    return jnp.where(valid, key, INT_MIN)


def _topk_select(key_ref, k, col, side=None, n_side=0):
    rows, cols = key_ref.shape
    kf = float(k)
    nbits = max(1, int(math.ceil(math.log2(cols))))

    ng = 4 if rows % 32 == 0 else 1
    rg = rows // ng

    def count_ge(g, c):
        return jnp.sum(jnp.where(key_ref[g * rg:(g + 1) * rg, :] >= c, 1.0, 0.0), axis=-1, keepdims=True)

    bases = tuple(jnp.where(count_ge(g, jnp.zeros((rg, 1), I32)) >= kf, 0, INT_MIN).astype(I32)
                  for g in range(ng))

    def bit_step(i, bases):
        bit = lax.shift_left(jnp.int32(1), 30 - i)
        return tuple(jnp.where(count_ge(g, b | bit) >= kf, b | bit, b) for g, b in enumerate(bases))

    if side is None:
        bases = lax.fori_loop(0, 31, bit_step, bases, unroll=4)
    else:
        per = -(-31 // n_side)

        def outer(j, bases):
            side(j)
            for t in range(per):
                s = j * per + t
                bit = jnp.where(s <= 30, lax.shift_left(jnp.int32(1), jnp.maximum(30 - s, 0)), 0)
                bases = tuple(jnp.where(count_ge(g, b | bit) >= kf, b | bit, b) for g, b in enumerate(bases))
            return bases

        bases = lax.fori_loop(0, n_side, outer, bases)
    thr = bases[0] if ng == 1 else jnp.concatenate(bases, axis=0)
    key = key_ref[...]
    need = kf - jnp.sum(jnp.where(key > thr, 1.0, 0.0), axis=-1, keepdims=True)
    excess = jnp.sum(jnp.where(key == thr, 1.0, 0.0), axis=-1, keepdims=True) > need

    def pos_step(i, pos):
        cand = pos + lax.shift_left(jnp.int32(1), nbits - 1 - i)
        hit = jnp.where(key_ref[...] == thr, jnp.where(col < cand, 1.0, 0.0), 0.0)
        return jnp.where(jnp.sum(hit, axis=-1, keepdims=True) < need, cand, pos)

    n_iter = jnp.where(jnp.max(jnp.where(excess, 1, 0)) > 0, nbits, 0)
    pos = lax.fori_loop(0, n_iter, pos_step, jnp.zeros((rows, 1), I32))
    pos = jnp.where(excess, pos, cols)
    return jnp.where(key > thr, 1.0, jnp.where(key == thr, jnp.where(col <= pos, 1.0, 0.0), 0.0))


def _dsa_attn_kernel(qh_ref, qih_ref, kwq_ref, kh_ref, vh_ref, kib_ref, o_ref, key_ref, lg_ref,
                     *, tq, topk, nvar):
    it = pl.program_id(1)
    t0 = it * tq
    seq = kh_ref.shape[1]
    step = seq // nvar
    per = step // tq

    def body(klen):
        ki = kib_ref[0:klen, 0:IDX_D]
        kwq = kwq_ref[...]
        scores = jnp.zeros((tq, klen), F32)
        for h in range(IDX_H):
            s = lax.dot_general(qih_ref[h], ki, _LANES, preferred_element_type=F32)
            w = kwq[:, IDX_D + h:IDX_D + h + 1] * (IDX_D ** -0.5)
            scores = scores + w * jnp.maximum(s, 0.0)
        col = lax.broadcasted_iota(I32, (tq, klen), 1)
        row = t0 + lax.broadcasted_iota(I32, (tq, klen), 0)
        causal = col <= row
        keys = key_ref.at[:, pl.ds(0, klen)]
        keys[...] = _score_keys(scores, causal)

        def logits(h):
            lg_ref[h, :, 0:klen] = lax.dot_general(qh_ref[h], kh_ref[h, 0:klen, :], _LANES,
                                                   preferred_element_type=F32) * (HD ** -0.5)

        sel = jnp.where(causal, _topk_select(keys, topk, col, logits, ATT_H), 0.0) > 0.0

        def head(h):
            lg = jnp.where(sel, lg_ref[h, :, 0:klen], -jnp.inf)
            m = jnp.max(lg, axis=-1, keepdims=True)
            yield
            p = jnp.exp(lg - m)
            l = jnp.sum(p, axis=-1, keepdims=True)
            o = jnp.dot(p.astype(BF16), vh_ref[h, 0:klen, :], preferred_element_type=F32)
            yield
            o_ref[:, h * HD:(h + 1) * HD] = o / l

        chains = [head(h) for h in range(ATT_H)]
        while chains:
            chains = [ch for ch in chains if next(ch, True) is None]

    for var in range(nvar):
        pl.when(it // per == var)(functools.partial(body, (var + 1) * step))


def _dsa_attn(qh, kh, vh, qih, kwr, kib, batch, seq, tq):
    topk = min(TOPK_MAX, seq // 4)
    nq = seq // tq
    nvar = min(8, nq)
    while seq // nvar < topk:
        nvar //= 2
    return pl.pallas_call(
        functools.partial(_dsa_attn_kernel, tq=tq, topk=topk, nvar=nvar),
        grid=(batch, nq),
        in_specs=[pl.BlockSpec((ATT_H, tq, HD), lambda b, i: (0, b * nq + i, 0)),
                  pl.BlockSpec((IDX_H, tq, IDX_D), lambda b, i: (0, b * nq + i, 0)),
                  pl.BlockSpec((tq, 128), lambda b, i: (b * nq + i, 0)),
                  pl.BlockSpec((ATT_H, seq, HD), lambda b, i: (0, b, 0)),
                  pl.BlockSpec((ATT_H, seq, HD), lambda b, i: (0, b, 0)),
                  pl.BlockSpec((seq, 128), lambda b, i: (b, 0))],
        out_specs=pl.BlockSpec((tq, 256), lambda b, i: (b * nq + i, 0)),
        out_shape=jax.ShapeDtypeStruct((batch * seq, 256), F32),
        scratch_shapes=[pltpu.VMEM((tq, seq), I32), pltpu.VMEM((ATT_H, tq, seq), F32)],
        compiler_params=_cparams(2), name="dsa_attn",
    )(qh, qih, kwr, kh, vh, kib)


def _split3_rhs(m, x):
    hi = x.astype(BF16)
    r = x - hi.astype(F32)
    mid = r.astype(BF16)
    lo = (r - mid.astype(F32)).astype(BF16)
    return (jnp.dot(m, hi, preferred_element_type=F32) + jnp.dot(m, mid, preferred_element_type=F32)
            + jnp.dot(m, lo, preferred_element_type=F32))


_LANES = (((1,), (1,)), ((), ()))
_ROWS = (((0,), (0,)), ((), ()))


def _gla_kernel(v_ref, r_ref, qk_ref, a_ref, st0_ref, a2h_ref, a2l_ref, ab_ref, ng_ref,
                tri_ref, ones_ref, amask_ref, hm128_ref, hm256_ref, bd_ref, wavg_ref,
                o_ref, st_ref, *, tile, chunk, lvalid, independent):
    if not independent:
        @pl.when(pl.program_id(1) == 0)
        def _():
            st_ref[...] = st0_ref[...]

    nch = tile // chunk
    z = _dot3(a_ref[...], a2h_ref[...], a2l_ref[...]) + ab_ref[...]
    la = (jnp.minimum(z, 0.0) - jnp.log1p(jnp.exp(-jnp.abs(z)))) * (1.0 / GLA_TAU)
    q = qk_ref[:, 0:128] * (GLA_DK ** -0.5)
    k = qk_ref[:, 128:256]
    v = v_ref[...]
    if independent and lvalid < chunk:
        keep = (lax.broadcasted_iota(I32, (tile, 128), 0) % chunk) < lvalid
        la = jnp.where(keep, la, 0.0)
        k = jnp.where(keep, k, 0.0)
        v = jnp.where((lax.broadcasted_iota(I32, (tile, 256), 0) % chunk) < lvalid, v, 0.0)
    b = _split3_rhs(tri_ref[...], la)
    tot = _split3_rhs(ones_ref[...], la)
    qe = q * jnp.exp(b)
    kinv = (k * jnp.exp(-b)).astype(BF16)
    kd = (k * jnp.exp(tot - b)).astype(BF16)
    vb = v.astype(BF16)
    hm128 = hm128_ref[...]
    hm256 = hm256_ref[...]
    qblk = jnp.concatenate([qe[c * chunk:(c + 1) * chunk] * hm128[h:h + 1]
                            for c in range(nch) for h in range(GLA_H)], axis=0).astype(BF16)
    att = lax.dot_general(qblk, kinv, _LANES, preferred_element_type=F32) * amask_ref[...]
    intra = jnp.dot(att.astype(BF16), vb, preferred_element_type=F32)
    qeb = qe.astype(BF16)
    st = None if independent else st_ref[...]
    for c in range(nch):
        rows = slice(c * chunk, (c + 1) * chunk)
        if independent:
            st = st0_ref[c]
        o = lax.dot_general(qeb[rows], st.astype(BF16), _LANES, preferred_element_type=F32)
        for h in range(GLA_H):
            r0 = (c * GLA_H + h) * chunk
            o = o + intra[r0:r0 + chunk] * hm256[h:h + 1]
        o_ref[rows, :] = o
        upd = lax.dot_general(vb[rows], kd[rows], _ROWS, preferred_element_type=F32)
        st = st * jnp.exp(tot[c * chunk:c * chunk + 1]) + upd * bd_ref[...]
        if independent:
            st_ref[c] = st
    if not independent:
        st_ref[...] = st
    o = o_ref[...]
    ms = _split_dot(o * o, wavg_ref[...], 2)
    r = r_ref[...]
    o_ref[...] = o * lax.rsqrt(ms + EPS) * ng_ref[...] * (r * _sigmoid(r))


def _gla(P, st0, w, consts, batch, seq, tg, chunk, lvalid):
    independent = lvalid < seq
    if independent:
        assert seq == chunk
        nseq = tg // chunk
        nt, grid = 1, (batch // nseq, 1)
        st_spec = pl.BlockSpec((nseq, 256, 128), lambda b, j: (b, 0, 0))
    else:
        nt, grid = seq // tg, (batch, seq // tg)
        st_spec = pl.BlockSpec((None, 256, 128), lambda b, j: (b, 0, 0))

    def col(off, wd):
        return pl.BlockSpec((tg, wd), lambda b, j: (b * nt + j, off // wd))

    gm = consts["gla"][(tg, chunk)]
    cs = [w["gla_a2h"], w["gla_a2l"], w["gla_ab"], w["gla_ng"],
          gm["tri"], gm["ones"], gm["amask"], consts["hm128"], consts["hm8"], consts["bd"], consts["wavg"]]
    return pl.pallas_call(
        functools.partial(_gla_kernel, tile=tg, chunk=chunk, lvalid=lvalid, independent=independent),
        grid=grid,
        in_specs=[col(C_GV, 256), col(C_GR, 256), col(C_GQK, 256), col(C_GA, 128), st_spec]
                 + [_const_spec(a) for a in cs],
        out_specs=[pl.BlockSpec((tg, 256), lambda b, j: (b * nt + j, 0)), st_spec],
        out_shape=[jax.ShapeDtypeStruct((batch * seq, 256), F32),
                   jax.ShapeDtypeStruct((batch, 256, 128), F32)],
        compiler_params=_cparams(2), name="gla",
    )(P, P, P, P, st0, *cs)


def _gelu_tanh(x):
    return 0.5 * x * (1.0 + jnp.tanh(math.sqrt(2.0 / math.pi) * (x + 0.044715 * (x * x * x))))


def _s5_kernel(u_ref, x0_ref, a_ref, bh_ref, c_ref, d_ref, gw_ref, gb_ref,
               o_ref, xf_ref, st_s, bur_s, bui_s, xr_s, xi_s, *, tile, last_row):
    @pl.when(pl.program_id(1) == 0)
    def _():
        st_s[...] = x0_ref[...]

    u = u_ref[...]
    bu = jnp.dot(u.astype(BF16), bh_ref[...], preferred_element_type=F32)
    bur_s[...] = bu[:, 0:S5_N]
    bui_s[...] = bu[:, S5_N:2 * S5_N]
    ar = a_ref[:, 0:S5_N]
    ai = a_ref[:, S5_N:2 * S5_N]

    def step(t, carry):
        xr, xi = carry
        row = pl.ds(t, 1)
        nr = ar * xr - ai * xi + bur_s[row, :]
        ni = ar * xi + ai * xr + bui_s[row, :]
        xr_s[row, :] = nr
        xi_s[row, :] = ni
        return nr, ni

    xr, xi = lax.fori_loop(0, tile, step, (st_s[:, 0:S5_N], st_s[:, S5_N:2 * S5_N]), unroll=8)
    st_s[:, 0:S5_N] = xr
    st_s[:, S5_N:2 * S5_N] = xi
    y = (jnp.dot(xr_s[...].astype(BF16), c_ref[0:S5_N, :], preferred_element_type=F32)
         + jnp.dot(xi_s[...].astype(BF16), c_ref[S5_N:2 * S5_N, :], preferred_element_type=F32)
         + d_ref[...] * u)
    z = _gelu_tanh(y)
    gate = jnp.dot(z.astype(BF16), gw_ref[...], preferred_element_type=F32) + gb_ref[...]
    o_ref[...] = z * _sigmoid(gate)
    xf_ref[:, 0:S5_N] = xr_s[last_row:last_row + 1, :]
    xf_ref[:, S5_N:2 * S5_N] = xi_s[last_row:last_row + 1, :]


def _s5(P, x0, w, batch, seq, tile, lvalid):
    nt = seq // tile
    last_row = (lvalid - 1) % tile
    cs = [w["s5_a"], w["s5_bh"], w["s5_c"], w["s5_d"], w["s5_gw"], w["s5_gb"]]
    return pl.pallas_call(
        functools.partial(_s5_kernel, tile=tile, last_row=last_row),
        grid=(batch, nt),
        in_specs=[pl.BlockSpec((tile, 256), lambda b, j: (b * nt + j, C_S5 // 256)),
                  pl.BlockSpec((None, 1, 2 * S5_N), lambda b, j: (b, 0, 0))] + [_const_spec(a) for a in cs],
        out_specs=[pl.BlockSpec((tile, 256), lambda b, j: (b * nt + j, 0)),
                   pl.BlockSpec((None, 1, 2 * S5_N), lambda b, j: (b, 0, 0))],
        out_shape=[jax.ShapeDtypeStruct((batch * seq, 256), F32),
                   jax.ShapeDtypeStruct((batch, 1, 2 * S5_N), F32)],
        scratch_shapes=[pltpu.VMEM((1, 2 * S5_N), F32)] + [pltpu.VMEM((tile, S5_N), F32)] * 4,
        compiler_params=_cparams(2), name="s5",
    )(P, x0, *cs)


def _s5_step_kernel(u_ref, x0_ref, a_ref, bh_ref, c_ref, d_ref, gw_ref, gb_ref, o_ref, xf_ref):
    u = u_ref[...]
    bu = jnp.dot(u.astype(BF16), bh_ref[...], preferred_element_type=F32)
    ar, ai = a_ref[:, 0:S5_N], a_ref[:, S5_N:2 * S5_N]
    xr0, xi0 = x0_ref[:, 0:S5_N], x0_ref[:, S5_N:2 * S5_N]
    xr = ar * xr0 - ai * xi0 + bu[:, 0:S5_N]
    xi = ar * xi0 + ai * xr0 + bu[:, S5_N:2 * S5_N]
    y = (jnp.dot(xr.astype(BF16), c_ref[0:S5_N, :], preferred_element_type=F32)
         + jnp.dot(xi.astype(BF16), c_ref[S5_N:2 * S5_N, :], preferred_element_type=F32)
         + d_ref[...] * u)
    z = _gelu_tanh(y)
    gate = jnp.dot(z.astype(BF16), gw_ref[...], preferred_element_type=F32) + gb_ref[...]
    o_ref[...] = z * _sigmoid(gate)
    xf_ref[:, 0:S5_N] = xr
    xf_ref[:, S5_N:2 * S5_N] = xi


def _s5_step(P1, x0, w):
    n = P1.shape[0]
    cs = [w["s5_a"], w["s5_bh"], w["s5_c"], w["s5_d"], w["s5_gw"], w["s5_gb"]]
    return pl.pallas_call(
        _s5_step_kernel,
        grid=(1,),
        in_specs=[pl.BlockSpec((n, 256), lambda i: (0, C_S5 // 256)),
                  pl.BlockSpec((n, 2 * S5_N), lambda i: (0, 0))] + [_const_spec(a) for a in cs],
        out_specs=[pl.BlockSpec((n, 256), lambda i: (0, 0)), pl.BlockSpec((n, 2 * S5_N), lambda i: (0, 0))],
        out_shape=[jax.ShapeDtypeStruct((n, 256), F32), jax.ShapeDtypeStruct((n, 2 * S5_N), F32)],
        compiler_params=_cparams(1), name="s5_step",
    )(P1, x0, *cs)


RW_CH = 16
RW_SC = 4 * RW_CH


def _rwkv_chunked(sf_ref, r_s, k_s, v_s, al_s, be_s, lw_s, y_s, sbd_s, hm, tri, onesb, strict, incl, bd,
                  nb, tile):
    nh = RW_H

    def blk(x):
        return jnp.concatenate([x[RW_CH * c:RW_CH * (c + 1)] * hm[h:h + 1]
                                for c in range(4) for h in range(nh)], axis=0)

    def rep(x):
        return jnp.concatenate([x[RW_CH * c:RW_CH * (c + 1)] for c in range(4) for _ in range(nh)], axis=0)

    def stack_heads(x):
        return jnp.concatenate([x[RW_CH * c:RW_CH * (c + 1), RW_N * h:RW_N * (h + 1)]
                                for c in range(4) for h in range(nh)], axis=0)

    def mm(a, b):
        return jnp.dot(a, b, preferred_element_type=F32)

    for b in range(nb):
        sbd_s[b] = sf_ref[b]

    def superchunk(sc, carry):
        r0 = pl.multiple_of(sc * RW_SC, RW_SC)
        rows = pl.ds(r0, RW_SC)

        def chain(b):
            lw = lw_s[b, rows, :]
            cum = _split3_rhs(tri, lw)
            tot = _split3_rhs(onesb, lw)
            rr, kx, vv = r_s[b, rows, :], k_s[b, rows, :], v_s[b, rows, :]
            al, be = al_s[b, rows, :], be_s[b, rows, :]
            pinv = jnp.exp(-cum)
            pend = jnp.exp(tot - cum)
            ab = al * jnp.exp(cum - lw)
            rb = rr * jnp.exp(cum)
            bt, kt, bp, kp = be * pinv, kx * pinv, be * pend, kx * pend
            ablk = blk(ab)
            lhs = jnp.concatenate([ablk, blk(rb)], axis=0).astype(BF16)
            rhs = jnp.concatenate([rep(bt), rep(kt)], axis=0).astype(BF16)
            g = lax.dot_general(lhs, rhs, _LANES, preferred_element_type=F32)
            yield
            mb = g[0:256, 0:256] * strict
            mk = g[0:256, 256:512] * strict
            myb = (g[256:512, 0:256] * incl).astype(BF16)
            myk = (g[256:512, 256:512] * incl).astype(BF16)
            vst = stack_heads(vv).astype(BF16)
            w0 = mm(mk.astype(BF16), vst)
            y0 = mm(myk, vst)
            u, mp = mb, mb
            for _ in range(3):
                mpb = mp.astype(BF16)
                mp = mm(mpb, mpb)
                yield
                u = u + mp + mm(u.astype(BF16), mp.astype(BF16))
                yield
            ub = u.astype(BF16)
            ab1 = (ablk + mm(ub, ablk.astype(BF16))).astype(BF16)
            rblk = lhs[256:512]
            z0 = w0 + mm(ub, w0.astype(BF16))
            bpk = jnp.concatenate([blk(bp), blk(kp)], axis=1).astype(BF16)
            yield
            S = sbd_s[b]
            nr = nh * RW_CH
            for c in range(4):
                d0 = nr * c
                lc = jnp.concatenate([ab1[d0:d0 + nr], rblk[d0:d0 + nr]], axis=0)
                s_hi = S.astype(BF16)
                s_lo = (S - s_hi.astype(F32)).astype(BF16)
                x = (lax.dot_general(lc, s_hi, _LANES, preferred_element_type=F32)
                     + lax.dot_general(lc, s_lo, _LANES, preferred_element_type=F32))
                yield
                zst = x[0:nr] + z0[d0:d0 + nr]
                zb = zst.astype(BF16)
                yst = x[nr:2 * nr] + y0[d0:d0 + nr] + mm(myb[d0:d0 + nr, d0:d0 + nr], zb)
                y_s[b, pl.ds(r0 + RW_CH * c, RW_CH), :] = jnp.concatenate(
                    [yst[RW_CH * h:RW_CH * (h + 1)] for h in range(nh)], axis=1)
                upd = (lax.dot_general(zb, bpk[d0:d0 + nr, 0:256], _ROWS, preferred_element_type=F32)
                       + lax.dot_general(vst[d0:d0 + nr], bpk[d0:d0 + nr, 256:512], _ROWS,
                                         preferred_element_type=F32))
                S = S * jnp.exp(tot[RW_CH * c:RW_CH * c + 1]) + upd
                yield
            sbd_s[b] = S

        chains = [chain(b) for b in range(nb)]
        while chains:
            chains = [ch for ch in chains if next(ch, True) is None]
        return carry

    lax.fori_loop(0, tile // RW_SC, superchunk, 0)
    for b in range(nb):
        sf_ref[b] = sbd_s[b]


def _rwkv_kernel(r_ref, k_ref, v_ref, lo_ref, s0_ref, prev_ref, mu_ref, w0_ref, a0_ref,
                 w2h_ref, w2l_ref, a2h_ref, a2l_ref, g2h_ref, g2l_ref, kkp_ref, ka_ref, rk_ref, ng_ref,
                 wones_ref, wavg_ref, idt_ref, hm_ref, tri_ref, ones_ref, strict_ref, incl_ref, bd_ref,
                 o_ref, sf_ref, r_s, k_s, v_s, kk_s, ka_s, w_s, y_s, g_s, bo_s, prev_s, sbd_s,
                 *, nb, tile, nsteps, chunked):
    @pl.when(pl.program_id(1) == 0)
    def _():
        sf_ref[...] = s0_ref[...]
        prev_s[...] = prev_ref[...]

    wones = wones_ref[...]
    wavg = wavg_ref[...]
    idt = idt_ref[...]
    row0_256 = lax.broadcasted_iota(I32, (tile, 256), 0) == 0
    row0_128 = lax.broadcasted_iota(I32, (tile, 128), 0) == 0

    def shift_mix(p, prev_row, mu, row0):
        sh = jnp.where(row0, prev_row, pltpu.roll(p, 1, 0))
        return p + (sh - p) * mu

    for b in range(nb):
        pr, pk, pv, plo = r_ref[b], k_ref[b], v_ref[b], lo_ref[b]
        r = shift_mix(pr, prev_s[b, :, 0:256], mu_ref[:, 0:256], row0_256)
        k = shift_mix(pk, prev_s[b, :, 256:512], mu_ref[:, 256:512], row0_256)
        v = shift_mix(pv, prev_s[b, :, 512:768], mu_ref[:, 512:768], row0_256)
        lo = shift_mix(plo, prev_s[b, :, 768:896], mu_ref[:, 768:896], row0_128)
        prev_s[b, :, 0:256] = pr[tile - 1:tile, :]
        prev_s[b, :, 256:512] = pk[tile - 1:tile, :]
        prev_s[b, :, 512:768] = pv[tile - 1:tile, :]
        prev_s[b, :, 768:896] = plo[tile - 1:tile, :]
        wl = w0_ref[...] + _dot3(jnp.tanh(lo), w2h_ref[...], w2l_ref[...])
        w = -_softplus(-wl) - 0.5
        a = _sigmoid(a0_ref[...] + _split_dot(lo, a2h_ref[...], 2))
        g = jnp.dot(_sigmoid(lo).astype(BF16), g2h_ref[...], preferred_element_type=F32)
        kk = k * kkp_ref[...]
        kk = kk * lax.rsqrt(_split_dot(kk * kk, wones, 2) + EPS)
        k2 = k * (1.0 + (a - 1.0) * ka_ref[...])
        bonus = _split_dot(r * k2 * rk_ref[...], wones, 2) * v
        r_s[b] = r
        k_s[b] = k2
        v_s[b] = v
        kk_s[b] = -kk
        ka_s[b] = kk * a
        w_s[b] = -jnp.exp(w)
        if nsteps < tile:
            y_s[b] = jnp.zeros((tile, 256), F32)
        g_s[b] = g
        bo_s[b] = bonus

    def step(t, carry):
        for b in range(nb):
            row = pl.ds(t, 1)
            S = sf_ref[b]
            sa = _split_dot(S * kk_s[b, row, :], wones, 2)
            vcol = _split_dot(idt * v_s[b, row, :], wones, 2)
            Sn = S * jnp.exp(w_s[b, row, :]) + sa * ka_s[b, row, :] + vcol * k_s[b, row, :]
            yb = jnp.dot((Sn * r_s[b, row, :]).astype(BF16), wones, preferred_element_type=F32)
            y_s[b, row, :] = jnp.sum(yb * idt, axis=0, keepdims=True)
            sf_ref[b] = Sn
        return carry

    if chunked:
        _rwkv_chunked(sf_ref, r_s, k_s, v_s, kk_s, ka_s, w_s, y_s, sbd_s, hm_ref[...], tri_ref[...],
                      ones_ref[...], strict_ref[...], incl_ref[...], bd_ref[...], nb, tile)
    else:
        lax.fori_loop(0, nsteps, step, 0)
    for b in range(nb):
        y = y_s[b]
        mu = _split_dot(y, wavg, 2)
        yc = y - mu
        var = _split_dot(yc * yc, wavg, 2)
        o_ref[b] = (yc * lax.rsqrt(var + RW_GN_EPS) * ng_ref[...] + bo_s[b]) * g_s[b]


def _rwkv(P3, s0, prev, w, consts, nb, tile, lvalid):
    batch, seq, _ = P3.shape
    nt = seq // tile
    nsteps = tile if lvalid >= seq else lvalid
    cs = [w["rw_mu"], w["rw_w0"], w["rw_a0"], w["rw_w2h"], w["rw_w2l"], w["rw_a2h"], w["rw_a2l"],
          w["rw_g2h"], w["rw_g2l"], w["rw_kk"], w["rw_ka"], w["rw_rk"], w["rw_ng"],
          consts["wones"], consts["wavg"], consts["idt"], consts["hm8"], consts["rw_tri"], consts["rw_ones"],
          consts["rw_strict"], consts["rw_incl"], consts["bd256"]]
    chunked = nsteps == tile and tile % RW_SC == 0

    def col(off, wd):
        return pl.BlockSpec((nb, tile, wd), lambda g, j: (g, j, off // wd))

    big = lambda: pltpu.VMEM((nb, tile, 256), F32)
    return pl.pallas_call(
        functools.partial(_rwkv_kernel, nb=nb, tile=tile, nsteps=nsteps, chunked=chunked),
        grid=(batch // nb, nt),
        in_specs=[col(C_RW, 256), col(C_RW + 256, 256), col(C_RW + 512, 256), col(C_RWLO, 128),
                  pl.BlockSpec((nb, RW_N, 256), lambda g, j: (g, 0, 0)),
                  pl.BlockSpec((nb, 1, RW_COLS), lambda g, j: (g, 0, 0))] + [_const_spec(a) for a in cs],
        out_specs=[pl.BlockSpec((nb, tile, 256), lambda g, j: (g, j, 0)),
                   pl.BlockSpec((nb, RW_N, 256), lambda g, j: (g, 0, 0))],
        out_shape=[jax.ShapeDtypeStruct((batch, seq, 256), F32),
                   jax.ShapeDtypeStruct((batch, RW_N, 256), F32)],
        scratch_shapes=[big(), big(), big(), big(), big(), big(), big(), big(), big(),
                        pltpu.VMEM((nb, 1, RW_COLS), F32),
                        pltpu.VMEM((nb, RW_N, 256) if chunked else (1, 8, 128), F32)],
        compiler_params=_cparams(2), name="rwkv",
    )(P3, P3, P3, P3, s0, prev, *cs)


def _merge_kernel(x_ref, sc_ref, sh_ref, gt_ref, g_ref, og_ref, oa_ref, os_ref, or_ref,
                  wg_ref, wbr_ref, wo_ref, o_ref):
    x = x_ref[...]
    h = _norm_mod(x, g_ref[...], sc_ref[...], sh_ref[...]).astype(BF16)
    merged = None
    for b, oref in enumerate((og_ref, oa_ref, os_ref, or_ref)):
        gate = _sigmoid(jnp.dot(h, wg_ref[:, b * D_MODEL:(b + 1) * D_MODEL], preferred_element_type=F32))
        proj = jnp.dot(oref[...].astype(BF16), wbr_ref[b], preferred_element_type=F32)
        merged = gate * proj if merged is None else merged + gate * proj
    y = jnp.dot(merged.astype(BF16), wo_ref[...], preferred_element_type=F32)
    o_ref[...] = x + gt_ref[...] * y


def _merge(x, mod, g, outs, w, tm, seq):
    rows = x.shape[0]
    row256 = pl.BlockSpec((tm, 256), lambda i: (i, 0))
    return pl.pallas_call(
        _merge_kernel,
        grid=(rows // tm,),
        in_specs=[pl.BlockSpec((tm, D_MODEL), lambda i: (i, 0)),
                  _mod_spec(mod, 1, tm, seq), _mod_spec(mod, 0, tm, seq), _mod_spec(mod, 2, tm, seq),
                  _const_spec(g), row256, row256, row256, row256,
                  _const_spec(w["w_gates"]), _const_spec(w["w_br"]), _const_spec(w["w_o"])],
        out_specs=pl.BlockSpec((tm, D_MODEL), lambda i: (i, 0)),
        out_shape=jax.ShapeDtypeStruct((rows, D_MODEL), F32),
        compiler_params=_cparams(1), name="merge",
    )(x, mod, mod, mod, g, *outs, w["w_gates"], w["w_br"], w["w_o"])


def _ffn_kernel(x_ref, sc_ref, sh_ref, gt_ref, g_ref, w1_ref, w2_ref, o_ref):
    x = x_ref[...]
    h = _norm_mod(x, g_ref[...], sc_ref[...], sh_ref[...]).astype(BF16)
    acc = None
    for c in range(D_FF // D_MODEL):
        cs = slice(c * D_MODEL, (c + 1) * D_MODEL)
        u = jnp.maximum(jnp.dot(h, w1_ref[:, cs], preferred_element_type=F32), 0.0)
        d = jnp.dot((u * u).astype(BF16), w2_ref[cs, :], preferred_element_type=F32)
        acc = d if acc is None else acc + d
    o_ref[...] = x + gt_ref[...] * acc


def _ffn(x, mod, g, w, tm, seq):
    rows = x.shape[0]
    return pl.pallas_call(
        _ffn_kernel,
        grid=(rows // tm,),
        in_specs=[pl.BlockSpec((tm, D_MODEL), lambda i: (i, 0)),
                  _mod_spec(mod, 4, tm, seq), _mod_spec(mod, 3, tm, seq), _mod_spec(mod, 5, tm, seq),
                  _const_spec(g), _const_spec(w["w_ff1"]), _const_spec(w["w_ff2"])],
        out_specs=pl.BlockSpec((tm, D_MODEL), lambda i: (i, 0)),
        out_shape=jax.ShapeDtypeStruct((rows, D_MODEL), F32),
        compiler_params=_cparams(1), name="ffn",
    )(x, mod, mod, mod, g, w["w_ff1"], w["w_ff2"])


def _ds_scores_kernel(pt_ref, q8_ref, w8_ref, kcur_ref, *refs, npages):
    pages, o_ref = refs[:npages], refs[npages]
    q8 = q8_ref[...]
    q8b = q8.astype(BF16)
    w8 = w8_ref[...] * (IDX_D ** -0.5)
    for p in range(npages):
        s = jnp.dot(q8b, pages[p][...].astype(BF16), preferred_element_type=F32)
        o_ref[p:p + 1, :] = jnp.sum(w8 * jnp.maximum(s, 0.0), axis=0, keepdims=True)
    s_cur = jnp.sum(q8 * kcur_ref[...], axis=-1, keepdims=True)
    i_cur = jnp.sum(w8 * jnp.maximum(s_cur, 0.0), axis=0, keepdims=True)
    lane = lax.broadcasted_iota(I32, (8, 128), 1)
    rowi = lax.broadcasted_iota(I32, (8, 128), 0)
    o_ref[npages:npages + 8, :] = jnp.where((lane == 0) & (rowi == 0), i_cur, -jnp.inf)


def _ds_scores(page_table, q8, w8, kcur, cache_idx, layer):
    n, npages = page_table.shape
    page_specs = [pl.BlockSpec((None, None, IDX_D, PAGE_SIZE), lambda i, pt, p=p: (layer, pt[i, p], 0, 0))
                  for p in range(npages)]
    return pl.pallas_call(
        functools.partial(_ds_scores_kernel, npages=npages),
        grid_spec=pltpu.PrefetchScalarGridSpec(
            num_scalar_prefetch=1, grid=(n,),
            in_specs=[pl.BlockSpec((None, IDX_H, IDX_D), lambda i, pt: (i, 0, 0)),
                      pl.BlockSpec((None, IDX_H, 1), lambda i, pt: (i, 0, 0)),
                      pl.BlockSpec((None, 1, IDX_D), lambda i, pt: (i, 0, 0))] + page_specs,
            out_specs=pl.BlockSpec((None, npages + 8, 128), lambda i, pt: (i, 0, 0))),
        out_shape=jax.ShapeDtypeStruct((n, npages + 8, 128), F32),
        compiler_params=_cparams(1), name="ds_scores",
    )(page_table, q8, w8, kcur, *([cache_idx] * npages))


def _ds_select_kernel(s_ref, o_ref, key_ref, *, topk, nvalid):
    rows, cols = s_ref.shape
    col = lax.broadcasted_iota(I32, (rows, cols), 1)
    valid = col < nvalid
    key_ref[...] = _score_keys(s_ref[...], valid)
    o_ref[...] = jnp.where(valid, _topk_select(key_ref, topk, col), 0.0)


def _ds_select(scores, topk, nvalid):
    rows, cols = scores.shape
    return pl.pallas_call(
        functools.partial(_ds_select_kernel, topk=topk, nvalid=nvalid),
        grid=(1,),
        in_specs=[pl.BlockSpec((rows, cols), lambda i: (0, 0))],
        out_specs=pl.BlockSpec((rows, cols), lambda i: (0, 0)),
        out_shape=jax.ShapeDtypeStruct((rows, cols), F32),
        scratch_shapes=[pltpu.VMEM((rows, cols), I32)],
        compiler_params=_cparams(1), name="ds_select",
    )(scores)


def _ds_attn_kernel(pt_ref, q_ref, kcur_ref, vcur_ref, m_ref, hm_ref, *refs, npages):
    kp, vp, o_ref = refs[:npages], refs[npages:2 * npages], refs[2 * npages]
    hm = hm_ref[...]
    qf = q_ref[...] * hm
    qb = qf.astype(BF16)
    sc = HD ** -0.5
    lg_cur = jnp.sum(qf * kcur_ref[...], axis=-1, keepdims=True) * sc
    cur_sel = m_ref[npages:npages + 1, 0:1] > 0.0
    mx = lg_cur
    lgs = []
    for p in range(npages):
        lg = jnp.dot(qb, kp[p][...].astype(BF16), preferred_element_type=F32) * sc
        lg = jnp.where(m_ref[p:p + 1, :] > 0.0, lg, -jnp.inf)
        lgs.append(lg)
        mx = jnp.maximum(mx, jnp.max(lg, axis=-1, keepdims=True))
    pc = jnp.where(cur_sel, jnp.exp(lg_cur - mx), 0.0)
    l = pc
    acc = pc * vcur_ref[...]
    for p in range(npages):
        pe = jnp.exp(lgs[p] - mx)
        l = l + jnp.sum(pe, axis=-1, keepdims=True)
        acc = acc + lax.dot_general(pe.astype(BF16), vp[p][...].astype(BF16), _LANES,
                                    preferred_element_type=F32)
    o_ref[...] = jnp.sum((acc / l) * hm, axis=0, keepdims=True)


def _ds_attn(page_table, q, kcur, vcur, mask, cache_k, cache_v, layer, headmask):
    n, npages = page_table.shape
    pspec = lambda p: pl.BlockSpec((None, None, 256, PAGE_SIZE), lambda i, pt, p=p: (layer, pt[i, p], 0, 0))
    row = pl.BlockSpec((None, 1, 256), lambda i, pt: (i, 0, 0))
    return pl.pallas_call(
        functools.partial(_ds_attn_kernel, npages=npages),
        grid_spec=pltpu.PrefetchScalarGridSpec(
            num_scalar_prefetch=1, grid=(n,),
            in_specs=[row, row, row,
                      pl.BlockSpec((None, npages + 8, 128), lambda i, pt: (i, 0, 0)),
                      pl.BlockSpec((8, 256), lambda i, pt: (0, 0))]
                     + [pspec(p) for p in range(npages)] + [pspec(p) for p in range(npages)],
            out_specs=row),
        out_shape=jax.ShapeDtypeStruct((n, 1, 256), F32),
        compiler_params=_cparams(1), name="ds_attn",
    )(page_table, q, kcur, vcur, mask, headmask, *([cache_k] * npages), *([cache_v] * npages))


def _constants(gla_keys):
    lane256 = np.arange(256)
    head = lane256 // 64
    wones = (head[:, None] == head[None, :]).astype(np.float32)
    idt = (np.arange(64)[:, None] == (lane256 % 64)[None, :]).astype(np.float32)
    e2 = ((np.arange(128) // 32)[:, None] == head[None, :]).astype(np.float32)
    bd = (head[:, None] == (np.arange(128) // 32)[None, :]).astype(np.float32)
    hm8 = (np.arange(8)[:, None] == head[None, :]).astype(np.float32)
    hm128 = (np.arange(8)[:, None] == (np.arange(128) // 32)[None, :]).astype(np.float32)

    def chunk_masks(tile, chunk, heads):
        t = np.arange(tile)
        same = (t[:, None] // chunk) == (t[None, :] // chunk)
        rows = np.arange(heads * tile)
        amask = ((rows // (heads * chunk))[:, None] == (t // chunk)[None, :]) & \
                ((t % chunk)[None, :] <= (rows % chunk)[:, None])
        return dict(tri=jnp.asarray(same & (t[None, :] <= t[:, None]), BF16), ones=jnp.asarray(same, BF16),
                    amask=jnp.asarray(amask, F32))

    gla = {key: chunk_masks(key[0], key[1], GLA_H) for key in gla_keys}
    rwm = chunk_masks(RW_SC, RW_CH, RW_H)
    i256 = np.arange(256)
    same16 = (i256[:, None] // RW_CH) == (i256[None, :] // RW_CH)
    strict = same16 & ((i256 % RW_CH)[None, :] < (i256 % RW_CH)[:, None])
    incl = same16 & ((i256 % RW_CH)[None, :] <= (i256 % RW_CH)[:, None])

    def left(width, group, half):
        return jnp.asarray(((np.arange(width) % group) < half).astype(np.float32)[None, :])

    return dict(wones=jnp.asarray(wones, BF16), wavg=jnp.asarray(wones / 64.0, BF16), idt=jnp.asarray(idt),
                e2=jnp.asarray(e2, BF16), bd=jnp.asarray(bd), hm8=jnp.asarray(hm8), hm128=jnp.asarray(hm128),
                gla=gla, rw_tri=rwm["tri"], rw_ones=rwm["ones"], rw_strict=jnp.asarray(strict, F32),
                rw_incl=jnp.asarray(incl, F32), bd256=jnp.asarray(wones, F32),
                left_q=left(256, HD, ROT // 2), left_i=left(256, IDX_D, IDX_ROT // 2),
                left_kw=left(128, 128, IDX_ROT // 2))


def _rope_tables(pos, periodic):
    pos = pos.astype(F32)[:, None]

    def build(width, group, rot, extra=None):
        half = rot // 2
        freq = ROPE_THETA ** (-jnp.arange(half, dtype=F32) * (2.0 / rot))
        ang = pos * freq
        cos, sin = jnp.cos(ang), jnp.sin(ang)
        n = pos.shape[0]
        ones = jnp.ones((n, group - rot), F32)
        zeros = jnp.zeros((n, group - rot), F32)
        cg = jnp.concatenate([cos, cos, ones], axis=1)
        sg = jnp.concatenate([-sin, sin, zeros], axis=1)
        reps = width // group
        c, s = jnp.tile(cg, (1, reps)), jnp.tile(sg, (1, reps))
        if extra is not None:
            c, s = extra(c, s)
        return c, s

    cq, sq = build(256, HD, ROT)
    ci, si = build(256, IDX_D, IDX_ROT)

    def kw_extra(c, s):
        lane = jnp.arange(128)
        scale = jnp.where((lane >= IDX_D) & (lane < IDX_D + IDX_H), IDX_H ** -0.5, 1.0)
        keep = (lane < IDX_D)
        return jnp.where(keep, c, scale[None, :]), jnp.where(keep, s, 0.0)

    ckw, skw = build(128, IDX_D, IDX_ROT, kw_extra)
    return dict(cq=cq, sq=sq, ci=ci, si=si, ckw=ckw, skw=skw, periodic=periodic)


def _blockdiag(blocks):
    g, r, c = blocks.shape
    eye = jnp.eye(g, dtype=blocks.dtype)
    return jnp.einsum('grc,gh->grhc', blocks, eye).reshape(g * r, g * c)


def _layer_weights(l, p):
    w_in = p["w_in"][l]
    z = lambda n: jnp.zeros((D_MODEL, n), F32)
    w_mix = jnp.concatenate([
        w_in[:, 256:512], w_in[:, 528:784], w_in[:, 0:256],
        w_in[:, 784:1808], w_in[:, 1848:2104], w_in[:, 2104:3000],
        w_in[:, 512:528], z(112), w_in[:, 1808:1848], z(88)], axis=1).astype(BF16)
    w = dict(w_mix=w_mix, w_gates=w_in[:, 3000:7096].astype(BF16),
             w_br=p["w_br"][l].astype(BF16), w_o=p["w_o"][l].astype(BF16),
             w_ff1=p["w_ff1"][l].astype(BF16), w_ff2=p["w_ff2"][l].astype(BF16),
             norm1_g=p["norm1_g"][l][None, :], norm2_g=p["norm2_g"][l][None, :])
    a2 = jnp.zeros((128, 128), F32).at[0:GLA_RANK].set(p["gla_a2"][l])
    w["gla_a2h"], w["gla_a2l"] = _hilo(a2)
    w["gla_ab"] = p["gla_ab"][l][None, :]
    w["gla_ng"] = jnp.tile(p["gla_ng"][l], GLA_H)[None, :]
    w["att_qg"] = jnp.tile(p["att_qg"][l], ATT_H)[None, :]
    w["att_kg"] = jnp.tile(p["att_kg"][l], ATT_H)[None, :]
    dt = jnp.exp(p["s5_log_dt"][l])[:, None]
    lr = jnp.minimum(p["s5_a_re"][l], -1e-4)
    li = p["s5_a_im"][l]
    mag = jnp.exp(lr * dt)
    abr, abi = mag * jnp.cos(li * dt), mag * jnp.sin(li * dt)
    den = lr * lr + li * li
    fr = ((abr - 1.0) * lr + abi * li) / den
    fi = (abi * lr - (abr - 1.0) * li) / den
    b_re, b_im = p["s5_b_re"][l], p["s5_b_im"][l]
    bbr = fr[..., None] * b_re - fi[..., None] * b_im
    bbi = fr[..., None] * b_im + fi[..., None] * b_re
    bmat = jnp.concatenate([_blockdiag(bbr.transpose(0, 2, 1)), _blockdiag(bbi.transpose(0, 2, 1))], axis=1)
    w["s5_bh"] = bmat.astype(BF16)
    w["s5_a"] = jnp.concatenate([abr.reshape(1, -1), abi.reshape(1, -1)], axis=1)
    w["s5_c"] = jnp.concatenate([_blockdiag(p["s5_c_re"][l].transpose(0, 2, 1)),
                                 -_blockdiag(p["s5_c_im"][l].transpose(0, 2, 1))], axis=0).astype(BF16)
    w["s5_d"] = p["s5_d"][l].reshape(1, -1)
    w["s5_gw"] = p["s5_glu_w"][l].astype(BF16)
    w["s5_gb"] = p["s5_glu_b"][l][None, :]
    w["rw_mu"] = p["rw_mu"][l][None, :]
    w["rw_w0"] = p["rw_w0"][l][None, :]
    w["rw_a0"] = p["rw_a0"][l][None, :]
    lo = jnp.zeros((128, 256), F32)
    w["rw_w2h"], w["rw_w2l"] = _hilo(lo.at[0:RW_WR].set(p["rw_w2"][l]))
    w["rw_a2h"], w["rw_a2l"] = _hilo(lo.at[RW_WR:RW_WR + RW_AR].set(p["rw_a2"][l]))
    w["rw_g2h"], w["rw_g2l"] = _hilo(lo.at[RW_WR + RW_AR:128].set(p["rw_g2"][l]))
    for nm in ("rw_kk", "rw_ka", "rw_rk", "rw_ng"):
        w[nm] = p[nm][l][None, :]
    return w


def _mix_and_ffn(x, mod, w, consts, P, o_att, st_gla0, st_s50, st_rw0, prev, batch, seq, lvalid, tm, tiles):
    o_gla, st_gla = _gla(P, st_gla0, w, consts, batch, seq, tiles["gla"], tiles["chunk"], lvalid)
    if lvalid == 1:
        o1, st1 = _s5_step(P.reshape(batch, seq, NP_COLS)[:, 0], st_s50[:, 0], w)
        o_s5 = jnp.pad(o1[:, None], ((0, 0), (0, seq - 1), (0, 0))).reshape(batch * seq, 256)
        st_s5 = st1[:, None]
    else:
        o_s5, st_s5 = _s5(P, st_s50, w, batch, seq, tiles["s5"], lvalid)
    o_rw, st_rw = _rwkv(P.reshape(batch, seq, NP_COLS), st_rw0, prev, w, consts, tiles["nb"], tiles["rw"], lvalid)
    x = _merge(x, mod, w["norm1_g"], (o_gla, o_att, o_s5, o_rw.reshape(batch * seq, 256)), w, tm, seq)
    x = _ffn(x, mod, w["norm2_g"], w, tm, seq)
    return x, st_gla, st_s5, st_rw


def _gla_state_out(st):
    n = st.shape[0]
    s = st.reshape(n, GLA_H, GLA_DV, GLA_H, GLA_DK)
    s = jnp.stack([s[:, h, :, h, :] for h in range(GLA_H)], axis=1)
    return s.transpose(0, 1, 3, 2)


def _gla_state_in(s):
    eye = jnp.eye(GLA_H, dtype=s.dtype)
    n = s.shape[0]
    return jnp.einsum('nhkv,hg->nhvgk', s, eye).reshape(n, GLA_H * GLA_DV, GLA_H * GLA_DK)


def _rw_state_out(st):
    n = st.shape[0]
    return st.reshape(n, RW_N, RW_H, RW_N).transpose(0, 2, 1, 3)


def _rw_state_in(s):
    n = s.shape[0]
    return s.transpose(0, 2, 1, 3).reshape(n, RW_N, RW_H * RW_N)


def _forward(x_prompt, x_sample, c_prompt, c_sample, cache_k, cache_v, cache_idx, state_gla,
             state_s5_re, state_s5_im, state_rwkv, state_shift, page_table, p):
    B, S, _ = x_prompt.shape
    N = x_sample.shape[0]
    depth = p["w_in"].shape[0]
    past = page_table.shape[1] * PAGE_SIZE
    tm_p = min(512, S)
    tm_s = min(128, N * SAMPLE_PAD)
    tiles_p = dict(gla=min(256, S), chunk=16, s5=min(512, S), rw=min(128, S), nb=min(8, B))
    tiles_s = dict(gla=min(128, N * SAMPLE_PAD), chunk=SAMPLE_PAD, s5=SAMPLE_PAD, rw=SAMPLE_PAD, nb=8)
    tq = min(128, S)
    consts = _constants({(t["gla"], t["chunk"]) for t in (tiles_p, tiles_s)})
    mod_all = _modulation(jnp.concatenate([c_prompt, c_sample], axis=0), p["ada_w"], p["ada_b"])
    tabs_p = _rope_tables(jnp.arange(S), True)
    tabs_s = _rope_tables(jnp.full((N * SAMPLE_PAD,), past), False)
    ck = cache_k.transpose(0, 1, 3, 4, 2).reshape(*cache_k.shape[:2], ATT_H * HD, PAGE_SIZE)
    cv = cache_v.transpose(0, 1, 3, 4, 2).reshape(*cache_v.shape[:2], ATT_H * HD, PAGE_SIZE)
    ci = cache_idx.transpose(0, 1, 3, 2)

    xp = x_prompt.reshape(B * S, D_MODEL)
    xs = jnp.pad(x_sample, ((0, 0), (0, SAMPLE_PAD - 1), (0, 0))).reshape(N * SAMPLE_PAD, D_MODEL)
    outs_p, outs_s = [], []
    for l in range(depth):
        w = _layer_weights(l, p)
        mod = mod_all[l, :B].reshape(B, 1, 6 * D_MODEL)
        P = _inproj(xp, mod, w["norm1_g"], w["w_mix"], tm_p, S)
        qn, kn, qir, kwr, qh, kh, vh, qih, kib = _dsa_prep(P, tabs_p, consts, w["att_qg"], w["att_kg"], tm_p, S)
        o_att = _dsa_attn(qh, kh, vh, qih, kwr, kib, B, S, tq)
        xp, st_gla, st_s5, st_rw = _mix_and_ffn(
            xp, mod, w, consts, P, o_att,
            jnp.zeros((B, 256, 128), F32), jnp.zeros((B, 1, 2 * S5_N), F32),
            jnp.zeros((B, RW_N, 256), F32), jnp.zeros((B, 1, RW_COLS), F32), B, S, S, tm_p, tiles_p)
        P3 = P.reshape(B, S, NP_COLS)
        outs_p.append((kn.reshape(B, S, ATT_H, HD), P3[:, :, C_DV:C_DV + 256].reshape(B, S, ATT_H, HD),
                       kwr.reshape(B, S, 128)[:, :, :IDX_D], _gla_state_out(st_gla),
                       st_s5[:, 0, :S5_N].reshape(B, S5_G, S5_P), st_s5[:, 0, S5_N:].reshape(B, S5_G, S5_P),
                       _rw_state_out(st_rw), P3[:, S - 1, C_RW:C_RW + RW_COLS]))
        mod = jnp.repeat(mod_all[l, B:], SAMPLE_PAD, axis=0)
        P = _inproj(xs, mod, w["norm1_g"], w["w_mix"], tm_s, SAMPLE_PAD)
        qn, kn, qir, kwr = _dsa_prep(P, tabs_s, consts, w["att_qg"], w["att_kg"], tm_s, SAMPLE_PAD)[:4]
        first = lambda a: a.reshape(N, SAMPLE_PAD, a.shape[-1])[:, 0]
        qn1, kn1, qir1, kwr1, P1 = first(qn), first(kn), first(qir), first(kwr), first(P)
        v1 = P1[:, C_DV:C_DV + 256]
        scores = _ds_scores(page_table, qir1.reshape(N, IDX_H, IDX_D),
                            kwr1[:, IDX_D:IDX_D + IDX_H].reshape(N, IDX_H, 1),
                            kwr1[:, :IDX_D].reshape(N, 1, IDX_D), ci, l)
        ncols = scores.shape[1] * 128
        sel = _ds_select(scores.reshape(N, ncols), min(TOPK_MAX, (past + 1) // 4), past + 1)
        o1 = _ds_attn(page_table, qn1.reshape(N, 1, 256), kn1.reshape(N, 1, 256), v1.reshape(N, 1, 256),
                      sel.reshape(N, ncols // 128, 128), ck, cv, l, consts["hm8"])
        o_att = jnp.pad(o1, ((0, 0), (0, SAMPLE_PAD - 1), (0, 0))).reshape(N * SAMPLE_PAD, 256)
        x0 = jnp.concatenate([state_s5_re[l].reshape(N, 1, S5_N), state_s5_im[l].reshape(N, 1, S5_N)], axis=2)
        xs, st_gla, st_s5, st_rw = _mix_and_ffn(
            xs, mod, w, consts, P, o_att, _gla_state_in(state_gla[l]), x0, _rw_state_in(state_rwkv[l]),
            state_shift[l].reshape(N, 1, RW_COLS), N, SAMPLE_PAD, 1, tm_s, tiles_s)
        outs_s.append((kn1.reshape(N, 1, ATT_H, HD), v1.reshape(N, 1, ATT_H, HD), kwr1[:, None, :IDX_D],
                       _gla_state_out(st_gla), st_s5[:, 0, :S5_N].reshape(N, S5_G, S5_P),
                       st_s5[:, 0, S5_N:].reshape(N, S5_G, S5_P), _rw_state_out(st_rw),
                       P1[:, C_RW:C_RW + RW_COLS]))
    yp = xp.reshape(B, S, D_MODEL)
    ys = xs.reshape(N, SAMPLE_PAD, D_MODEL)[:, 0:1]
    stack = lambda lst, i: jnp.stack([s[i] for s in lst])
    return (yp, ys) + tuple(stack(outs_p, i) for i in range(8)) + tuple(stack(outs_s, i) for i in range(8))


def kernel(x_prompt, x_sample, c_prompt, c_sample, cache_k, cache_v, cache_idx, state_gla, state_s5_re, state_s5_im, state_rwkv, state_shift, page_table, ada_w, ada_b, norm1_g, norm2_g, w_in, gla_a2, gla_ab, gla_ng, att_qg, att_kg, s5_a_re, s5_a_im, s5_log_dt, s5_b_re, s5_b_im, s5_c_re, s5_c_im, s5_d, s5_glu_w, s5_glu_b, rw_mu, rw_w0, rw_w2, rw_a0, rw_a2, rw_g2, rw_kk, rw_ka, rw_rk, rw_ng, w_br, w_o, w_ff1, w_ff2):
    p = dict(ada_w=ada_w, ada_b=ada_b, norm1_g=norm1_g, norm2_g=norm2_g, w_in=w_in, gla_a2=gla_a2,
             gla_ab=gla_ab, gla_ng=gla_ng, att_qg=att_qg, att_kg=att_kg, s5_a_re=s5_a_re, s5_a_im=s5_a_im,
             s5_log_dt=s5_log_dt, s5_b_re=s5_b_re, s5_b_im=s5_b_im, s5_c_re=s5_c_re, s5_c_im=s5_c_im,
             s5_d=s5_d, s5_glu_w=s5_glu_w, s5_glu_b=s5_glu_b, rw_mu=rw_mu, rw_w0=rw_w0, rw_w2=rw_w2,
             rw_a0=rw_a0, rw_a2=rw_a2, rw_g2=rw_g2, rw_kk=rw_kk, rw_ka=rw_ka, rw_rk=rw_rk, rw_ng=rw_ng,
             w_br=w_br, w_o=w_o, w_ff1=w_ff1, w_ff2=w_ff2)
    return _forward(x_prompt, x_sample, c_prompt, c_sample, cache_k, cache_v, cache_idx, state_gla,
                    state_s5_re, state_s5_im, state_rwkv, state_shift, page_table, p)
```
